```python
import jax, jax.numpy as jnp
from jax import lax
import numpy as np

D_MODEL = 1024
BATCH = 8
SEQ = 2048
DEPTH = 4
DEC_BATCH = 2
DEC_SEQ = 8192
PAST_LEN = 128

EPS = 1e-6
NEG_INF = -1e30
HEAD_DIM = 64

SWA_HEADS = 6
SWA_KV_HEADS = 2
SWA_GROUP = SWA_HEADS // SWA_KV_HEADS
WINDOW = 128
BLOCK = WINDOW

MLA_HEADS = 6
MLA_Q_RANK = 256
MLA_KV_RANK = 128
MLA_NOPE = 64
MLA_ROPE = 32
MLA_QK = MLA_NOPE + MLA_ROPE
MLA_V = 64
ROPE_BASE = 10000.0
Q_BLOCK = 128

CONV_CH = 256
CONV_GROUPS = 4
CONV_WIDTH = 31
CONV_PAD = (CONV_WIDTH - 1) // 2

A_WIDTH = SWA_HEADS * HEAD_DIM
B_WIDTH = MLA_HEADS * MLA_V
C_WIDTH = CONV_CH
MIX_WIDTH = A_WIDTH + B_WIDTH + C_WIDTH

A_Q = SWA_HEADS * HEAD_DIM
A_KV = SWA_KV_HEADS * HEAD_DIM
A_IN = A_Q + 2 * A_KV
B_IN = MLA_Q_RANK + MLA_KV_RANK + MLA_ROPE
C_IN = 2 * CONV_CH
IN_WIDTH = A_IN + B_IN + C_IN

N_GROUPS = 4
EXPERTS_PER_GROUP = 8
N_EXPERTS = N_GROUPS * EXPERTS_PER_GROUP
TOP_K = 2
D_EXPERT = 256
N_MOD = 6

kernel_name = 'hybrid_bidir_encoder'


def rmsnorm(x, g):
    xf = x.astype(jnp.float32)
    y = xf * lax.rsqrt(jnp.mean(xf * xf, axis=-1, keepdims=True) + EPS)
    return (y * g.astype(jnp.float32)).astype(x.dtype)


def alibi_slopes(n):
    return jnp.asarray(2.0 ** (-8.0 * np.arange(1, n + 1) / n), dtype=jnp.float32)


def windowed_gqa(q, k, v, q_gain, k_gain, sink):
    b, s = q.shape[0], q.shape[1]
    nb = s // BLOCK
    q = rmsnorm(q, q_gain)
    k = rmsnorm(k, k_gain)
    pad = ((0, 0), (BLOCK, BLOCK), (0, 0), (0, 0))
    kb = jnp.pad(k, pad).reshape(b, nb + 2, BLOCK, SWA_KV_HEADS, HEAD_DIM)
    vb = jnp.pad(v, pad).reshape(b, nb + 2, BLOCK, SWA_KV_HEADS, HEAD_DIM)
    kw = jnp.concatenate([kb[:, :-2], kb[:, 1:-1], kb[:, 2:]], axis=2)
    vw = jnp.concatenate([vb[:, :-2], vb[:, 1:-1], vb[:, 2:]], axis=2)
    qb = q.reshape(b, nb, BLOCK, SWA_KV_HEADS, SWA_GROUP, HEAD_DIM)
    scores = jnp.einsum('bnqhgd,bnkhd->bnhgqk', qb, kw).astype(jnp.float32) * (HEAD_DIM ** -0.5)
    rel = jnp.abs(jnp.arange(3 * BLOCK)[None, :] - BLOCK - jnp.arange(BLOCK)[:, None])
    key_pos = jnp.arange(nb)[:, None] * BLOCK - BLOCK + jnp.arange(3 * BLOCK)[None, :]
    mask = (rel <= WINDOW)[None] & ((key_pos >= 0) & (key_pos < s))[:, None, :]
    slopes = alibi_slopes(SWA_HEADS).reshape(SWA_KV_HEADS, SWA_GROUP)
    scores = scores - slopes[:, :, None, None] * rel.astype(jnp.float32)
    scores = jnp.where(mask[None, :, None, None], scores, NEG_INF)
    sink_l = sink.astype(jnp.float32).reshape(1, 1, SWA_KV_HEADS, SWA_GROUP, 1, 1)
    m = jnp.maximum(jnp.max(scores, axis=-1, keepdims=True), sink_l)
    p = jnp.exp(scores - m)
    p = p / (jnp.sum(p, axis=-1, keepdims=True) + jnp.exp(sink_l - m))
    out = jnp.einsum('bnhgqk,bnkhd->bnqhgd', p.astype(v.dtype), vw)
    return out.reshape(b, s, A_WIDTH)


def rope_tables(seq_len):
    inv_freq = 1.0 / (ROPE_BASE ** (jnp.arange(0, MLA_ROPE, 2, dtype=jnp.float32) / MLA_ROPE))
    ang = jnp.arange(seq_len, dtype=jnp.float32)[:, None] * inv_freq[None, :]
    return jnp.cos(ang), jnp.sin(ang)


def apply_rope(x, cos, sin):
    c = cos[:, None, :].astype(x.dtype)
    sn = sin[:, None, :].astype(x.dtype)
    x1, x2 = jnp.split(x, 2, axis=-1)
    return jnp.concatenate([x1 * c - x2 * sn, x1 * sn + x2 * c], axis=-1)


def latent_attention(cq, ckv, k_rope, q_norm_g, w_q_up, kv_norm_g, w_kv_up, q_gain, k_gain):
    b, s = cq.shape[0], cq.shape[1]
    q = (rmsnorm(cq, q_norm_g) @ w_q_up).reshape(b, s, MLA_HEADS, MLA_QK)
    kv = (rmsnorm(ckv, kv_norm_g) @ w_kv_up).reshape(b, s, MLA_HEADS, MLA_NOPE + MLA_V)
    k_nope, v = kv[..., :MLA_NOPE], kv[..., MLA_NOPE:]
    k_r = jnp.broadcast_to(k_rope[:, :, None, :], (b, s, MLA_HEADS, MLA_ROPE))
    k = jnp.concatenate([k_nope, k_r], axis=-1)
    q = rmsnorm(q, q_gain)
    k = rmsnorm(k, k_gain)
    cos, sin = rope_tables(s)
    q = jnp.concatenate([q[..., :MLA_NOPE], apply_rope(q[..., MLA_NOPE:], cos, sin)], axis=-1)
    k = jnp.concatenate([k[..., :MLA_NOPE], apply_rope(k[..., MLA_NOPE:], cos, sin)], axis=-1)
    nb = s // Q_BLOCK
    qb = jnp.moveaxis(q.reshape(b, nb, Q_BLOCK, MLA_HEADS, MLA_QK), 1, 0)
    scale = MLA_QK ** -0.5

    def attend_block(q_blk):
        sc = jnp.einsum('bqhd,bkhd->bhqk', q_blk, k).astype(jnp.float32) * scale
        p = jax.nn.softmax(sc, axis=-1)
        return jnp.einsum('bhqk,bkhd->bqhd', p.astype(v.dtype), v)

    out = lax.map(attend_block, qb)
    return jnp.moveaxis(out, 0, 1).reshape(b, s, B_WIDTH)


def conformer_conv(u, dw_w, dw_b, gn_g, gn_b, w_pw2, b_pw2):
    b, s = u.shape[0], u.shape[1]
    a, gate = jnp.split(u, 2, axis=-1)
    h = a * jax.nn.sigmoid(gate)
    h = lax.conv_general_dilated(h, dw_w, window_strides=(1,), padding=[(CONV_PAD, CONV_PAD)],
                                 dimension_numbers=('NWC', 'WIO', 'NWC'),
                                 feature_group_count=CONV_CH) + dw_b
    hg = h.astype(jnp.float32).reshape(b, s, CONV_GROUPS, CONV_CH // CONV_GROUPS)
    mu = jnp.mean(hg, axis=-1, keepdims=True)
    var = jnp.mean(jnp.square(hg - mu), axis=-1, keepdims=True)
    hn = ((hg - mu) * lax.rsqrt(var + EPS)).reshape(b, s, CONV_CH)
    hn = (hn * gn_g.astype(jnp.float32) + gn_b.astype(jnp.float32)).astype(u.dtype)
    return jax.nn.silu(hn) @ w_pw2 + b_pw2


def hierarchical_moe(x, w_group, b_group, w_expert, b_expert, w_gate, w_up, w_down):
    b, s, d = x.shape
    t = x.reshape(b * s, d)
    g_prob = jax.nn.softmax((t @ w_group).astype(jnp.float32) + b_group.astype(jnp.float32), axis=-1)
    g_val, g_idx = lax.top_k(g_prob, 1)
    e_logits = ((t @ w_expert).astype(jnp.float32) + b_expert.astype(jnp.float32)).reshape(
        -1, N_GROUPS, EXPERTS_PER_GROUP)
    sel = jax.nn.one_hot(g_idx[:, 0], N_GROUPS, dtype=jnp.float32)
    e_prob = jax.nn.softmax(jnp.einsum('tge,tg->te', e_logits, sel), axis=-1)
    e_val, e_idx = lax.top_k(e_prob, TOP_K)
    w = g_val * e_val / jnp.sum(e_val, axis=-1, keepdims=True)
    expert_id = g_idx * EXPERTS_PER_GROUP + e_idx
    combine = jnp.einsum('tk,tke->te', w, jax.nn.one_hot(expert_id, N_EXPERTS, dtype=jnp.float32)).astype(x.dtype)
    y = jnp.zeros_like(t)
    for gi in range(N_GROUPS):
        lo = gi * EXPERTS_PER_GROUP
        hi = lo + EXPERTS_PER_GROUP
        hg = jnp.einsum('td,edf->tef', t, w_gate[lo:hi])
        hu = jnp.einsum('td,edf->tef', t, w_up[lo:hi])
        act = jax.nn.silu(hg) * hu * combine[:, lo:hi, None]
        y = y + jnp.einsum('tef,efd->td', act, w_down[lo:hi])
    return y.reshape(b, s, d)


def encoder_layer(x, c, w_ada, b_ada, norm1_g, norm2_g, w_in,
                  swa_q_gain, swa_k_gain, swa_sink,
                  mla_q_norm_g, mla_w_q_up, mla_kv_norm_g, mla_w_kv_up, mla_q_gain, mla_k_gain,
                  conv_dw_w, conv_dw_b, conv_gn_g, conv_gn_b, conv_w_pw2, conv_b_pw2,
                  out_norm_a, out_norm_b, out_norm_c, w_out,
                  moe_w_group, moe_b_group, moe_w_expert, moe_b_expert, moe_w_gate, moe_w_up, moe_w_down):
    b, s = x.shape[0], x.shape[1]
    mod = (jax.nn.silu(c) @ w_ada + b_ada)[:, None, :]
    shift1, scale1, gate1, shift2, scale2, gate2 = jnp.split(mod, N_MOD, axis=-1)
    h = rmsnorm(x, norm1_g) * (1 + scale1) + shift1
    z = h @ w_in
    za = z[..., :A_IN]
    zb = z[..., A_IN:A_IN + B_IN]
    zc = z[..., A_IN + B_IN:]
    q = za[..., :A_Q].reshape(b, s, SWA_HEADS, HEAD_DIM)
    k = za[..., A_Q:A_Q + A_KV].reshape(b, s, SWA_KV_HEADS, HEAD_DIM)
    v = za[..., A_Q + A_KV:].reshape(b, s, SWA_KV_HEADS, HEAD_DIM)
    out_a = windowed_gqa(q, k, v, swa_q_gain, swa_k_gain, swa_sink)
    out_b = latent_attention(zb[..., :MLA_Q_RANK], zb[..., MLA_Q_RANK:MLA_Q_RANK + MLA_KV_RANK],
                             zb[..., MLA_Q_RANK + MLA_KV_RANK:], mla_q_norm_g, mla_w_q_up,
                             mla_kv_norm_g, mla_w_kv_up, mla_q_gain, mla_k_gain)
    out_c = conformer_conv(zc, conv_dw_w, conv_dw_b, conv_gn_g, conv_gn_b, conv_w_pw2, conv_b_pw2)
    merged = jnp.concatenate([rmsnorm(out_a, out_norm_a), rmsnorm(out_b, out_norm_b),
                              rmsnorm(out_c, out_norm_c)], axis=-1)
    x = x + gate1 * (merged @ w_out)
    h = rmsnorm(x, norm2_g) * (1 + scale2) + shift2
    x = x + gate2 * hierarchical_moe(h, moe_w_group, moe_b_group, moe_w_expert, moe_b_expert,
                                     moe_w_gate, moe_w_up, moe_w_down)
    return x


def setup_inputs(seed: int = 0) -> dict:
    key = jax.random.key(seed)
    keys = jax.random.split(key, 40)
    L = DEPTH

    def nrm(i, shape, scale):
        return jax.random.normal(keys[i], shape, dtype=jnp.float32) * scale

    def gain(i, shape):
        return 1.0 + nrm(i, shape, 0.05)

    return {
        'x_prompt': nrm(0, (BATCH, SEQ, D_MODEL), 1.0),
        'x_sample': nrm(1, (DEC_BATCH, DEC_SEQ, D_MODEL), 1.0),
        'c_prompt': nrm(2, (BATCH, D_MODEL), 1.0),
        'c_sample': nrm(3, (DEC_BATCH, D_MODEL), 1.0),
        'w_ada': nrm(4, (L, D_MODEL, N_MOD * D_MODEL), 0.5 * D_MODEL ** -0.5),
        'b_ada': nrm(5, (L, N_MOD * D_MODEL), 0.02),
        'norm1_g': gain(6, (L, D_MODEL)),
        'norm2_g': gain(7, (L, D_MODEL)),
        'w_in': nrm(8, (L, D_MODEL, IN_WIDTH), D_MODEL ** -0.5),
        'swa_q_gain': gain(9, (L, HEAD_DIM)),
        'swa_k_gain': gain(10, (L, HEAD_DIM)),
        'swa_sink': nrm(11, (L, SWA_HEADS), 0.5),
        'mla_q_norm_g': gain(12, (L, MLA_Q_RANK)),
        'mla_w_q_up': nrm(13, (L, MLA_Q_RANK, MLA_HEADS * MLA_QK), MLA_Q_RANK ** -0.5),
        'mla_kv_norm_g': gain(14, (L, MLA_KV_RANK)),
        'mla_w_kv_up': nrm(15, (L, MLA_KV_RANK, MLA_HEADS * (MLA_NOPE + MLA_V)), MLA_KV_RANK ** -0.5),
        'mla_q_gain': gain(16, (L, MLA_QK)),
        'mla_k_gain': gain(17, (L, MLA_QK)),
        'conv_dw_w': nrm(18, (L, CONV_WIDTH, 1, CONV_CH), CONV_WIDTH ** -0.5),
        'conv_dw_b': nrm(19, (L, CONV_CH), 0.02),
        'conv_gn_g': gain(20, (L, CONV_CH)),
        'conv_gn_b': nrm(21, (L, CONV_CH), 0.02),
        'conv_w_pw2': nrm(22, (L, CONV_CH, CONV_CH), CONV_CH ** -0.5),
        'conv_b_pw2': nrm(23, (L, CONV_CH), 0.02),
        'out_norm_a': gain(24, (L, A_WIDTH)),
        'out_norm_b': gain(25, (L, B_WIDTH)),
        'out_norm_c': gain(26, (L, C_WIDTH)),
        'w_out': nrm(27, (L, MIX_WIDTH, D_MODEL), MIX_WIDTH ** -0.5),
        'moe_w_group': nrm(28, (L, D_MODEL, N_GROUPS), D_MODEL ** -0.5),
        'moe_b_group': nrm(29, (L, N_GROUPS), 0.01),
        'moe_w_expert': nrm(30, (L, D_MODEL, N_EXPERTS), D_MODEL ** -0.5),
        'moe_b_expert': nrm(31, (L, N_EXPERTS), 0.01),
        'moe_w_gate': nrm(32, (L, N_EXPERTS, D_MODEL, D_EXPERT), D_MODEL ** -0.5),
        'moe_w_up': nrm(33, (L, N_EXPERTS, D_MODEL, D_EXPERT), D_MODEL ** -0.5),
        'moe_w_down': nrm(34, (L, N_EXPERTS, D_EXPERT, D_MODEL), D_EXPERT ** -0.5),
    }


def reference(x_prompt, x_sample, c_prompt, c_sample, w_ada, b_ada, norm1_g, norm2_g, w_in,
              swa_q_gain, swa_k_gain, swa_sink,
              mla_q_norm_g, mla_w_q_up, mla_kv_norm_g, mla_w_kv_up, mla_q_gain, mla_k_gain,
              conv_dw_w, conv_dw_b, conv_gn_g, conv_gn_b, conv_w_pw2, conv_b_pw2,
              out_norm_a, out_norm_b, out_norm_c, w_out,
              moe_w_group, moe_b_group, moe_w_expert, moe_b_expert, moe_w_gate, moe_w_up, moe_w_down):
    y_prompt = x_prompt
    y_sample = x_sample
    for l in range(DEPTH):
        lp = (w_ada[l], b_ada[l], norm1_g[l], norm2_g[l], w_in[l],
              swa_q_gain[l], swa_k_gain[l], swa_sink[l],
              mla_q_norm_g[l], mla_w_q_up[l], mla_kv_norm_g[l], mla_w_kv_up[l], mla_q_gain[l], mla_k_gain[l],
              conv_dw_w[l], conv_dw_b[l], conv_gn_g[l], conv_gn_b[l], conv_w_pw2[l], conv_b_pw2[l],
              out_norm_a[l], out_norm_b[l], out_norm_c[l], w_out[l],
              moe_w_group[l], moe_b_group[l], moe_w_expert[l], moe_b_expert[l],
              moe_w_gate[l], moe_w_up[l], moe_w_down[l])
        y_prompt = encoder_layer(y_prompt, c_prompt, *lp)
        y_sample = encoder_layer(y_sample, c_sample, *lp)
    return (y_prompt, y_sample)
```

```python
import functools

import numpy as np
import jax
import jax.numpy as jnp
from jax import lax
from jax.experimental import pallas as pl
from jax.experimental.pallas import tpu as pltpu

F32 = jnp.float32
BF16 = jnp.bfloat16

EPS = 1e-6
NEG_INF = -1e30

D_MODEL = 1024
HEAD_DIM = 64
SWA_HEADS = 6
SWA_KV_HEADS = 2
SWA_GROUP = SWA_HEADS // SWA_KV_HEADS
WINDOW = 128
BLOCK = WINDOW
MLA_HEADS = 6
MLA_Q_RANK = 256
MLA_KV_RANK = 128
MLA_NOPE = 64
MLA_ROPE = 32
MLA_QK = MLA_NOPE + MLA_ROPE
MLA_V = 64
ROPE_BASE = 10000.0
CONV_CH = 256
CONV_GROUPS = 4
CONV_WIDTH = 31
CONV_PAD = (CONV_WIDTH - 1) // 2
A_Q = SWA_HEADS * HEAD_DIM
A_KV = SWA_KV_HEADS * HEAD_DIM
A_IN = A_Q + 2 * A_KV
B_IN = MLA_Q_RANK + MLA_KV_RANK + MLA_ROPE
N_GROUPS = 4
EXPERTS_PER_GROUP = 8
N_EXPERTS = N_GROUPS * EXPERTS_PER_GROUP
D_EXPERT = 256
N_MOD = 6

LANES = 128
HALO = 16
VMEM_LIMIT = 48 * 1024 * 1024


class _Geom:
    def __init__(self, bp, sp, bs, ss):
        self.bp, self.sp, self.bs, self.ss = bp, sp, bs, ss
        self.tp = bp * sp
        self.t = bp * sp + bs * ss
        self.nb = bp + bs

    def tile(self, target):
        t = target
        while self.sp % t or self.ss % t:
            t //= 2
        return t

    def batch(self, i, tm):
        npt = self.tp // tm
        return jnp.where(i < npt, i // (self.sp // tm), self.bp + (i - npt) // (self.ss // tm))

    def pos(self, i, tm):
        npt = self.tp // tm
        return jnp.where(i < npt, i % (self.sp // tm), (i - npt) % (self.ss // tm))

    def is_last(self, i, tm):
        npt = self.tp // tm
        return jnp.where(i < npt, i % (self.sp // tm) == self.sp // tm - 1,
                         (i - npt) % (self.ss // tm) == self.ss // tm - 1)


def _cparams(sem):
    return pltpu.CompilerParams(dimension_semantics=sem, vmem_limit_bytes=VMEM_LIMIT)


def _silu(x):
    return x * jax.nn.sigmoid(x)


def _dot(a, b):
    return jnp.dot(a, b, preferred_element_type=F32)


def _dot_nt(a, b):
    return lax.dot_general(a, b, (((1,), (1,)), ((), ())), preferred_element_type=F32)


def _split_dot(x, w):
    hi = x.astype(BF16)
    lo = (x - hi.astype(F32)).astype(BF16)
    return _dot(hi, w) + _dot(lo, w)


def _mod_kernel(c_ref, w_ref, b_ref, o_ref):
    c = c_ref[...]
    o_ref[...] = _dot(_silu(c).astype(BF16), w_ref[...].astype(BF16)) + b_ref[...]


def _modulation(c_pad, w_ada, b_ada):
    L, d, n = w_ada.shape
    tn = 768
    rows = c_pad.shape[0]
    return pl.pallas_call(
        _mod_kernel,
        grid=(L, n // tn),
        in_specs=[pl.BlockSpec((rows, d), lambda l, j: (0, 0)),
                  pl.BlockSpec((None, d, tn), lambda l, j: (l, 0, j)),
                  pl.BlockSpec((None, 1, tn), lambda l, j: (l, 0, j))],
        out_specs=pl.BlockSpec((None, rows, tn), lambda l, j: (l, 0, j)),
        out_shape=jax.ShapeDtypeStruct((L, rows, n), F32),
        compiler_params=_cparams(("arbitrary", "arbitrary")),
        name="adaln_mod",
    )(c_pad, w_ada, b_ada.reshape(L, 1, n))


_IN_SEGS = (("q", A_Q), ("k", A_KV), ("v", A_KV), ("cq", MLA_Q_RANK), ("ckv", MLA_KV_RANK), ("kr", LANES),
            ("uc", 2 * CONV_CH))


def _inproj_kernel(x_ref, sh_ref, sc_ref, g_ref, w_ref, *out_refs):
    x = x_ref[...]
    ms = jnp.mean(x * x, axis=-1, keepdims=True)
    h = x * lax.rsqrt(ms + EPS) * g_ref[...]
    h = (h * (1.0 + sc_ref[...]) + sh_ref[...]).astype(BF16)
    off = 0
    for (_, width), o_ref in zip(_IN_SEGS, out_refs):
        o_ref[...] = _dot(h, w_ref[:, off:off + width]).astype(o_ref.dtype)
        off += width


def _mod_spec(geom, tm, k):
    return pl.BlockSpec((None, 1, D_MODEL), lambda i: (geom.batch(i, tm) * N_MOD + k, 0, 0))


def _inproj(geom, x, mods, g1, w_in_p):
    tm = geom.tile(512)
    nw = w_in_p.shape[1]
    return pl.pallas_call(
        _inproj_kernel,
        grid=(geom.t // tm,),
        in_specs=[pl.BlockSpec((tm, D_MODEL), lambda i: (i, 0)),
                  _mod_spec(geom, tm, 0), _mod_spec(geom, tm, 1),
                  pl.BlockSpec((1, D_MODEL), lambda i: (0, 0)),
                  pl.BlockSpec((D_MODEL, nw), lambda i: (0, 0))],
        out_specs=[pl.BlockSpec((tm, w), lambda i: (i, 0)) for _, w in _IN_SEGS],
        out_shape=[jax.ShapeDtypeStruct((geom.t, w), BF16) for _, w in _IN_SEGS],
        compiler_params=_cparams(("parallel",)),
        name="norm1_inproj",
    )(x, mods, mods, g1, w_in_p)


def _swa_kernel(geom, q_ref, kp_ref, kc_ref, kn_ref, vp_ref, vc_ref, vn_ref, gq_ref, gk_ref,
                bdq_ref, bdk_ref, sink_ref, o_ref):
    i = pl.program_id(0)
    first = geom.pos(i, BLOCK) == 0
    last = geom.is_last(i, BLOCK)

    q = q_ref[...].astype(F32)
    msq = _dot((q * q).astype(BF16), bdq_ref[...]) * (1.0 / HEAD_DIM)
    qn = (q * lax.rsqrt(msq + EPS) * gq_ref[...] * (HEAD_DIM ** -0.5)).astype(BF16)

    k3 = jnp.concatenate([kp_ref[...], kc_ref[...], kn_ref[...]], axis=0).astype(F32)
    msk = _dot((k3 * k3).astype(BF16), bdk_ref[...]) * (1.0 / HEAD_DIM)
    kn = (k3 * lax.rsqrt(msk + EPS) * gk_ref[...]).astype(BF16)
    v3 = jnp.concatenate([vp_ref[...], vc_ref[...], vn_ref[...]], axis=0)

    qi = lax.broadcasted_iota(jnp.int32, (BLOCK, 3 * BLOCK), 0)
    ki = lax.broadcasted_iota(jnp.int32, (BLOCK, 3 * BLOCK), 1)
    rel = jnp.abs(ki - BLOCK - qi)
    k_lo = jnp.where(first, BLOCK, 0)
    k_hi = jnp.where(last, 2 * BLOCK, 3 * BLOCK)
    valid = jnp.logical_and(rel <= WINDOW, jnp.logical_and(ki >= k_lo, ki < k_hi))
    relf = rel.astype(F32)

    outs = []
    for h in range(SWA_HEADS):
        g = h // SWA_GROUP
        slope = 2.0 ** (-8.0 * (h + 1) / SWA_HEADS)
        qh = qn[:, h * HEAD_DIM:(h + 1) * HEAD_DIM]
        kg = kn[:, g * HEAD_DIM:(g + 1) * HEAD_DIM]
        vg = v3[:, g * HEAD_DIM:(g + 1) * HEAD_DIM]
        s = _dot_nt(qh, kg) - slope * relf
        s = jnp.where(valid, s, NEG_INF)
        sink = sink_ref[:, h:h + 1]
        m = jnp.maximum(jnp.max(s, axis=-1, keepdims=True), sink)
        p = jnp.exp(s - m)
        denom = jnp.sum(p, axis=-1, keepdims=True) + jnp.exp(sink - m)
        outs.append(_dot(p.astype(BF16), vg) / denom)
    o_ref[...] = jnp.concatenate(outs, axis=-1).astype(o_ref.dtype)


def _swa(geom, q, k, v, gq, gk, bdq, bdk, sink):
    nblk = geom.t // BLOCK
    prev = lambda i: (jnp.maximum(i - 1, 0), 0)
    cur = lambda i: (i, 0)
    nxt = lambda i: (jnp.minimum(i + 1, nblk - 1), 0)
    kv_spec = lambda f: pl.BlockSpec((BLOCK, A_KV), f)
    const = lambda shape: pl.BlockSpec(shape, lambda i: (0, 0))
    return pl.pallas_call(
        functools.partial(_swa_kernel, geom),
        grid=(nblk,),
        in_specs=[pl.BlockSpec((BLOCK, A_Q), cur),
                  kv_spec(prev), kv_spec(cur), kv_spec(nxt),
                  kv_spec(prev), kv_spec(cur), kv_spec(nxt),
                  const((1, A_Q)), const((1, A_KV)), const((A_Q, A_Q)), const((A_KV, A_KV)),
                  const((1, LANES))],
        out_specs=pl.BlockSpec((BLOCK, A_Q), cur),
        out_shape=jax.ShapeDtypeStruct((geom.t, A_Q), BF16),
        compiler_params=_cparams(("parallel",)),
        name="swa_attention",
    )(q, k, k, k, v, v, v, gq, gk, bdq, bdk, sink)


def _rope(x, c_ref, sa_ref, sb_ref):
    return x * c_ref[...] + pltpu.roll(x, LANES - MLA_ROPE // 2, 1) * sa_ref[...] + \
        pltpu.roll(x, MLA_ROPE // 2, 1) * sb_ref[...]


def _mla_prep_kernel(cq_ref, ckv_ref, kr_ref, gqn_ref, gkvn_ref, wq_ref, wkv_ref, gq_ref, gk_ref,
                     c_ref, sa_ref, sb_ref, q_out, k_out, v_out):
    cq = cq_ref[...].astype(F32)
    qn = (cq * lax.rsqrt(jnp.mean(cq * cq, axis=-1, keepdims=True) + EPS) * gqn_ref[...]).astype(BF16)
    ckv = ckv_ref[...].astype(F32)
    kvn = (ckv * lax.rsqrt(jnp.mean(ckv * ckv, axis=-1, keepdims=True) + EPS) * gkvn_ref[...]).astype(BF16)
    kr = kr_ref[...].astype(F32)
    lane = lax.broadcasted_iota(jnp.int32, (1, LANES), 1)
    for h in range(MLA_HEADS):
        sl = slice(h * LANES, (h + 1) * LANES)
        qh = _dot(qn, wq_ref[:, sl])
        qh = qh * lax.rsqrt(jnp.sum(qh * qh, axis=-1, keepdims=True) * (1.0 / MLA_QK) + EPS) * gq_ref[...]
        q_out[h] = (_rope(qh, c_ref, sa_ref, sb_ref) * (MLA_QK ** -0.5)).astype(BF16)
        kvh = _dot(kvn, wkv_ref[:, sl])
        v_out[h] = kvh.astype(BF16)
        kh = jnp.where(lane < MLA_NOPE, kvh, 0.0) + kr
        kh = kh * lax.rsqrt(jnp.sum(kh * kh, axis=-1, keepdims=True) * (1.0 / MLA_QK) + EPS) * gk_ref[...]
        k_out[h] = _rope(kh, c_ref, sa_ref, sb_ref).astype(BF16)


def _mla_prep(geom, cq, ckv, kr, gqn, gkvn, wq_p, wkv, gq_p, gk_p, rope_c, rope_sa, rope_sb):
    tm = geom.tile(256)
    tok = lambda w: pl.BlockSpec((tm, w), lambda i: (i, 0))
    const = lambda shape: pl.BlockSpec(shape, lambda i: (0, 0))
    rope = pl.BlockSpec((tm, LANES), lambda i: (geom.pos(i, tm), 0))
    hm = pl.BlockSpec((MLA_HEADS, tm, LANES), lambda i: (0, i, 0))
    return pl.pallas_call(
        _mla_prep_kernel,
        grid=(geom.t // tm,),
        in_specs=[tok(MLA_Q_RANK), tok(MLA_KV_RANK), tok(LANES),
                  const((1, MLA_Q_RANK)), const((1, MLA_KV_RANK)),
                  const((MLA_Q_RANK, MLA_HEADS * LANES)), const((MLA_KV_RANK, MLA_HEADS * LANES)),
                  const((1, LANES)), const((1, LANES)), rope, rope, rope],
        out_specs=[hm, hm, hm],
        out_shape=[jax.ShapeDtypeStruct((MLA_HEADS, geom.t, LANES), BF16)] * 3,
        compiler_params=_cparams(("parallel",)),
        name="mla_prep",
    )(cq, ckv, kr, gqn, gkvn, wq_p, wkv, gq_p, gk_p, rope_c, rope_sa, rope_sb)


def _mla_attn_kernel(tk, nk, q_ref, k_ref, v_ref, o_ref):
    tq = q_ref.shape[1]
    outs = []
    for hh in range(2):
        q = q_ref[hh]

        def body(j, carry, hh=hh, q=q):
            m, l, acc = carry
            off = pl.multiple_of(j * tk, tk)
            kc = k_ref[hh, pl.ds(off, tk), :]
            vc = v_ref[hh, pl.ds(off, tk), :]
            s = _dot_nt(q, kc)
            m_new = jnp.maximum(m, jnp.max(s, axis=-1, keepdims=True))
            alpha = jnp.exp(m - m_new)
            p = jnp.exp(s - m_new)
            l = alpha * l + jnp.sum(p, axis=-1, keepdims=True)
            acc = alpha * acc + _dot(p.astype(BF16), vc)
            return m_new, l, acc

        m0 = jnp.full((tq, 1), NEG_INF, F32)
        l0 = jnp.zeros((tq, 1), F32)
        a0 = jnp.zeros((tq, LANES), F32)
        _, l, acc = lax.fori_loop(0, nk, body, (m0, l0, a0))
        outs.append(acc / l)
    lane = lax.broadcasted_iota(jnp.int32, (1, LANES), 1)
    o_ref[...] = jnp.where(lane < MLA_V, pltpu.roll(outs[0], MLA_V, 1), outs[1]).astype(o_ref.dtype)


def _mla_attn_group(geom, q, k, v, out_prev, nseq, s, tok0):
    tq = min(256, s)
    tk = min(512, s)
    nq = s // tq
    qb0 = tok0 // tq
    sb0 = tok0 // s
    args = [q, k, v]
    in_specs = [pl.BlockSpec((2, tq, LANES), lambda b, hp, i: (hp, qb0 + b * nq + i, 0)),
                pl.BlockSpec((2, s, LANES), lambda b, hp, i: (hp, sb0 + b, 0)),
                pl.BlockSpec((2, s, LANES), lambda b, hp, i: (hp, sb0 + b, 0))]
    aliases = {}
    kern = functools.partial(_mla_attn_kernel, tk, s // tk)
    if out_prev is not None:
        args.append(out_prev)
        in_specs.append(pl.BlockSpec(memory_space=pl.ANY))
        aliases = {3: 0}
        kern = lambda q_ref, k_ref, v_ref, prev_ref, o_ref, _k=kern: _k(q_ref, k_ref, v_ref, o_ref)
    return pl.pallas_call(
        kern,
        grid=(nseq, MLA_HEADS // 2, nq),
        in_specs=in_specs,
        out_specs=pl.BlockSpec((tq, LANES), lambda b, hp, i: (qb0 + b * nq + i, hp)),
        out_shape=jax.ShapeDtypeStruct((geom.t, MLA_HEADS * MLA_V), BF16),
        input_output_aliases=aliases,
        compiler_params=_cparams(("parallel", "parallel", "arbitrary")),
        name="mla_attention",
    )(*args)


def _mla_attn(geom, q, k, v):
    assert geom.tp % geom.ss == 0
    out = _mla_attn_group(geom, q, k, v, None, geom.bp, geom.sp, 0)
    return _mla_attn_group(geom, q, k, v, out, geom.bs, geom.ss, geom.tp)


def _conv_kernel(geom, tc, up_ref, uc_ref, un_ref, w_ref, b_ref, gng_ref, gnb_ref, bd_ref, wpw_ref, bpw_ref,
                 o_ref, hs_ref):
    i = pl.program_id(0)
    first = geom.pos(i, tc) == 0
    last = geom.is_last(i, tc)

    def glu(u):
        u = u.astype(F32)
        return u[:, :CONV_CH] * jax.nn.sigmoid(u[:, CONV_CH:])

    hs_ref[0:HALO, :] = jnp.where(first, 0.0, glu(up_ref[...]))
    hs_ref[HALO:HALO + tc, :] = glu(uc_ref[...])
    hs_ref[HALO + tc:HALO + tc + HALO, :] = jnp.where(last, 0.0, glu(un_ref[...]))

    rows = 64 if tc % 64 == 0 else tc
    for r0 in range(0, tc, rows):
        acc = jnp.zeros((rows, CONV_CH), F32) + b_ref[...]
        for j in range(CONV_WIDTH):
            start = HALO + r0 + j - CONV_PAD
            acc = acc + hs_ref[start:start + rows, :] * w_ref[j:j + 1, :]
        mu = _split_dot(acc, bd_ref[...]) * (CONV_GROUPS / CONV_CH)
        d = acc - mu
        var = _split_dot(d * d, bd_ref[...]) * (CONV_GROUPS / CONV_CH)
        hn = d * lax.rsqrt(var + EPS) * gng_ref[...] + gnb_ref[...]
        y = _dot(_silu(hn).astype(BF16), wpw_ref[...]) + bpw_ref[...]
        o_ref[r0:r0 + rows, :] = y.astype(o_ref.dtype)


def _conv(geom, uc, dw_w, dw_b, gn_g, gn_b, bd, w_pw2, b_pw2):
    tc = geom.tile(256)
    hb = tc // HALO
    nh = geom.t // HALO
    const = lambda shape: pl.BlockSpec(shape, lambda i: (0, 0))
    return pl.pallas_call(
        functools.partial(_conv_kernel, geom, tc),
        grid=(geom.t // tc,),
        in_specs=[pl.BlockSpec((HALO, 2 * CONV_CH), lambda i: (jnp.maximum(i * hb - 1, 0), 0)),
                  pl.BlockSpec((tc, 2 * CONV_CH), lambda i: (i, 0)),
                  pl.BlockSpec((HALO, 2 * CONV_CH), lambda i: (jnp.minimum((i + 1) * hb, nh - 1), 0)),
                  const((32, CONV_CH)), const((1, CONV_CH)), const((1, CONV_CH)), const((1, CONV_CH)),
                  const((CONV_CH, CONV_CH)), const((CONV_CH, CONV_CH)), const((1, CONV_CH))],
        out_specs=pl.BlockSpec((tc, CONV_CH), lambda i: (i, 0)),
        out_shape=jax.ShapeDtypeStruct((geom.t, CONV_CH), BF16),
        scratch_shapes=[pltpu.VMEM((tc + 2 * HALO, CONV_CH), F32)],
        compiler_params=_cparams(("parallel",)),
        name="conformer_conv",
    )(uc, uc, uc, dw_w, dw_b, gn_g, gn_b, bd, w_pw2, b_pw2)


def _rms_rows(x, g):
    return x * lax.rsqrt(jnp.mean(x * x, axis=-1, keepdims=True) + EPS) * g


def _outproj_kernel(x_ref, oa_ref, ob_ref, oc_ref, ga_ref, gb_ref, gc_ref, w_ref, gate_ref, sh_ref, sc_ref,
                    g2_ref, wr_hi_ref, wr_lo_ref, br_ref, x1_ref, h2_ref, lg_ref):
    na = _rms_rows(oa_ref[...].astype(F32), ga_ref[...]).astype(BF16)
    nb = _rms_rows(ob_ref[...].astype(F32), gb_ref[...]).astype(BF16)
    nc = _rms_rows(oc_ref[...].astype(F32), gc_ref[...]).astype(BF16)
    wa = A_Q
    wb = wa + MLA_HEADS * MLA_V
    y = _dot(na, w_ref[0:wa, :]) + _dot(nb, w_ref[wa:wb, :]) + _dot(nc, w_ref[wb:, :])
    x1 = x_ref[...] + gate_ref[...] * y
    x1_ref[...] = x1
    h2 = _rms_rows(x1, g2_ref[...]) * (1.0 + sc_ref[...]) + sh_ref[...]
    hi = h2.astype(BF16)
    lo = (h2 - hi.astype(F32)).astype(BF16)
    h2_ref[...] = hi
    lg_ref[...] = _dot(hi, wr_hi_ref[...]) + _dot(lo, wr_hi_ref[...]) + _dot(hi, wr_lo_ref[...]) + br_ref[...]


def _outproj(geom, x, oa, ob, oc, ga, gb, gc, w_out, mods, g2, wr_hi, wr_lo, br):
    tm = geom.tile(256)
    tok = lambda w: pl.BlockSpec((tm, w), lambda i: (i, 0))
    const = lambda shape: pl.BlockSpec(shape, lambda i: (0, 0))
    wb = MLA_HEADS * MLA_V
    return pl.pallas_call(
        _outproj_kernel,
        grid=(geom.t // tm,),
        in_specs=[tok(D_MODEL), tok(A_Q), tok(wb), tok(CONV_CH),
                  const((1, A_Q)), const((1, wb)), const((1, CONV_CH)),
                  const((D_MODEL, D_MODEL)),
                  _mod_spec(geom, tm, 2), _mod_spec(geom, tm, 3), _mod_spec(geom, tm, 4),
                  const((1, D_MODEL)), const((D_MODEL, LANES)), const((D_MODEL, LANES)), const((1, LANES))],
        out_specs=[tok(D_MODEL), tok(D_MODEL), tok(LANES)],
        out_shape=[jax.ShapeDtypeStruct((geom.t, D_MODEL), F32),
                   jax.ShapeDtypeStruct((geom.t, D_MODEL), BF16),
                   jax.ShapeDtypeStruct((geom.t, LANES), F32)],
        compiler_params=_cparams(("parallel",)),
        name="merge_outproj_norm2",
    )(x, oa, ob, oc, ga, gb, gc, w_out, mods, mods, mods, g2, wr_hi, wr_lo, br)


def _route_kernel(lg_ref, cmb_ref):
    lg = lg_ref[...]
    lane = lax.broadcasted_iota(jnp.int32, lg.shape, 1)
    lanef = lane.astype(F32)
    big = float(2 * LANES)
    is_g = jnp.logical_and(lane >= N_EXPERTS, lane < N_EXPERTS + N_GROUPS)
    gl = jnp.where(is_g, lg, NEG_INF)
    gmax = jnp.max(gl, axis=-1, keepdims=True)
    gsum = jnp.sum(jnp.exp(gl - gmax), axis=-1, keepdims=True)
    g_val = 1.0 / gsum
    g_idx = jnp.min(jnp.where(jnp.logical_and(is_g, gl == gmax), lanef, big), axis=-1, keepdims=True) - N_EXPERTS
    lo = g_idx * EXPERTS_PER_GROUP
    is_e = jnp.logical_and(lanef >= lo, lanef < lo + EXPERTS_PER_GROUP)
    el = jnp.where(is_e, lg, NEG_INF)
    emax = jnp.max(el, axis=-1, keepdims=True)
    ee = jnp.exp(el - emax)
    e_prob = ee / jnp.sum(ee, axis=-1, keepdims=True)
    p1 = jnp.where(is_e, e_prob, -1.0)
    v1 = jnp.max(p1, axis=-1, keepdims=True)
    i1 = jnp.min(jnp.where(p1 == v1, lanef, big), axis=-1, keepdims=True)
    p2 = jnp.where(lanef == i1, -1.0, p1)
    v2 = jnp.max(p2, axis=-1, keepdims=True)
    i2 = jnp.min(jnp.where(p2 == v2, lanef, big), axis=-1, keepdims=True)
    scale = g_val / (v1 + v2)
    cmb_ref[...] = jnp.where(lanef == i1, v1 * scale, 0.0) + jnp.where(lanef == i2, v2 * scale, 0.0)


def _route(geom, logits):
    tm = geom.tile(512)
    spec = pl.BlockSpec((tm, LANES), lambda i: (i, 0))
    return pl.pallas_call(
        _route_kernel,
        grid=(geom.t // tm,),
        in_specs=[spec],
        out_specs=spec,
        out_shape=jax.ShapeDtypeStruct((geom.t, LANES), F32),
        compiler_params=_cparams(("parallel",)),
        name="moe_route",
    )(logits)


def _moe_kernel(h_ref, cmb_ref, wg_ref, wu_ref, wd_ref, x1_ref, gate_ref, o_ref, acc_ref):
    e = pl.program_id(1)

    @pl.when(e == 0)
    def _():
        acc_ref[...] = jnp.zeros_like(acc_ref)

    h = h_ref[...]
    lane = lax.broadcasted_iota(jnp.int32, cmb_ref.shape, 1)
    cw = jnp.sum(jnp.where(lane == e, cmb_ref[...], 0.0), axis=-1, keepdims=True)
    act = _silu(_dot(h, wg_ref[...])) * _dot(h, wu_ref[...]) * cw
    acc_ref[...] += _dot(act.astype(BF16), wd_ref[...])

    @pl.when(e == pl.num_programs(1) - 1)
    def _():
        o_ref[...] = x1_ref[...] + gate_ref[...] * acc_ref[...]


def _moe(geom, h2, cmb, wg, wu, wd, x1, mods):
    tm = geom.tile(512)
    tok = lambda w: pl.BlockSpec((tm, w), lambda i, e: (i, 0))
    return pl.pallas_call(
        _moe_kernel,
        grid=(geom.t // tm, N_EXPERTS),
        in_specs=[tok(D_MODEL), tok(LANES),
                  pl.BlockSpec((None, D_MODEL, D_EXPERT), lambda i, e: (e, 0, 0)),
                  pl.BlockSpec((None, D_MODEL, D_EXPERT), lambda i, e: (e, 0, 0)),
                  pl.BlockSpec((None, D_EXPERT, D_MODEL), lambda i, e: (e, 0, 0)),
                  tok(D_MODEL),
                  pl.BlockSpec((None, 1, D_MODEL), lambda i, e: (geom.batch(i, tm) * N_MOD + 5, 0, 0))],
        out_specs=tok(D_MODEL),
        out_shape=jax.ShapeDtypeStruct((geom.t, D_MODEL), F32),
        scratch_shapes=[pltpu.VMEM((tm, D_MODEL), F32)],
        compiler_params=_cparams(("parallel", "arbitrary")),
        name="moe_experts",
    )(h2, cmb, wg, wu, wd, x1, mods)


def _block_diag_ones(n, blk):
    idx = np.arange(n) // blk
    return jnp.asarray((idx[:, None] == idx[None, :]).astype(np.float32), dtype=BF16)


def _rope_tables(smax):
    half = MLA_ROPE // 2
    inv_freq = 1.0 / (ROPE_BASE ** (jnp.arange(0, MLA_ROPE, 2, dtype=F32) / MLA_ROPE))
    ang = jnp.arange(smax, dtype=F32)[:, None] * inv_freq[None, :]
    cos, sin = jnp.cos(ang), jnp.sin(ang)
    ones = jnp.ones((smax, MLA_NOPE), F32)
    zeros = lambda w: jnp.zeros((smax, w), F32)
    pad = LANES - MLA_QK
    c = jnp.concatenate([ones, cos, cos, zeros(pad)], axis=1)
    sa = jnp.concatenate([zeros(MLA_NOPE), -sin, zeros(half), zeros(pad)], axis=1)
    sb = jnp.concatenate([zeros(MLA_NOPE), zeros(half), sin, zeros(pad)], axis=1)
    return c, sa, sb


def _pad_cols(w, n):
    return jnp.pad(w, ((0, 0), (0, n - w.shape[1])))


def _layer_weights(l, w_in, mla_w_q_up, mla_q_gain, mla_k_gain, swa_q_gain, swa_k_gain, swa_sink,
                   moe_w_group, moe_b_group, moe_w_expert, moe_b_expert):
    wi = w_in[l]
    kr = jnp.pad(wi[:, A_IN + MLA_Q_RANK + MLA_KV_RANK:A_IN + B_IN], ((0, 0), (MLA_NOPE, LANES - MLA_QK)))
    w_in_p = jnp.concatenate([wi[:, :A_IN], wi[:, A_IN:A_IN + MLA_Q_RANK + MLA_KV_RANK], kr, wi[:, A_IN + B_IN:]],
                             axis=1).astype(BF16)
    wq = mla_w_q_up[l].reshape(MLA_Q_RANK, MLA_HEADS, MLA_QK)
    wq_p = jnp.pad(wq, ((0, 0), (0, 0), (0, LANES - MLA_QK))).reshape(MLA_Q_RANK, MLA_HEADS * LANES).astype(BF16)
    gq_p = _pad_cols(mla_q_gain[l][None, :], LANES)
    gk_p = _pad_cols(mla_k_gain[l][None, :], LANES)
    gq_a = jnp.tile(swa_q_gain[l], SWA_HEADS)[None, :]
    gk_a = jnp.tile(swa_k_gain[l], SWA_KV_HEADS)[None, :]
    sink = _pad_cols(swa_sink[l][None, :], LANES)
    wr = _pad_cols(jnp.concatenate([moe_w_expert[l], moe_w_group[l]], axis=1), LANES)
    wr_hi = wr.astype(BF16)
    wr_lo = (wr - wr_hi.astype(F32)).astype(BF16)
    br = _pad_cols(jnp.concatenate([moe_b_expert[l], moe_b_group[l]])[None, :], LANES)
    return w_in_p, wq_p, gq_p, gk_p, gq_a, gk_a, sink, wr_hi, wr_lo, br


def kernel(x_prompt, x_sample, c_prompt, c_sample, w_ada, b_ada, norm1_g, norm2_g, w_in, swa_q_gain, swa_k_gain, swa_sink, mla_q_norm_g, mla_w_q_up, mla_kv_norm_g, mla_w_kv_up, mla_q_gain, mla_k_gain, conv_dw_w, conv_dw_b, conv_gn_g, conv_gn_b, conv_w_pw2, conv_b_pw2, out_norm_a, out_norm_b, out_norm_c, w_out, moe_w_group, moe_b_group, moe_w_expert, moe_b_expert, moe_w_gate, moe_w_up, moe_w_down):
    bp, sp, d = x_prompt.shape
    bs, ss, _ = x_sample.shape
    assert d == D_MODEL
    geom = _Geom(bp, sp, bs, ss)
    depth = w_ada.shape[0]

    x = jnp.concatenate([x_prompt.reshape(bp * sp, d), x_sample.reshape(bs * ss, d)], axis=0)
    c = jnp.concatenate([c_prompt, c_sample], axis=0)
    rows = -(-geom.nb // 8) * 8
    c_pad = jnp.pad(c, ((0, rows - geom.nb), (0, 0)))
    mods_all = _modulation(c_pad, w_ada, b_ada)

    rope_c, rope_sa, rope_sb = _rope_tables(max(sp, ss))
    bdq = _block_diag_ones(A_Q, HEAD_DIM)
    bdk = _block_diag_ones(A_KV, HEAD_DIM)
    bdc = _block_diag_ones(CONV_CH, CONV_CH // CONV_GROUPS)
    row = lambda v: v[None, :]

    for l in range(depth):
        (w_in_p, wq_p, gq_p, gk_p, gq_a, gk_a, sink, wr_hi, wr_lo, br) = _layer_weights(
            l, w_in, mla_w_q_up, mla_q_gain, mla_k_gain, swa_q_gain, swa_k_gain, swa_sink,
            moe_w_group, moe_b_group, moe_w_expert, moe_b_expert)
        mods = mods_all[l, :geom.nb].reshape(geom.nb * N_MOD, 1, D_MODEL)

        q_a, k_a, v_a, cq, ckv, kr, uc = _inproj(geom, x, mods, row(norm1_g[l]), w_in_p)
        out_a = _swa(geom, q_a, k_a, v_a, gq_a, gk_a, bdq, bdk, sink)
        q_b, k_b, v_b = _mla_prep(geom, cq, ckv, kr, row(mla_q_norm_g[l]), row(mla_kv_norm_g[l]), wq_p,
                                  mla_w_kv_up[l].astype(BF16), gq_p, gk_p, rope_c, rope_sa, rope_sb)
        out_b = _mla_attn(geom, q_b, k_b, v_b)
        dw_w = jnp.pad(conv_dw_w[l].reshape(CONV_WIDTH, CONV_CH), ((0, 32 - CONV_WIDTH), (0, 0)))
        out_c = _conv(geom, uc, dw_w, row(conv_dw_b[l]), row(conv_gn_g[l]), row(conv_gn_b[l]), bdc,
                      conv_w_pw2[l].astype(BF16), row(conv_b_pw2[l]))
        x1, h2, logits = _outproj(geom, x, out_a, out_b, out_c, row(out_norm_a[l]), row(out_norm_b[l]),
                                  row(out_norm_c[l]), w_out[l].astype(BF16), mods, row(norm2_g[l]),
                                  wr_hi, wr_lo, br)
        cmb = _route(geom, logits)
        x = _moe(geom, h2, cmb, moe_w_gate[l].astype(BF16), moe_w_up[l].astype(BF16),
                 moe_w_down[l].astype(BF16), x1, mods)

    y_prompt = x[:geom.tp].reshape(bp, sp, d)
    y_sample = x[geom.tp:].reshape(bs, ss, d)
    return (y_prompt, y_sample)
```

```python
import functools

import numpy as np
import jax
import jax.numpy as jnp
from jax import lax
from jax.experimental import pallas as pl
from jax.experimental.pallas import tpu as pltpu

F32 = jnp.float32
BF16 = jnp.bfloat16

EPS = 1e-6
NEG_INF = -1e30
LOG2E = 1.4426950408889634

D_MODEL = 1024
HEAD_DIM = 64
SWA_HEADS = 6
SWA_KV_HEADS = 2
SWA_GROUP = SWA_HEADS // SWA_KV_HEADS
WINDOW = 128
BLOCK = WINDOW
MLA_HEADS = 6
MLA_Q_RANK = 256
MLA_KV_RANK = 128
MLA_NOPE = 64
MLA_ROPE = 32
MLA_QK = MLA_NOPE + MLA_ROPE
MLA_V = 64
ROPE_BASE = 10000.0
CONV_CH = 256
CONV_GROUPS = 4
CONV_WIDTH = 31
CONV_PAD = (CONV_WIDTH - 1) // 2
A_Q = SWA_HEADS * HEAD_DIM
A_KV = SWA_KV_HEADS * HEAD_DIM
A_IN = A_Q + 2 * A_KV
B_IN = MLA_Q_RANK + MLA_KV_RANK + MLA_ROPE
N_GROUPS = 4
EXPERTS_PER_GROUP = 8
N_EXPERTS = N_GROUPS * EXPERTS_PER_GROUP
D_EXPERT = 256
N_MOD = 6

LANES = 128
HALO = 16
VMEM_LIMIT = 48 * 1024 * 1024


class _Geom:
    def __init__(self, bp, sp, bs, ss):
        self.bp, self.sp, self.bs, self.ss = bp, sp, bs, ss
        self.tp = bp * sp
        self.t = bp * sp + bs * ss
        self.nb = bp + bs

    def tile(self, target):
        t = target
        while self.sp % t or self.ss % t:
            t //= 2
        return t

    def batch(self, i, tm):
        npt = self.tp // tm
        return jnp.where(i < npt, i // (self.sp // tm), self.bp + (i - npt) // (self.ss // tm))

    def pos(self, i, tm):
        npt = self.tp // tm
        return jnp.where(i < npt, i % (self.sp // tm), (i - npt) % (self.ss // tm))

    def is_last(self, i, tm):
        npt = self.tp // tm
        return jnp.where(i < npt, i % (self.sp // tm) == self.sp // tm - 1,
                         (i - npt) % (self.ss // tm) == self.ss // tm - 1)


def _cparams(sem):
    return pltpu.CompilerParams(dimension_semantics=sem, vmem_limit_bytes=VMEM_LIMIT)


def _silu(x):
    return x * jax.nn.sigmoid(x)


def _dot(a, b):
    return jnp.dot(a, b, preferred_element_type=F32)


def _dot_nt(a, b):
    return lax.dot_general(a, b, (((1,), (1,)), ((), ())), preferred_element_type=F32)


def _split_dot(x, w):
    hi = x.astype(BF16)
    lo = (x - hi.astype(F32)).astype(BF16)
    return _dot(hi, w) + _dot(lo, w)


def _mod_kernel(c_ref, w_ref, b_ref, o_ref):
    c = c_ref[...]
    o_ref[...] = _dot(_silu(c).astype(BF16), w_ref[...].astype(BF16)) + b_ref[...]


def _modulation(c_pad, w_ada, b_ada):
    L, d, n = w_ada.shape
    tn = 768
    rows = c_pad.shape[0]
    return pl.pallas_call(
        _mod_kernel,
        grid=(L, n // tn),
        in_specs=[pl.BlockSpec((rows, d), lambda l, j: (0, 0)),
                  pl.BlockSpec((None, d, tn), lambda l, j: (l, 0, j)),
                  pl.BlockSpec((None, 1, tn), lambda l, j: (l, 0, j))],
        out_specs=pl.BlockSpec((None, rows, tn), lambda l, j: (l, 0, j)),
        out_shape=jax.ShapeDtypeStruct((L, rows, n), F32),
        compiler_params=_cparams(("arbitrary", "arbitrary")),
        name="adaln_mod",
    )(c_pad, w_ada, b_ada.reshape(L, 1, n))


_IN_SEGS = (("q", A_Q), ("k", A_KV), ("v", A_KV), ("cq", MLA_Q_RANK), ("ckv", MLA_KV_RANK), ("kr", LANES),
            ("uc", 2 * CONV_CH))


def _inproj_kernel(x_ref, sh_ref, sc_ref, g_ref, w_ref, *out_refs):
    x = x_ref[...]
    ms = jnp.mean(x * x, axis=-1, keepdims=True)
    h = x * lax.rsqrt(ms + EPS) * g_ref[...]
    h = (h * (1.0 + sc_ref[...]) + sh_ref[...]).astype(BF16)
    off = 0
    for (_, width), o_ref in zip(_IN_SEGS, out_refs):
        o_ref[...] = _dot(h, w_ref[:, off:off + width]).astype(o_ref.dtype)
        off += width


def _mod_spec(geom, tm, k):
    return pl.BlockSpec((None, 1, D_MODEL), lambda i: (geom.batch(i, tm) * N_MOD + k, 0, 0))


def _inproj(geom, x, mods, g1, w_in_p):
    tm = geom.tile(512)
    nw = w_in_p.shape[1]
    return pl.pallas_call(
        _inproj_kernel,
        grid=(geom.t // tm,),
        in_specs=[pl.BlockSpec((tm, D_MODEL), lambda i: (i, 0)),
                  _mod_spec(geom, tm, 0), _mod_spec(geom, tm, 1),
                  pl.BlockSpec((1, D_MODEL), lambda i: (0, 0)),
                  pl.BlockSpec((D_MODEL, nw), lambda i: (0, 0))],
        out_specs=[pl.BlockSpec((tm, w), lambda i: (i, 0)) for _, w in _IN_SEGS],
        out_shape=[jax.ShapeDtypeStruct((geom.t, w), BF16) for _, w in _IN_SEGS],
        compiler_params=_cparams(("parallel",)),
        name="norm1_inproj",
    )(x, mods, mods, g1, w_in_p)


def _swa_kernel(geom, q_ref, kp_ref, kc_ref, kn_ref, vp_ref, vc_ref, vn_ref, gq_ref, gk_ref,
                bdq_ref, bdk_ref, sink_ref, o_ref):
    i = pl.program_id(0)
    first = geom.pos(i, BLOCK) == 0
    last = geom.is_last(i, BLOCK)

    q = q_ref[...].astype(F32)
    msq = _dot((q * q).astype(BF16), bdq_ref[...]) * (1.0 / HEAD_DIM)
    qn = (q * lax.rsqrt(msq + EPS) * gq_ref[...] * (HEAD_DIM ** -0.5)).astype(BF16)

    k3 = jnp.concatenate([kp_ref[...], kc_ref[...], kn_ref[...]], axis=0).astype(F32)
    msk = _dot((k3 * k3).astype(BF16), bdk_ref[...]) * (1.0 / HEAD_DIM)
    kn = (k3 * lax.rsqrt(msk + EPS) * gk_ref[...]).astype(BF16)
    v3 = jnp.concatenate([vp_ref[...], vc_ref[...], vn_ref[...]], axis=0)

    qi = lax.broadcasted_iota(jnp.int32, (BLOCK, 3 * BLOCK), 0)
    ki = lax.broadcasted_iota(jnp.int32, (BLOCK, 3 * BLOCK), 1)
    rel = jnp.abs(ki - BLOCK - qi)
    k_lo = jnp.where(first, BLOCK, 0)
    k_hi = jnp.where(last, 2 * BLOCK, 3 * BLOCK)
    valid = jnp.logical_and(rel <= WINDOW, jnp.logical_and(ki >= k_lo, ki < k_hi))
    relf = rel.astype(F32)

    outs = []
    for h in range(SWA_HEADS):
        g = h // SWA_GROUP
        slope = 2.0 ** (-8.0 * (h + 1) / SWA_HEADS)
        qh = qn[:, h * HEAD_DIM:(h + 1) * HEAD_DIM]
        kg = kn[:, g * HEAD_DIM:(g + 1) * HEAD_DIM]
        vg = v3[:, g * HEAD_DIM:(g + 1) * HEAD_DIM]
        s = _dot_nt(qh, kg) - slope * relf
        s = jnp.where(valid, s, NEG_INF)
        sink = sink_ref[:, h:h + 1]
        m = jnp.maximum(jnp.max(s, axis=-1, keepdims=True), sink)
        p = jnp.exp(s - m)
        denom = jnp.sum(p, axis=-1, keepdims=True) + jnp.exp(sink - m)
        outs.append(_dot(p.astype(BF16), vg) / denom)
    o_ref[...] = jnp.concatenate(outs, axis=-1).astype(o_ref.dtype)


def _swa(geom, q, k, v, gq, gk, bdq, bdk, sink):
    nblk = geom.t // BLOCK
    prev = lambda i: (jnp.maximum(i - 1, 0), 0)
    cur = lambda i: (i, 0)
    nxt = lambda i: (jnp.minimum(i + 1, nblk - 1), 0)
    kv_spec = lambda f: pl.BlockSpec((BLOCK, A_KV), f)
    const = lambda shape: pl.BlockSpec(shape, lambda i: (0, 0))
    return pl.pallas_call(
        functools.partial(_swa_kernel, geom),
        grid=(nblk,),
        in_specs=[pl.BlockSpec((BLOCK, A_Q), cur),
                  kv_spec(prev), kv_spec(cur), kv_spec(nxt),
                  kv_spec(prev), kv_spec(cur), kv_spec(nxt),
                  const((1, A_Q)), const((1, A_KV)), const((A_Q, A_Q)), const((A_KV, A_KV)),
                  const((1, LANES))],
        out_specs=pl.BlockSpec((BLOCK, A_Q), cur),
        out_shape=jax.ShapeDtypeStruct((geom.t, A_Q), BF16),
        compiler_params=_cparams(("parallel",)),
        name="swa_attention",
    )(q, k, k, k, v, v, v, gq, gk, bdq, bdk, sink)


def _rope(x, c_ref, sa_ref, sb_ref):
    return x * c_ref[...] + pltpu.roll(x, LANES - MLA_ROPE // 2, 1) * sa_ref[...] + \
        pltpu.roll(x, MLA_ROPE // 2, 1) * sb_ref[...]


def _mla_prep_kernel(cq_ref, ckv_ref, kr_ref, gqn_ref, gkvn_ref, wq_ref, wkv_ref, gq_ref, gk_ref,
                     c_ref, sa_ref, sb_ref, q_out, k_out, vt_out):
    cq = cq_ref[...].astype(F32)
    qn = (cq * lax.rsqrt(jnp.mean(cq * cq, axis=-1, keepdims=True) + EPS) * gqn_ref[...]).astype(BF16)
    ckv = ckv_ref[...].astype(F32)
    kvn = (ckv * lax.rsqrt(jnp.mean(ckv * ckv, axis=-1, keepdims=True) + EPS) * gkvn_ref[...]).astype(BF16)
    kr = kr_ref[...].astype(F32)
    lane = lax.broadcasted_iota(jnp.int32, (1, LANES), 1)
    trow = lax.broadcasted_iota(jnp.int32, (LANES, 1), 0)
    for h in range(MLA_HEADS):
        sl = slice(h * LANES, (h + 1) * LANES)
        qh = _dot(qn, wq_ref[:, sl])
        qh = qh * lax.rsqrt(jnp.sum(qh * qh, axis=-1, keepdims=True) * (1.0 / MLA_QK) + EPS) * gq_ref[...]
        q_out[h] = (_rope(qh, c_ref, sa_ref, sb_ref) * (MLA_QK ** -0.5 * LOG2E)).astype(BF16)
        kvh = _dot(kvn, wkv_ref[:, sl])
        vt_out[h, 0] = jnp.where(trow == 0, 1.0, kvh.T).astype(BF16)
        kh = jnp.where(lane < MLA_NOPE, kvh, 0.0) + kr
        kh = kh * lax.rsqrt(jnp.sum(kh * kh, axis=-1, keepdims=True) * (1.0 / MLA_QK) + EPS) * gk_ref[...]
        k_out[h] = _rope(kh, c_ref, sa_ref, sb_ref).astype(BF16)


def _mla_prep(geom, tk, cq, ckv, kr, gqn, gkvn, wq_p, wkv, gq_p, gk_p, rope_c, rope_sa, rope_sb):
    tm = tk
    tok = lambda w: pl.BlockSpec((tm, w), lambda i: (i, 0))
    const = lambda shape: pl.BlockSpec(shape, lambda i: (0, 0))
    rope = pl.BlockSpec((tm, LANES), lambda i: (geom.pos(i, tm), 0))
    hm = pl.BlockSpec((MLA_HEADS, tm, LANES), lambda i: (0, i, 0))
    hm_t = pl.BlockSpec((MLA_HEADS, 1, LANES, tm), lambda i: (0, i, 0, 0))
    return pl.pallas_call(
        _mla_prep_kernel,
        grid=(geom.t // tm,),
        in_specs=[tok(MLA_Q_RANK), tok(MLA_KV_RANK), tok(LANES),
                  const((1, MLA_Q_RANK)), const((1, MLA_KV_RANK)),
                  const((MLA_Q_RANK, MLA_HEADS * LANES)), const((MLA_KV_RANK, MLA_HEADS * LANES)),
                  const((1, LANES)), const((1, LANES)), rope, rope, rope],
        out_specs=[hm, hm, hm_t],
        out_shape=[jax.ShapeDtypeStruct((MLA_HEADS, geom.t, LANES), BF16),
                   jax.ShapeDtypeStruct((MLA_HEADS, geom.t, LANES), BF16),
                   jax.ShapeDtypeStruct((MLA_HEADS, geom.t // tm, LANES, tm), BF16)],
        compiler_params=_cparams(("parallel",)),
        name="mla_prep",
    )(cq, ckv, kr, gqn, gkvn, wq_p, wkv, gq_p, gk_p, rope_c, rope_sa, rope_sb)


def _mla_attn_kernel(nk, q_ref, k_ref, vt_ref, o_ref, s_ref, mc_ref, p_ref, al_ref, acc_ref):
    tk = vt_ref.shape[3]
    tq = q_ref.shape[1]

    def stage_a(chunk, slot):
        off = chunk * tk
        if not isinstance(off, int):
            off = pl.multiple_of(off, tk)
        for hh in range(2):
            s = _dot_nt(k_ref[hh, pl.ds(off, tk), :], q_ref[hh])
            s_ref[slot, hh] = s
            mc_ref[slot, hh] = jnp.max(s, axis=0, keepdims=True)

    def stage_b(slot, m):
        m_out = []
        for hh in range(2):
            m_new = jnp.maximum(m[hh], mc_ref[slot, hh])
            al_ref[slot, hh] = jnp.exp2(m[hh] - m_new)
            p_ref[slot, hh] = jnp.exp2(s_ref[slot, hh] - m_new).astype(BF16)
            m_out.append(m_new)
        return tuple(m_out)

    def stage_c(chunk, slot):
        for hh in range(2):
            acc_ref[hh] = acc_ref[hh] * al_ref[slot, hh] + _dot(vt_ref[hh, chunk], p_ref[slot, hh])

    stage_a(0, 0)
    p_ref[1] = jnp.zeros(p_ref.shape[1:], BF16)
    al_ref[1] = jnp.ones(al_ref.shape[1:], F32)
    acc_ref[...] = jnp.zeros(acc_ref.shape, F32)
    m = (jnp.full((1, tq), NEG_INF, F32),) * 2

    def body(jj, m):
        j = 2 * jj
        stage_a(j + 1, 1)
        m = stage_b(0, m)
        stage_c(jnp.maximum(j - 1, 0), 1)
        stage_a(j + 2, 0)
        m = stage_b(1, m)
        stage_c(j, 0)
        return m

    m = lax.fori_loop(0, nk // 2 - 1, body, m)
    stage_a(nk - 1, 1)
    m = stage_b(0, m)
    stage_c(max(nk - 3, 0), 1)
    m = stage_b(1, m)
    stage_c(nk - 2, 0)
    stage_c(nk - 1, 1)
    o_t = jnp.concatenate([acc_ref[hh][LANES - MLA_V:, :] / acc_ref[hh][0:1, :] for hh in range(2)], axis=0)
    o_ref[...] = o_t.T.astype(o_ref.dtype)


def _mla_attn_group(geom, tk, q, k, vt, out_prev, nseq, s, tok0):
    tq = min(512, s)
    nq = s // tq
    nk = s // tk
    assert nk % 2 == 0
    qb0 = tok0 // tq
    sb0 = tok0 // s
    args = [q, k, vt]
    in_specs = [pl.BlockSpec((2, tq, LANES), lambda b, hp, i: (hp, qb0 + b * nq + i, 0)),
                pl.BlockSpec((2, s, LANES), lambda b, hp, i: (hp, sb0 + b, 0)),
                pl.BlockSpec((2, nk, LANES, tk), lambda b, hp, i: (hp, sb0 + b, 0, 0))]
    aliases = {}
    kern = functools.partial(_mla_attn_kernel, nk)
    if out_prev is not None:
        args.append(out_prev)
        in_specs.append(pl.BlockSpec(memory_space=pl.ANY))
        aliases = {3: 0}
        kern = lambda q_ref, k_ref, vt_ref, prev_ref, *rest, _k=kern: _k(q_ref, k_ref, vt_ref, *rest)
    return pl.pallas_call(
        kern,
        grid=(nseq, MLA_HEADS // 2, nq),
        in_specs=in_specs,
        out_specs=pl.BlockSpec((tq, LANES), lambda b, hp, i: (qb0 + b * nq + i, hp)),
        out_shape=jax.ShapeDtypeStruct((geom.t, MLA_HEADS * MLA_V), BF16),
        scratch_shapes=[pltpu.VMEM((2, 2, tk, tq), F32), pltpu.VMEM((2, 2, 1, tq), F32),
                        pltpu.VMEM((2, 2, tk, tq), BF16), pltpu.VMEM((2, 2, 1, tq), F32),
                        pltpu.VMEM((2, LANES, tq), F32)],
        input_output_aliases=aliases,
        compiler_params=_cparams(("parallel", "parallel", "arbitrary")),
        name="mla_attention",
    )(*args)


def _mla_attn(geom, tk, q, k, vt):
    assert geom.tp % geom.ss == 0
    out = _mla_attn_group(geom, tk, q, k, vt, None, geom.bp, geom.sp, 0)
    return _mla_attn_group(geom, tk, q, k, vt, out, geom.bs, geom.ss, geom.tp)


def _conv_kernel(geom, tc, up_ref, uc_ref, un_ref, w_ref, b_ref, gng_ref, gnb_ref, bd_ref, wpw_ref, bpw_ref,
                 o_ref, hs_ref):
    i = pl.program_id(0)
    first = geom.pos(i, tc) == 0
    last = geom.is_last(i, tc)

    def glu(u):
        u = u.astype(F32)
        return u[:, :CONV_CH] * jax.nn.sigmoid(u[:, CONV_CH:])

    hs_ref[0:HALO, :] = jnp.where(first, 0.0, glu(up_ref[...]))
    hs_ref[HALO:HALO + tc, :] = glu(uc_ref[...])
    hs_ref[HALO + tc:HALO + tc + HALO, :] = jnp.where(last, 0.0, glu(un_ref[...]))

    rows = 64 if tc % 64 == 0 else tc
    for r0 in range(0, tc, rows):
        acc = jnp.zeros((rows, CONV_CH), F32) + b_ref[...]
        for j in range(CONV_WIDTH):
            start = HALO + r0 + j - CONV_PAD
            acc = acc + hs_ref[start:start + rows, :] * w_ref[j:j + 1, :]
        mu = _split_dot(acc, bd_ref[...]) * (CONV_GROUPS / CONV_CH)
        d = acc - mu
        var = _split_dot(d * d, bd_ref[...]) * (CONV_GROUPS / CONV_CH)
        hn = d * lax.rsqrt(var + EPS) * gng_ref[...] + gnb_ref[...]
        y = _dot(_silu(hn).astype(BF16), wpw_ref[...]) + bpw_ref[...]
        o_ref[r0:r0 + rows, :] = y.astype(o_ref.dtype)


def _conv(geom, uc, dw_w, dw_b, gn_g, gn_b, bd, w_pw2, b_pw2):
    tc = geom.tile(256)
    hb = tc // HALO
    nh = geom.t // HALO
    const = lambda shape: pl.BlockSpec(shape, lambda i: (0, 0))
    return pl.pallas_call(
        functools.partial(_conv_kernel, geom, tc),
        grid=(geom.t // tc,),
        in_specs=[pl.BlockSpec((HALO, 2 * CONV_CH), lambda i: (jnp.maximum(i * hb - 1, 0), 0)),
                  pl.BlockSpec((tc, 2 * CONV_CH), lambda i: (i, 0)),
                  pl.BlockSpec((HALO, 2 * CONV_CH), lambda i: (jnp.minimum((i + 1) * hb, nh - 1), 0)),
                  const((32, CONV_CH)), const((1, CONV_CH)), const((1, CONV_CH)), const((1, CONV_CH)),
                  const((CONV_CH, CONV_CH)), const((CONV_CH, CONV_CH)), const((1, CONV_CH))],
        out_specs=pl.BlockSpec((tc, CONV_CH), lambda i: (i, 0)),
        out_shape=jax.ShapeDtypeStruct((geom.t, CONV_CH), BF16),
        scratch_shapes=[pltpu.VMEM((tc + 2 * HALO, CONV_CH), F32)],
        compiler_params=_cparams(("parallel",)),
        name="conformer_conv",
    )(uc, uc, uc, dw_w, dw_b, gn_g, gn_b, bd, w_pw2, b_pw2)


def _rms_rows(x, g):
    return x * lax.rsqrt(jnp.mean(x * x, axis=-1, keepdims=True) + EPS) * g


def _outproj_kernel(x_ref, oa_ref, ob_ref, oc_ref, ga_ref, gb_ref, gc_ref, w_ref, gate_ref, sh_ref, sc_ref,
                    g2_ref, wr_hi_ref, wr_lo_ref, br_ref, x1_ref, h2_ref, lg_ref):
    na = _rms_rows(oa_ref[...].astype(F32), ga_ref[...]).astype(BF16)
    nb = _rms_rows(ob_ref[...].astype(F32), gb_ref[...]).astype(BF16)
    nc = _rms_rows(oc_ref[...].astype(F32), gc_ref[...]).astype(BF16)
    wa = A_Q
    wb = wa + MLA_HEADS * MLA_V
    y = _dot(na, w_ref[0:wa, :]) + _dot(nb, w_ref[wa:wb, :]) + _dot(nc, w_ref[wb:, :])
    x1 = x_ref[...] + gate_ref[...] * y
    x1_ref[...] = x1
    h2 = _rms_rows(x1, g2_ref[...]) * (1.0 + sc_ref[...]) + sh_ref[...]
    hi = h2.astype(BF16)
    lo = (h2 - hi.astype(F32)).astype(BF16)
    h2_ref[...] = hi
    lg_ref[...] = _dot(hi, wr_hi_ref[...]) + _dot(lo, wr_hi_ref[...]) + _dot(hi, wr_lo_ref[...]) + br_ref[...]


def _outproj(geom, x, oa, ob, oc, ga, gb, gc, w_out, mods, g2, wr_hi, wr_lo, br):
    tm = geom.tile(256)
    tok = lambda w: pl.BlockSpec((tm, w), lambda i: (i, 0))
    const = lambda shape: pl.BlockSpec(shape, lambda i: (0, 0))
    wb = MLA_HEADS * MLA_V
    return pl.pallas_call(
        _outproj_kernel,
        grid=(geom.t // tm,),
        in_specs=[tok(D_MODEL), tok(A_Q), tok(wb), tok(CONV_CH),
                  const((1, A_Q)), const((1, wb)), const((1, CONV_CH)),
                  const((D_MODEL, D_MODEL)),
                  _mod_spec(geom, tm, 2), _mod_spec(geom, tm, 3), _mod_spec(geom, tm, 4),
                  const((1, D_MODEL)), const((D_MODEL, LANES)), const((D_MODEL, LANES)), const((1, LANES))],
        out_specs=[tok(D_MODEL), tok(D_MODEL), tok(LANES)],
        out_shape=[jax.ShapeDtypeStruct((geom.t, D_MODEL), F32),
                   jax.ShapeDtypeStruct((geom.t, D_MODEL), BF16),
                   jax.ShapeDtypeStruct((geom.t, LANES), F32)],
        compiler_params=_cparams(("parallel",)),
        name="merge_outproj_norm2",
    )(x, oa, ob, oc, ga, gb, gc, w_out, mods, mods, mods, g2, wr_hi, wr_lo, br)


def _route_kernel(lg_ref, cmb_ref):
    lg = lg_ref[...]
    lane = lax.broadcasted_iota(jnp.int32, lg.shape, 1)
    lanef = lane.astype(F32)
    big = float(2 * LANES)
    is_g = jnp.logical_and(lane >= N_EXPERTS, lane < N_EXPERTS + N_GROUPS)
    gl = jnp.where(is_g, lg, NEG_INF)
    gmax = jnp.max(gl, axis=-1, keepdims=True)
    gsum = jnp.sum(jnp.exp(gl - gmax), axis=-1, keepdims=True)
    g_val = 1.0 / gsum
    g_idx = jnp.min(jnp.where(jnp.logical_and(is_g, gl == gmax), lanef, big), axis=-1, keepdims=True) - N_EXPERTS
    lo = g_idx * EXPERTS_PER_GROUP
    is_e = jnp.logical_and(lanef >= lo, lanef < lo + EXPERTS_PER_GROUP)
    el = jnp.where(is_e, lg, NEG_INF)
    emax = jnp.max(el, axis=-1, keepdims=True)
    ee = jnp.exp(el - emax)
    e_prob = ee / jnp.sum(ee, axis=-1, keepdims=True)
    p1 = jnp.where(is_e, e_prob, -1.0)
    v1 = jnp.max(p1, axis=-1, keepdims=True)
    i1 = jnp.min(jnp.where(p1 == v1, lanef, big), axis=-1, keepdims=True)
    p2 = jnp.where(lanef == i1, -1.0, p1)
    v2 = jnp.max(p2, axis=-1, keepdims=True)
    i2 = jnp.min(jnp.where(p2 == v2, lanef, big), axis=-1, keepdims=True)
    scale = g_val / (v1 + v2)
    cmb_ref[...] = jnp.where(lanef == i1, v1 * scale, 0.0) + jnp.where(lanef == i2, v2 * scale, 0.0)


def _route(geom, logits):
    tm = geom.tile(512)
    spec = pl.BlockSpec((tm, LANES), lambda i: (i, 0))
    return pl.pallas_call(
        _route_kernel,
        grid=(geom.t // tm,),
        in_specs=[spec],
        out_specs=spec,
        out_shape=jax.ShapeDtypeStruct((geom.t, LANES), F32),
        compiler_params=_cparams(("parallel",)),
        name="moe_route",
    )(logits)


def _moe_kernel(h_ref, cmb_ref, wg_ref, wu_ref, wd_ref, x1_ref, gate_ref, o_ref, acc_ref):
    e = pl.program_id(1)

    @pl.when(e == 0)
    def _():
        acc_ref[...] = jnp.zeros_like(acc_ref)

    h = h_ref[...]
    lane = lax.broadcasted_iota(jnp.int32, cmb_ref.shape, 1)
    cw = jnp.sum(jnp.where(lane == e, cmb_ref[...], 0.0), axis=-1, keepdims=True)
    act = _silu(_dot(h, wg_ref[...])) * _dot(h, wu_ref[...]) * cw
    acc_ref[...] += _dot(act.astype(BF16), wd_ref[...])

    @pl.when(e == pl.num_programs(1) - 1)
    def _():
        o_ref[...] = x1_ref[...] + gate_ref[...] * acc_ref[...]


def _moe(geom, h2, cmb, wg, wu, wd, x1, mods):
    tm = geom.tile(512)
    tok = lambda w: pl.BlockSpec((tm, w), lambda i, e: (i, 0))
    return pl.pallas_call(
        _moe_kernel,
        grid=(geom.t // tm, N_EXPERTS),
        in_specs=[tok(D_MODEL), tok(LANES),
                  pl.BlockSpec((None, D_MODEL, D_EXPERT), lambda i, e: (e, 0, 0)),
                  pl.BlockSpec((None, D_MODEL, D_EXPERT), lambda i, e: (e, 0, 0)),
                  pl.BlockSpec((None, D_EXPERT, D_MODEL), lambda i, e: (e, 0, 0)),
                  tok(D_MODEL),
                  pl.BlockSpec((None, 1, D_MODEL), lambda i, e: (geom.batch(i, tm) * N_MOD + 5, 0, 0))],
        out_specs=tok(D_MODEL),
        out_shape=jax.ShapeDtypeStruct((geom.t, D_MODEL), F32),
        scratch_shapes=[pltpu.VMEM((tm, D_MODEL), F32)],
        compiler_params=_cparams(("parallel", "arbitrary")),
        name="moe_experts",
    )(h2, cmb, wg, wu, wd, x1, mods)


def _block_diag_ones(n, blk):
    idx = np.arange(n) // blk
    return jnp.asarray((idx[:, None] == idx[None, :]).astype(np.float32), dtype=BF16)


def _rope_tables(smax):
    half = MLA_ROPE // 2
    inv_freq = 1.0 / (ROPE_BASE ** (jnp.arange(0, MLA_ROPE, 2, dtype=F32) / MLA_ROPE))
    ang = jnp.arange(smax, dtype=F32)[:, None] * inv_freq[None, :]
    cos, sin = jnp.cos(ang), jnp.sin(ang)
    ones = jnp.ones((smax, MLA_NOPE), F32)
    zeros = lambda w: jnp.zeros((smax, w), F32)
    pad = LANES - MLA_QK
    c = jnp.concatenate([ones, cos, cos, zeros(pad)], axis=1)
    sa = jnp.concatenate([zeros(MLA_NOPE), -sin, zeros(half), zeros(pad)], axis=1)
    sb = jnp.concatenate([zeros(MLA_NOPE), zeros(half), sin, zeros(pad)], axis=1)
    return c, sa, sb


def _pad_cols(w, n):
    return jnp.pad(w, ((0, 0), (0, n - w.shape[1])))


def _layer_weights(l, w_in, mla_w_q_up, mla_q_gain, mla_k_gain, swa_q_gain, swa_k_gain, swa_sink,
                   moe_w_group, moe_b_group, moe_w_expert, moe_b_expert):
    wi = w_in[l]
    kr = jnp.pad(wi[:, A_IN + MLA_Q_RANK + MLA_KV_RANK:A_IN + B_IN], ((0, 0), (MLA_NOPE, LANES - MLA_QK)))
    w_in_p = jnp.concatenate([wi[:, :A_IN], wi[:, A_IN:A_IN + MLA_Q_RANK + MLA_KV_RANK], kr, wi[:, A_IN + B_IN:]],
                             axis=1).astype(BF16)
    wq = mla_w_q_up[l].reshape(MLA_Q_RANK, MLA_HEADS, MLA_QK)
    wq_p = jnp.pad(wq, ((0, 0), (0, 0), (0, LANES - MLA_QK))).reshape(MLA_Q_RANK, MLA_HEADS * LANES).astype(BF16)
    gq_p = _pad_cols(mla_q_gain[l][None, :], LANES)
    gk_p = _pad_cols(mla_k_gain[l][None, :], LANES)
    gq_a = jnp.tile(swa_q_gain[l], SWA_HEADS)[None, :]
    gk_a = jnp.tile(swa_k_gain[l], SWA_KV_HEADS)[None, :]
    sink = _pad_cols(swa_sink[l][None, :], LANES)
    wr = _pad_cols(jnp.concatenate([moe_w_expert[l], moe_w_group[l]], axis=1), LANES)
    wr_hi = wr.astype(BF16)
    wr_lo = (wr - wr_hi.astype(F32)).astype(BF16)
    br = _pad_cols(jnp.concatenate([moe_b_expert[l], moe_b_group[l]])[None, :], LANES)
    return w_in_p, wq_p, gq_p, gk_p, gq_a, gk_a, sink, wr_hi, wr_lo, br


def kernel(x_prompt, x_sample, c_prompt, c_sample, w_ada, b_ada, norm1_g, norm2_g, w_in, swa_q_gain, swa_k_gain, swa_sink, mla_q_norm_g, mla_w_q_up, mla_kv_norm_g, mla_w_kv_up, mla_q_gain, mla_k_gain, conv_dw_w, conv_dw_b, conv_gn_g, conv_gn_b, conv_w_pw2, conv_b_pw2, out_norm_a, out_norm_b, out_norm_c, w_out, moe_w_group, moe_b_group, moe_w_expert, moe_b_expert, moe_w_gate, moe_w_up, moe_w_down):
    bp, sp, d = x_prompt.shape
    bs, ss, _ = x_sample.shape
    assert d == D_MODEL
    geom = _Geom(bp, sp, bs, ss)
    depth = w_ada.shape[0]

    x = jnp.concatenate([x_prompt.reshape(bp * sp, d), x_sample.reshape(bs * ss, d)], axis=0)
    c = jnp.concatenate([c_prompt, c_sample], axis=0)
    rows = -(-geom.nb // 8) * 8
    c_pad = jnp.pad(c, ((0, rows - geom.nb), (0, 0)))
    mods_all = _modulation(c_pad, w_ada, b_ada)

    rope_c, rope_sa, rope_sb = _rope_tables(max(sp, ss))
    bdq = _block_diag_ones(A_Q, HEAD_DIM)
    bdk = _block_diag_ones(A_KV, HEAD_DIM)
    bdc = _block_diag_ones(CONV_CH, CONV_CH // CONV_GROUPS)
    row = lambda v: v[None, :]

    for l in range(depth):
        (w_in_p, wq_p, gq_p, gk_p, gq_a, gk_a, sink, wr_hi, wr_lo, br) = _layer_weights(
            l, w_in, mla_w_q_up, mla_q_gain, mla_k_gain, swa_q_gain, swa_k_gain, swa_sink,
            moe_w_group, moe_b_group, moe_w_expert, moe_b_expert)
        mods = mods_all[l, :geom.nb].reshape(geom.nb * N_MOD, 1, D_MODEL)

        q_a, k_a, v_a, cq, ckv, kr, uc = _inproj(geom, x, mods, row(norm1_g[l]), w_in_p)
        out_a = _swa(geom, q_a, k_a, v_a, gq_a, gk_a, bdq, bdk, sink)
        tk = geom.tile(512)
        q_b, k_b, vt_b = _mla_prep(geom, tk, cq, ckv, kr, row(mla_q_norm_g[l]), row(mla_kv_norm_g[l]), wq_p,
                                   mla_w_kv_up[l].astype(BF16), gq_p, gk_p, rope_c, rope_sa, rope_sb)
        out_b = _mla_attn(geom, tk, q_b, k_b, vt_b)
        dw_w = jnp.pad(conv_dw_w[l].reshape(CONV_WIDTH, CONV_CH), ((0, 32 - CONV_WIDTH), (0, 0)))
        out_c = _conv(geom, uc, dw_w, row(conv_dw_b[l]), row(conv_gn_g[l]), row(conv_gn_b[l]), bdc,
                      conv_w_pw2[l].astype(BF16), row(conv_b_pw2[l]))
        x1, h2, logits = _outproj(geom, x, out_a, out_b, out_c, row(out_norm_a[l]), row(out_norm_b[l]),
                                  row(out_norm_c[l]), w_out[l].astype(BF16), mods, row(norm2_g[l]),
                                  wr_hi, wr_lo, br)
        cmb = _route(geom, logits)
        x = _moe(geom, h2, cmb, moe_w_gate[l].astype(BF16), moe_w_up[l].astype(BF16),
                 moe_w_down[l].astype(BF16), x1, mods)

    y_prompt = x[:geom.tp].reshape(bp, sp, d)
    y_sample = x[geom.tp:].reshape(bs, ss, d)
    return (y_prompt, y_sample)
```

```python
import functools

import numpy as np
import jax
import jax.numpy as jnp
from jax import lax
from jax.experimental import pallas as pl
from jax.experimental.pallas import tpu as pltpu

F32 = jnp.float32
BF16 = jnp.bfloat16

EPS = 1e-6
NEG_INF = -1e30
LOG2E = 1.4426950408889634

D_MODEL = 1024
HEAD_DIM = 64
SWA_HEADS = 6
SWA_KV_HEADS = 2
SWA_GROUP = SWA_HEADS // SWA_KV_HEADS
WINDOW = 128
BLOCK = WINDOW
MLA_HEADS = 6
MLA_Q_RANK = 256
MLA_KV_RANK = 128
MLA_NOPE = 64
MLA_ROPE = 32
MLA_QK = MLA_NOPE + MLA_ROPE
MLA_V = 64
ROPE_BASE = 10000.0
CONV_CH = 256
CONV_GROUPS = 4
CONV_WIDTH = 31
CONV_PAD = (CONV_WIDTH - 1) // 2
A_Q = SWA_HEADS * HEAD_DIM
A_KV = SWA_KV_HEADS * HEAD_DIM
A_IN = A_Q + 2 * A_KV
B_IN = MLA_Q_RANK + MLA_KV_RANK + MLA_ROPE
N_GROUPS = 4
EXPERTS_PER_GROUP = 8
N_EXPERTS = N_GROUPS * EXPERTS_PER_GROUP
D_EXPERT = 256
N_MOD = 6

LANES = 128
SLAB = D_MODEL // 128
TME = 256
HALO = 16
VMEM_LIMIT = 48 * 1024 * 1024


class _Geom:
    def __init__(self, bp, sp, bs, ss):
        self.bp, self.sp, self.bs, self.ss = bp, sp, bs, ss
        self.tp = bp * sp
        self.t = bp * sp + bs * ss
        self.nb = bp + bs

    def tile(self, target):
        t = target
        while self.sp % t or self.ss % t:
            t //= 2
        return t

    def batch(self, i, tm):
        npt = self.tp // tm
        return jnp.where(i < npt, i // (self.sp // tm), self.bp + (i - npt) // (self.ss // tm))

    def pos(self, i, tm):
        npt = self.tp // tm
        return jnp.where(i < npt, i % (self.sp // tm), (i - npt) % (self.ss // tm))

    def is_last(self, i, tm):
        npt = self.tp // tm
        return jnp.where(i < npt, i % (self.sp // tm) == self.sp // tm - 1,
                         (i - npt) % (self.ss // tm) == self.ss // tm - 1)


def _cparams(sem):
    return pltpu.CompilerParams(dimension_semantics=sem, vmem_limit_bytes=VMEM_LIMIT)


def _silu(x):
    return x * jax.nn.sigmoid(x)


def _dot(a, b):
    return jnp.dot(a, b, preferred_element_type=F32)


def _dot_nt(a, b):
    return lax.dot_general(a, b, (((1,), (1,)), ((), ())), preferred_element_type=F32)


def _split_dot(x, w):
    hi = x.astype(BF16)
    lo = (x - hi.astype(F32)).astype(BF16)
    return _dot(hi, w) + _dot(lo, w)


def _mod_kernel(c_ref, w_ref, b_ref, o_ref):
    c = c_ref[...]
    o_ref[...] = _dot(_silu(c).astype(BF16), w_ref[...].astype(BF16)) + b_ref[...]


def _modulation(c_pad, w_ada, b_ada):
    L, d, n = w_ada.shape
    tn = 768
    rows = c_pad.shape[0]
    return pl.pallas_call(
        _mod_kernel,
        grid=(L, n // tn),
        in_specs=[pl.BlockSpec((rows, d), lambda l, j: (0, 0)),
                  pl.BlockSpec((None, d, tn), lambda l, j: (l, 0, j)),
                  pl.BlockSpec((None, 1, tn), lambda l, j: (l, 0, j))],
        out_specs=pl.BlockSpec((None, rows, tn), lambda l, j: (l, 0, j)),
        out_shape=jax.ShapeDtypeStruct((L, rows, n), F32),
        compiler_params=_cparams(("arbitrary", "arbitrary")),
        name="adaln_mod",
    )(c_pad, w_ada, b_ada.reshape(L, 1, n))


_IN_SEGS = (("q", A_Q), ("k", A_KV), ("v", A_KV), ("cq", MLA_Q_RANK), ("ckv", MLA_KV_RANK), ("kr", LANES),
            ("uc", 2 * CONV_CH))


def _inproj_kernel(x_ref, sh_ref, sc_ref, g_ref, w_ref, *out_refs):
    x = x_ref[...]
    ms = jnp.mean(x * x, axis=-1, keepdims=True)
    h = x * lax.rsqrt(ms + EPS) * g_ref[...]
    h = (h * (1.0 + sc_ref[...]) + sh_ref[...]).astype(BF16)
    off = 0
    for (_, width), o_ref in zip(_IN_SEGS, out_refs):
        o_ref[...] = _dot(h, w_ref[:, off:off + width]).astype(o_ref.dtype)
        off += width


def _mod_spec(geom, tm, k):
    return pl.BlockSpec((None, 1, D_MODEL), lambda i: (geom.batch(i, tm) * N_MOD + k, 0, 0))


def _inproj(geom, x, mods, g1, w_in_p):
    tm = geom.tile(512)
    nw = w_in_p.shape[1]
    return pl.pallas_call(
        _inproj_kernel,
        grid=(geom.t // tm,),
        in_specs=[pl.BlockSpec((tm, D_MODEL), lambda i: (i, 0)),
                  _mod_spec(geom, tm, 0), _mod_spec(geom, tm, 1),
                  pl.BlockSpec((1, D_MODEL), lambda i: (0, 0)),
                  pl.BlockSpec((D_MODEL, nw), lambda i: (0, 0))],
        out_specs=[pl.BlockSpec((tm, w), lambda i: (i, 0)) for _, w in _IN_SEGS],
        out_shape=[jax.ShapeDtypeStruct((geom.t, w), BF16) for _, w in _IN_SEGS],
        compiler_params=_cparams(("parallel",)),
        name="norm1_inproj",
    )(x, mods, mods, g1, w_in_p)


def _swa_kernel(geom, q_ref, kp_ref, kc_ref, kn_ref, vp_ref, vc_ref, vn_ref, gq_ref, gk_ref,
                bdq_ref, bdk_ref, sink_ref, o_ref):
    i = pl.program_id(0)
    first = geom.pos(i, BLOCK) == 0
    last = geom.is_last(i, BLOCK)

    q = q_ref[...].astype(F32)
    msq = _dot((q * q).astype(BF16), bdq_ref[...]) * (1.0 / HEAD_DIM)
    qn = (q * lax.rsqrt(msq + EPS) * gq_ref[...] * (HEAD_DIM ** -0.5)).astype(BF16)

    k3 = jnp.concatenate([kp_ref[...], kc_ref[...], kn_ref[...]], axis=0).astype(F32)
    msk = _dot((k3 * k3).astype(BF16), bdk_ref[...]) * (1.0 / HEAD_DIM)
    kn = (k3 * lax.rsqrt(msk + EPS) * gk_ref[...]).astype(BF16)
    v3 = jnp.concatenate([vp_ref[...], vc_ref[...], vn_ref[...]], axis=0)

    qi = lax.broadcasted_iota(jnp.int32, (BLOCK, 3 * BLOCK), 0)
    ki = lax.broadcasted_iota(jnp.int32, (BLOCK, 3 * BLOCK), 1)
    rel = jnp.abs(ki - BLOCK - qi)
    k_lo = jnp.where(first, BLOCK, 0)
    k_hi = jnp.where(last, 2 * BLOCK, 3 * BLOCK)
    valid = jnp.logical_and(rel <= WINDOW, jnp.logical_and(ki >= k_lo, ki < k_hi))
    relf = rel.astype(F32)

    outs = []
    for h in range(SWA_HEADS):
        g = h // SWA_GROUP
        slope = 2.0 ** (-8.0 * (h + 1) / SWA_HEADS)
        qh = qn[:, h * HEAD_DIM:(h + 1) * HEAD_DIM]
        kg = kn[:, g * HEAD_DIM:(g + 1) * HEAD_DIM]
        vg = v3[:, g * HEAD_DIM:(g + 1) * HEAD_DIM]
        s = _dot_nt(qh, kg) - slope * relf
        s = jnp.where(valid, s, NEG_INF)
        sink = sink_ref[:, h:h + 1]
        m = jnp.maximum(jnp.max(s, axis=-1, keepdims=True), sink)
        p = jnp.exp(s - m)
        denom = jnp.sum(p, axis=-1, keepdims=True) + jnp.exp(sink - m)
        outs.append(_dot(p.astype(BF16), vg) / denom)
    o_ref[...] = jnp.concatenate(outs, axis=-1).astype(o_ref.dtype)


def _swa(geom, q, k, v, gq, gk, bdq, bdk, sink):
    nblk = geom.t // BLOCK
    prev = lambda i: (jnp.maximum(i - 1, 0), 0)
    cur = lambda i: (i, 0)
    nxt = lambda i: (jnp.minimum(i + 1, nblk - 1), 0)
    kv_spec = lambda f: pl.BlockSpec((BLOCK, A_KV), f)
    const = lambda shape: pl.BlockSpec(shape, lambda i: (0, 0))
    return pl.pallas_call(
        functools.partial(_swa_kernel, geom),
        grid=(nblk,),
        in_specs=[pl.BlockSpec((BLOCK, A_Q), cur),
                  kv_spec(prev), kv_spec(cur), kv_spec(nxt),
                  kv_spec(prev), kv_spec(cur), kv_spec(nxt),
                  const((1, A_Q)), const((1, A_KV)), const((A_Q, A_Q)), const((A_KV, A_KV)),
                  const((1, LANES))],
        out_specs=pl.BlockSpec((BLOCK, A_Q), cur),
        out_shape=jax.ShapeDtypeStruct((geom.t, A_Q), BF16),
        compiler_params=_cparams(("parallel",)),
        name="swa_attention",
    )(q, k, k, k, v, v, v, gq, gk, bdq, bdk, sink)


def _rope(x, c_ref, sa_ref, sb_ref):
    return x * c_ref[...] + pltpu.roll(x, LANES - MLA_ROPE // 2, 1) * sa_ref[...] + \
        pltpu.roll(x, MLA_ROPE // 2, 1) * sb_ref[...]


def _mla_prep_kernel(cq_ref, ckv_ref, kr_ref, gqn_ref, gkvn_ref, wq_ref, wkv_ref, gq_ref, gk_ref,
                     c_ref, sa_ref, sb_ref, q_out, k_out, vt_out):
    cq = cq_ref[...].astype(F32)
    qn = (cq * lax.rsqrt(jnp.mean(cq * cq, axis=-1, keepdims=True) + EPS) * gqn_ref[...]).astype(BF16)
    ckv = ckv_ref[...].astype(F32)
    kvn = (ckv * lax.rsqrt(jnp.mean(ckv * ckv, axis=-1, keepdims=True) + EPS) * gkvn_ref[...]).astype(BF16)
    kr = kr_ref[...].astype(F32)
    lane = lax.broadcasted_iota(jnp.int32, (1, LANES), 1)
    trow = lax.broadcasted_iota(jnp.int32, (LANES, 1), 0)
    for h in range(MLA_HEADS):
        sl = slice(h * LANES, (h + 1) * LANES)
        qh = _dot(qn, wq_ref[:, sl])
        qh = qh * lax.rsqrt(jnp.sum(qh * qh, axis=-1, keepdims=True) * (1.0 / MLA_QK) + EPS) * gq_ref[...]
        q_out[h] = (_rope(qh, c_ref, sa_ref, sb_ref) * (MLA_QK ** -0.5 * LOG2E)).astype(BF16)
        kvh = _dot(kvn, wkv_ref[:, sl])
        vt_out[h, 0] = jnp.where(trow == 0, 1.0, kvh.T).astype(BF16)
        kh = jnp.where(lane < MLA_NOPE, kvh, 0.0) + kr
        kh = kh * lax.rsqrt(jnp.sum(kh * kh, axis=-1, keepdims=True) * (1.0 / MLA_QK) + EPS) * gk_ref[...]
        k_out[h] = _rope(kh, c_ref, sa_ref, sb_ref).astype(BF16)


def _mla_prep(geom, tk, cq, ckv, kr, gqn, gkvn, wq_p, wkv, gq_p, gk_p, rope_c, rope_sa, rope_sb):
    tm = tk
    tok = lambda w: pl.BlockSpec((tm, w), lambda i: (i, 0))
    const = lambda shape: pl.BlockSpec(shape, lambda i: (0, 0))
    rope = pl.BlockSpec((tm, LANES), lambda i: (geom.pos(i, tm), 0))
    hm = pl.BlockSpec((MLA_HEADS, tm, LANES), lambda i: (0, i, 0))
    hm_t = pl.BlockSpec((MLA_HEADS, 1, LANES, tm), lambda i: (0, i, 0, 0))
    return pl.pallas_call(
        _mla_prep_kernel,
        grid=(geom.t // tm,),
        in_specs=[tok(MLA_Q_RANK), tok(MLA_KV_RANK), tok(LANES),
                  const((1, MLA_Q_RANK)), const((1, MLA_KV_RANK)),
                  const((MLA_Q_RANK, MLA_HEADS * LANES)), const((MLA_KV_RANK, MLA_HEADS * LANES)),
                  const((1, LANES)), const((1, LANES)), rope, rope, rope],
        out_specs=[hm, hm, hm_t],
        out_shape=[jax.ShapeDtypeStruct((MLA_HEADS, geom.t, LANES), BF16),
                   jax.ShapeDtypeStruct((MLA_HEADS, geom.t, LANES), BF16),
                   jax.ShapeDtypeStruct((MLA_HEADS, geom.t // tm, LANES, tm), BF16)],
        compiler_params=_cparams(("parallel",)),
        name="mla_prep",
    )(cq, ckv, kr, gqn, gkvn, wq_p, wkv, gq_p, gk_p, rope_c, rope_sa, rope_sb)


def _mla_attn_kernel(nk, q_ref, k_ref, vt_ref, o_ref, s_ref, mc_ref, p_ref, al_ref, acc_ref):
    tk = vt_ref.shape[3]
    tq = q_ref.shape[1]

    def stage_a(chunk, slot):
        off = chunk * tk
        if not isinstance(off, int):
            off = pl.multiple_of(off, tk)
        for hh in range(2):
            s = _dot_nt(k_ref[hh, pl.ds(off, tk), :], q_ref[hh])
            s_ref[slot, hh] = s
            mc_ref[slot, hh] = jnp.max(s, axis=0, keepdims=True)

    def stage_b(slot, m):
        m_out = []
        for hh in range(2):
            m_new = jnp.maximum(m[hh], mc_ref[slot, hh])
            al_ref[slot, hh] = jnp.exp2(m[hh] - m_new)
            p_ref[slot, hh] = jnp.exp2(s_ref[slot, hh] - m_new).astype(BF16)
            m_out.append(m_new)
        return tuple(m_out)

    def stage_c(chunk, slot):
        for hh in range(2):
            acc_ref[hh] = acc_ref[hh] * al_ref[slot, hh] + _dot(vt_ref[hh, chunk], p_ref[slot, hh])

    stage_a(0, 0)
    p_ref[1] = jnp.zeros(p_ref.shape[1:], BF16)
    al_ref[1] = jnp.ones(al_ref.shape[1:], F32)
    acc_ref[...] = jnp.zeros(acc_ref.shape, F32)
    m = (jnp.full((1, tq), NEG_INF, F32),) * 2

    def body(jj, m):
        j = 2 * jj
        stage_a(j + 1, 1)
        m = stage_b(0, m)
        stage_c(jnp.maximum(j - 1, 0), 1)
        stage_a(j + 2, 0)
        m = stage_b(1, m)
        stage_c(j, 0)
        return m

    m = lax.fori_loop(0, nk // 2 - 1, body, m)
    stage_a(nk - 1, 1)
    m = stage_b(0, m)
    stage_c(max(nk - 3, 0), 1)
    m = stage_b(1, m)
    stage_c(nk - 2, 0)
    stage_c(nk - 1, 1)
    o_t = jnp.concatenate([acc_ref[hh][LANES - MLA_V:, :] / acc_ref[hh][0:1, :] for hh in range(2)], axis=0)
    o_ref[...] = o_t.T.astype(o_ref.dtype)


def _mla_attn_group(geom, tk, q, k, vt, out_prev, nseq, s, tok0):
    tq = min(512, s)
    nq = s // tq
    nk = s // tk
    assert nk % 2 == 0
    qb0 = tok0 // tq
    sb0 = tok0 // s
    args = [q, k, vt]
    in_specs = [pl.BlockSpec((2, tq, LANES), lambda b, hp, i: (hp, qb0 + b * nq + i, 0)),
                pl.BlockSpec((2, s, LANES), lambda b, hp, i: (hp, sb0 + b, 0)),
                pl.BlockSpec((2, nk, LANES, tk), lambda b, hp, i: (hp, sb0 + b, 0, 0))]
    aliases = {}
    kern = functools.partial(_mla_attn_kernel, nk)
    if out_prev is not None:
        args.append(out_prev)
        in_specs.append(pl.BlockSpec(memory_space=pl.ANY))
        aliases = {3: 0}
        kern = lambda q_ref, k_ref, vt_ref, prev_ref, *rest, _k=kern: _k(q_ref, k_ref, vt_ref, *rest)
    return pl.pallas_call(
        kern,
        grid=(nseq, MLA_HEADS // 2, nq),
        in_specs=in_specs,
        out_specs=pl.BlockSpec((tq, LANES), lambda b, hp, i: (qb0 + b * nq + i, hp)),
        out_shape=jax.ShapeDtypeStruct((geom.t, MLA_HEADS * MLA_V), BF16),
        scratch_shapes=[pltpu.VMEM((2, 2, tk, tq), F32), pltpu.VMEM((2, 2, 1, tq), F32),
                        pltpu.VMEM((2, 2, tk, tq), BF16), pltpu.VMEM((2, 2, 1, tq), F32),
                        pltpu.VMEM((2, LANES, tq), F32)],
        input_output_aliases=aliases,
        compiler_params=_cparams(("parallel", "parallel", "arbitrary")),
        name="mla_attention",
    )(*args)


def _mla_attn(geom, tk, q, k, vt):
    assert geom.tp % geom.ss == 0
    out = _mla_attn_group(geom, tk, q, k, vt, None, geom.bp, geom.sp, 0)
    return _mla_attn_group(geom, tk, q, k, vt, out, geom.bs, geom.ss, geom.tp)


def _conv_kernel(geom, tc, up_ref, uc_ref, un_ref, w_ref, b_ref, gng_ref, gnb_ref, bd_ref, wpw_ref, bpw_ref,
                 o_ref, hs_ref):
    i = pl.program_id(0)
    first = geom.pos(i, tc) == 0
    last = geom.is_last(i, tc)

    def glu(u):
        u = u.astype(F32)
        return u[:, :CONV_CH] * jax.nn.sigmoid(u[:, CONV_CH:])

    hs_ref[0:HALO, :] = jnp.where(first, 0.0, glu(up_ref[...]))
    hs_ref[HALO:HALO + tc, :] = glu(uc_ref[...])
    hs_ref[HALO + tc:HALO + tc + HALO, :] = jnp.where(last, 0.0, glu(un_ref[...]))

    rows = 64 if tc % 64 == 0 else tc
    for r0 in range(0, tc, rows):
        acc = jnp.zeros((rows, CONV_CH), F32) + b_ref[...]
        for j in range(CONV_WIDTH):
            start = HALO + r0 + j - CONV_PAD
            acc = acc + hs_ref[start:start + rows, :] * w_ref[j:j + 1, :]
        mu = _split_dot(acc, bd_ref[...]) * (CONV_GROUPS / CONV_CH)
        d = acc - mu
        var = _split_dot(d * d, bd_ref[...]) * (CONV_GROUPS / CONV_CH)
        hn = d * lax.rsqrt(var + EPS) * gng_ref[...] + gnb_ref[...]
        y = _dot(_silu(hn).astype(BF16), wpw_ref[...]) + bpw_ref[...]
        o_ref[r0:r0 + rows, :] = y.astype(o_ref.dtype)


def _conv(geom, uc, dw_w, dw_b, gn_g, gn_b, bd, w_pw2, b_pw2):
    tc = geom.tile(256)
    hb = tc // HALO
    nh = geom.t // HALO
    const = lambda shape: pl.BlockSpec(shape, lambda i: (0, 0))
    return pl.pallas_call(
        functools.partial(_conv_kernel, geom, tc),
        grid=(geom.t // tc,),
        in_specs=[pl.BlockSpec((HALO, 2 * CONV_CH), lambda i: (jnp.maximum(i * hb - 1, 0), 0)),
                  pl.BlockSpec((tc, 2 * CONV_CH), lambda i: (i, 0)),
                  pl.BlockSpec((HALO, 2 * CONV_CH), lambda i: (jnp.minimum((i + 1) * hb, nh - 1), 0)),
                  const((32, CONV_CH)), const((1, CONV_CH)), const((1, CONV_CH)), const((1, CONV_CH)),
                  const((CONV_CH, CONV_CH)), const((CONV_CH, CONV_CH)), const((1, CONV_CH))],
        out_specs=pl.BlockSpec((tc, CONV_CH), lambda i: (i, 0)),
        out_shape=jax.ShapeDtypeStruct((geom.t, CONV_CH), BF16),
        scratch_shapes=[pltpu.VMEM((tc + 2 * HALO, CONV_CH), F32)],
        compiler_params=_cparams(("parallel",)),
        name="conformer_conv",
    )(uc, uc, uc, dw_w, dw_b, gn_g, gn_b, bd, w_pw2, b_pw2)


def _rms_rows(x, g):
    return x * lax.rsqrt(jnp.mean(x * x, axis=-1, keepdims=True) + EPS) * g


def _store_slabs(ref, x, base=0):
    rows = x.shape[0]
    for c in range(SLAB):
        ref[pl.ds(base + c, rows, stride=SLAB), :] = x[:, c * LANES:(c + 1) * LANES]


def _load_slabs(ref, rows, base=0):
    return jnp.concatenate([ref[pl.ds(base + c, rows, stride=SLAB), :] for c in range(SLAB)], axis=1)


def _slab_row(ref, r):
    return ref.at[pl.ds(pl.multiple_of(r * SLAB, SLAB), SLAB)]


def _outproj_kernel(x_ref, oa_ref, ob_ref, oc_ref, ga_ref, gb_ref, gc_ref, w_ref, gate_ref, sh_ref, sc_ref,
                    g2_ref, wr_hi_ref, wr_lo_ref, br_ref, x1_ref, h2_ref, lg_ref):
    na = _rms_rows(oa_ref[...].astype(F32), ga_ref[...]).astype(BF16)
    nb = _rms_rows(ob_ref[...].astype(F32), gb_ref[...]).astype(BF16)
    nc = _rms_rows(oc_ref[...].astype(F32), gc_ref[...]).astype(BF16)
    wa = A_Q
    wb = wa + MLA_HEADS * MLA_V
    y = _dot(na, w_ref[0:wa, :]) + _dot(nb, w_ref[wa:wb, :]) + _dot(nc, w_ref[wb:, :])
    x1 = x_ref[...] + gate_ref[...] * y
    x1_ref[...] = x1
    h2 = _rms_rows(x1, g2_ref[...]) * (1.0 + sc_ref[...]) + sh_ref[...]
    hi = h2.astype(BF16)
    lo = (h2 - hi.astype(F32)).astype(BF16)
    _store_slabs(h2_ref, h2)
    lg_ref[...] = _dot(hi, wr_hi_ref[...]) + _dot(lo, wr_hi_ref[...]) + _dot(hi, wr_lo_ref[...]) + br_ref[...]


def _outproj(geom, x, oa, ob, oc, ga, gb, gc, w_out, mods, g2, wr_hi, wr_lo, br):
    tm = geom.tile(256)
    tok = lambda w: pl.BlockSpec((tm, w), lambda i: (i, 0))
    const = lambda shape: pl.BlockSpec(shape, lambda i: (0, 0))
    wb = MLA_HEADS * MLA_V
    return pl.pallas_call(
        _outproj_kernel,
        grid=(geom.t // tm,),
        in_specs=[tok(D_MODEL), tok(A_Q), tok(wb), tok(CONV_CH),
                  const((1, A_Q)), const((1, wb)), const((1, CONV_CH)),
                  const((D_MODEL, D_MODEL)),
                  _mod_spec(geom, tm, 2), _mod_spec(geom, tm, 3), _mod_spec(geom, tm, 4),
                  const((1, D_MODEL)), const((D_MODEL, LANES)), const((D_MODEL, LANES)), const((1, LANES))],
        out_specs=[tok(D_MODEL), pl.BlockSpec((tm * SLAB, LANES), lambda i: (i, 0)), tok(LANES)],
        out_shape=[jax.ShapeDtypeStruct((geom.t, D_MODEL), F32),
                   jax.ShapeDtypeStruct((geom.t * SLAB, LANES), F32),
                   jax.ShapeDtypeStruct((geom.t, LANES), F32)],
        compiler_params=_cparams(("parallel",)),
        name="merge_outproj_norm2",
    )(x, oa, ob, oc, ga, gb, gc, w_out, mods, mods, mods, g2, wr_hi, wr_lo, br)


def _route_kernel(lg_ref, rt_ref, cnt_ref):
    lg = lg_ref[...]
    lane = lax.broadcasted_iota(jnp.int32, lg.shape, 1)
    lanef = lane.astype(F32)
    big = float(2 * LANES)
    is_g = jnp.logical_and(lane >= N_EXPERTS, lane < N_EXPERTS + N_GROUPS)
    gl = jnp.where(is_g, lg, NEG_INF)
    gmax = jnp.max(gl, axis=-1, keepdims=True)
    gsum = jnp.sum(jnp.exp(gl - gmax), axis=-1, keepdims=True)
    g_val = 1.0 / gsum
    g_idx = jnp.min(jnp.where(jnp.logical_and(is_g, gl == gmax), lanef, big), axis=-1, keepdims=True) - N_EXPERTS
    lo = g_idx * EXPERTS_PER_GROUP
    is_e = jnp.logical_and(lanef >= lo, lanef < lo + EXPERTS_PER_GROUP)
    el = jnp.where(is_e, lg, NEG_INF)
    emax = jnp.max(el, axis=-1, keepdims=True)
    ee = jnp.exp(el - emax)
    e_prob = ee / jnp.sum(ee, axis=-1, keepdims=True)
    p1 = jnp.where(is_e, e_prob, -1.0)
    v1 = jnp.max(p1, axis=-1, keepdims=True)
    i1 = jnp.min(jnp.where(p1 == v1, lanef, big), axis=-1, keepdims=True)
    p2 = jnp.where(lanef == i1, -1.0, p1)
    v2 = jnp.max(p2, axis=-1, keepdims=True)
    i2 = jnp.min(jnp.where(p2 == v2, lanef, big), axis=-1, keepdims=True)
    scale = g_val / (v1 + v2)
    rt_ref[...] = (jnp.where(lane == 0, i1, 0.0) + jnp.where(lane == 1, i2, 0.0) +
                   jnp.where(lane == 2, v1 * scale, 0.0) + jnp.where(lane == 3, v2 * scale, 0.0))
    chosen = jnp.logical_or(lanef == i1, lanef == i2)
    cnt = jnp.sum(jnp.where(chosen, 1.0, 0.0), axis=0, keepdims=True)
    cnt_ref[...] = jnp.broadcast_to(cnt, cnt_ref.shape)


def _route(geom, tm, logits):
    spec = pl.BlockSpec((tm, LANES), lambda i: (i, 0))
    n = geom.t // tm
    return pl.pallas_call(
        _route_kernel,
        grid=(n,),
        in_specs=[spec],
        out_specs=[spec, pl.BlockSpec((8, LANES), lambda i: (i, 0))],
        out_shape=[jax.ShapeDtypeStruct((geom.t, LANES), F32), jax.ShapeDtypeStruct((n * 8, LANES), F32)],
        compiler_params=_cparams(("parallel",)),
        name="moe_route",
    )(logits)


def _slots_kernel(rt_ref, start_ref, tri_ref, sl_ref):
    rt = rt_ref[...]
    lane = lax.broadcasted_iota(jnp.int32, rt.shape, 1)
    lanef = lane.astype(F32)
    oh1 = lanef == rt[:, 0:1]
    oh2 = lanef == rt[:, 1:2]
    chosen = jnp.where(jnp.logical_or(oh1, oh2), 1.0, 0.0).astype(BF16)
    first = _dot(tri_ref[...], chosen) + start_ref[...]
    s1 = jnp.sum(jnp.where(oh1, first, 0.0), axis=-1, keepdims=True)
    s2 = jnp.sum(jnp.where(oh2, first, 0.0), axis=-1, keepdims=True)
    sl_ref[...] = jnp.where(lane == 0, s1, 0.0) + jnp.where(lane == 1, s2, 0.0)


def _slots(geom, tm, rt, start, tri):
    spec = pl.BlockSpec((tm, LANES), lambda i: (i, 0))
    return pl.pallas_call(
        _slots_kernel,
        grid=(geom.t // tm,),
        in_specs=[spec, pl.BlockSpec((None, 1, LANES), lambda i: (i, 0, 0)), pl.BlockSpec((tm, tm), lambda i: (0, 0))],
        out_specs=spec,
        out_shape=jax.ShapeDtypeStruct((geom.t, LANES), F32),
        compiler_params=_cparams(("parallel",)),
        name="moe_slots",
    )(rt, start, tri)


def _plan(geom, tm, cnt, n_tiles_max):
    n = geom.t // tm
    cnt_t = cnt.reshape(n, 8, LANES)[:, 0, :]
    incl = jnp.cumsum(cnt_t, axis=0)
    tiles_e = jnp.ceil(incl[-1] / TME)
    end_e = jnp.cumsum(tiles_e)
    start = (incl - cnt_t + ((end_e - tiles_e) * TME)[None, :]).reshape(n, 1, LANES)
    n_used = end_e[-1].astype(jnp.int32)
    tile_idx = jnp.minimum(jnp.arange(n_tiles_max, dtype=jnp.int32), n_used - 1)
    tile_exp = jnp.sum(tile_idx[:, None] >= end_e[None, :N_EXPERTS].astype(jnp.int32), axis=1).astype(jnp.int32)
    return start, tile_exp, tile_idx, n_used.reshape(1)


def _smem_slots(slots_f, tm):
    n = slots_f.shape[0] // tm
    return slots_f[:, :2].astype(jnp.int32).reshape(n, tm, 2).transpose(0, 2, 1).reshape(n, 1, 2 * tm)


def _scatter_kernel(sl_ref, h_ref, xs_in, xs_out, sem):
    del xs_in
    tm = h_ref.shape[0] // SLAB

    def body(r, carry):
        pltpu.make_async_copy(_slab_row(h_ref, r), _slab_row(xs_out, sl_ref[0, r]), sem).start()
        pltpu.make_async_copy(_slab_row(h_ref, r), _slab_row(xs_out, sl_ref[0, tm + r]), sem).start()
        return carry

    lax.fori_loop(0, tm, body, 0, unroll=8)
    for _ in range(2):
        pltpu.make_async_copy(h_ref, xs_out.at[pl.ds(0, tm * SLAB)], sem).wait()


def _scatter(geom, tm, slots_smem, h2, xs):
    return pl.pallas_call(
        _scatter_kernel,
        grid=(geom.t // tm,),
        in_specs=[pl.BlockSpec((None, 1, 2 * tm), lambda i: (i, 0, 0), memory_space=pltpu.SMEM),
                  pl.BlockSpec((tm * SLAB, LANES), lambda i: (i, 0)),
                  pl.BlockSpec(memory_space=pl.ANY)],
        out_specs=pl.BlockSpec(memory_space=pl.ANY),
        out_shape=jax.ShapeDtypeStruct(xs.shape, xs.dtype),
        scratch_shapes=[pltpu.SemaphoreType.DMA(())],
        input_output_aliases={2: 0},
        compiler_params=_cparams(("arbitrary",)),
        name="moe_scatter",
    )(slots_smem, h2, xs)


def _experts_kernel(te_ref, ti_ref, nu_ref, xs_ref, wg_ref, wu_ref, wd_ref, ys_ref):
    del te_ref, ti_ref

    @pl.when(pl.program_id(0) < nu_ref[0])
    def _():
        x = _load_slabs(xs_ref, TME).astype(BF16)
        act = (_silu(_dot(x, wg_ref[...])) * _dot(x, wu_ref[...])).astype(BF16)
        _store_slabs(ys_ref, _dot(act, wd_ref[...]))


def _experts(xs, tile_exp, tile_idx, n_used, wg, wu, wd):
    n_tiles_max = xs.shape[0] // (TME * SLAB)
    slab = pl.BlockSpec((TME * SLAB, LANES), lambda n, te, ti, nu: (ti[n], 0))
    wspec = lambda a, b: pl.BlockSpec((None, a, b), lambda n, te, ti, nu: (te[n], 0, 0))
    return pl.pallas_call(
        _experts_kernel,
        grid_spec=pltpu.PrefetchScalarGridSpec(
            num_scalar_prefetch=3,
            grid=(n_tiles_max,),
            in_specs=[slab, wspec(D_MODEL, D_EXPERT), wspec(D_MODEL, D_EXPERT), wspec(D_EXPERT, D_MODEL)],
            out_specs=slab),
        out_shape=jax.ShapeDtypeStruct(xs.shape, F32),
        compiler_params=_cparams(("arbitrary",)),
        name="moe_experts",
    )(tile_exp, tile_idx, n_used, xs, wg, wu, wd)


def _combine_kernel(sl_cur, sl_nxt, rt_ref, x1_ref, gate_ref, ys_hbm, o_ref, buf1, buf2, sem):
    i = pl.program_id(0)
    tm = x1_ref.shape[0]

    def issue(sl_ref, slot):
        def body(r, carry):
            dst = slot * tm + r
            pltpu.make_async_copy(_slab_row(ys_hbm, sl_ref[0, r]), _slab_row(buf1, dst), sem.at[slot]).start()
            pltpu.make_async_copy(_slab_row(ys_hbm, sl_ref[0, tm + r]), _slab_row(buf2, dst), sem.at[slot]).start()
            return carry
        lax.fori_loop(0, tm, body, 0, unroll=8)

    @pl.when(i == 0)
    def _():
        issue(sl_cur, 0)

    @pl.when(i + 1 < pl.num_programs(0))
    def _():
        issue(sl_nxt, (i + 1) % 2)

    slot = i % 2
    base = pl.multiple_of(slot * (tm * SLAB), tm * SLAB)
    for buf in (buf1, buf2):
        pltpu.make_async_copy(ys_hbm.at[pl.ds(0, tm * SLAB)], buf.at[pl.ds(base, tm * SLAB)], sem.at[slot]).wait()
    rt = rt_ref[...]
    y = rt[:, 2:3] * _load_slabs(buf1, tm, base) + rt[:, 3:4] * _load_slabs(buf2, tm, base)
    o_ref[...] = x1_ref[...] + gate_ref[...] * y


def _combine(geom, tm, slots_smem, rt, x1, mods, ys):
    n = geom.t // tm
    tok = lambda w: pl.BlockSpec((tm, w), lambda i: (i, 0))
    smem = lambda f: pl.BlockSpec((None, 1, 2 * tm), f, memory_space=pltpu.SMEM)
    return pl.pallas_call(
        _combine_kernel,
        grid=(n,),
        in_specs=[smem(lambda i: (i, 0, 0)), smem(lambda i: (jnp.minimum(i + 1, n - 1), 0, 0)),
                  tok(LANES), tok(D_MODEL), _mod_spec(geom, tm, 5),
                  pl.BlockSpec(memory_space=pl.ANY)],
        out_specs=tok(D_MODEL),
        out_shape=jax.ShapeDtypeStruct((geom.t, D_MODEL), F32),
        scratch_shapes=[pltpu.VMEM((2 * tm * SLAB, LANES), F32), pltpu.VMEM((2 * tm * SLAB, LANES), F32),
                        pltpu.SemaphoreType.DMA((2,))],
        compiler_params=_cparams(("arbitrary",)),
        name="moe_combine",
    )(slots_smem, slots_smem, rt, x1, mods, ys)


def _block_diag_ones(n, blk):
    idx = np.arange(n) // blk
    return jnp.asarray((idx[:, None] == idx[None, :]).astype(np.float32), dtype=BF16)


def _rope_tables(smax):
    half = MLA_ROPE // 2
    inv_freq = 1.0 / (ROPE_BASE ** (jnp.arange(0, MLA_ROPE, 2, dtype=F32) / MLA_ROPE))
    ang = jnp.arange(smax, dtype=F32)[:, None] * inv_freq[None, :]
    cos, sin = jnp.cos(ang), jnp.sin(ang)
    ones = jnp.ones((smax, MLA_NOPE), F32)
    zeros = lambda w: jnp.zeros((smax, w), F32)
    pad = LANES - MLA_QK
    c = jnp.concatenate([ones, cos, cos, zeros(pad)], axis=1)
    sa = jnp.concatenate([zeros(MLA_NOPE), -sin, zeros(half), zeros(pad)], axis=1)
    sb = jnp.concatenate([zeros(MLA_NOPE), zeros(half), sin, zeros(pad)], axis=1)
    return c, sa, sb


def _pad_cols(w, n):
    return jnp.pad(w, ((0, 0), (0, n - w.shape[1])))


def _layer_weights(l, w_in, mla_w_q_up, mla_q_gain, mla_k_gain, swa_q_gain, swa_k_gain, swa_sink,
                   moe_w_group, moe_b_group, moe_w_expert, moe_b_expert):
    wi = w_in[l]
    kr = jnp.pad(wi[:, A_IN + MLA_Q_RANK + MLA_KV_RANK:A_IN + B_IN], ((0, 0), (MLA_NOPE, LANES - MLA_QK)))
    w_in_p = jnp.concatenate([wi[:, :A_IN], wi[:, A_IN:A_IN + MLA_Q_RANK + MLA_KV_RANK], kr, wi[:, A_IN + B_IN:]],
                             axis=1).astype(BF16)
    wq = mla_w_q_up[l].reshape(MLA_Q_RANK, MLA_HEADS, MLA_QK)
    wq_p = jnp.pad(wq, ((0, 0), (0, 0), (0, LANES - MLA_QK))).reshape(MLA_Q_RANK, MLA_HEADS * LANES).astype(BF16)
    gq_p = _pad_cols(mla_q_gain[l][None, :], LANES)
    gk_p = _pad_cols(mla_k_gain[l][None, :], LANES)
    gq_a = jnp.tile(swa_q_gain[l], SWA_HEADS)[None, :]
    gk_a = jnp.tile(swa_k_gain[l], SWA_KV_HEADS)[None, :]
    sink = _pad_cols(swa_sink[l][None, :], LANES)
    wr = _pad_cols(jnp.concatenate([moe_w_expert[l], moe_w_group[l]], axis=1), LANES)
    wr_hi = wr.astype(BF16)
    wr_lo = (wr - wr_hi.astype(F32)).astype(BF16)
    br = _pad_cols(jnp.concatenate([moe_b_expert[l], moe_b_group[l]])[None, :], LANES)
    return w_in_p, wq_p, gq_p, gk_p, gq_a, gk_a, sink, wr_hi, wr_lo, br


def kernel(x_prompt, x_sample, c_prompt, c_sample, w_ada, b_ada, norm1_g, norm2_g, w_in, swa_q_gain, swa_k_gain, swa_sink, mla_q_norm_g, mla_w_q_up, mla_kv_norm_g, mla_w_kv_up, mla_q_gain, mla_k_gain, conv_dw_w, conv_dw_b, conv_gn_g, conv_gn_b, conv_w_pw2, conv_b_pw2, out_norm_a, out_norm_b, out_norm_c, w_out, moe_w_group, moe_b_group, moe_w_expert, moe_b_expert, moe_w_gate, moe_w_up, moe_w_down):
    bp, sp, d = x_prompt.shape
    bs, ss, _ = x_sample.shape
    assert d == D_MODEL
    geom = _Geom(bp, sp, bs, ss)
    depth = w_ada.shape[0]

    x = jnp.concatenate([x_prompt.reshape(bp * sp, d), x_sample.reshape(bs * ss, d)], axis=0)
    c = jnp.concatenate([c_prompt, c_sample], axis=0)
    rows = -(-geom.nb // 8) * 8
    c_pad = jnp.pad(c, ((0, rows - geom.nb), (0, 0)))
    mods_all = _modulation(c_pad, w_ada, b_ada)

    rope_c, rope_sa, rope_sb = _rope_tables(max(sp, ss))
    bdq = _block_diag_ones(A_Q, HEAD_DIM)
    bdk = _block_diag_ones(A_KV, HEAD_DIM)
    bdc = _block_diag_ones(CONV_CH, CONV_CH // CONV_GROUPS)
    row = lambda v: v[None, :]
    tm_moe = geom.tile(256)
    n_tiles_max = 2 * geom.t // TME + N_EXPERTS
    tri = jnp.asarray(np.tril(np.ones((tm_moe, tm_moe), np.float32), -1), dtype=BF16)
    xs = jnp.zeros((n_tiles_max * TME * SLAB, LANES), F32)

    for l in range(depth):
        (w_in_p, wq_p, gq_p, gk_p, gq_a, gk_a, sink, wr_hi, wr_lo, br) = _layer_weights(
            l, w_in, mla_w_q_up, mla_q_gain, mla_k_gain, swa_q_gain, swa_k_gain, swa_sink,
            moe_w_group, moe_b_group, moe_w_expert, moe_b_expert)
        mods = mods_all[l, :geom.nb].reshape(geom.nb * N_MOD, 1, D_MODEL)

        q_a, k_a, v_a, cq, ckv, kr, uc = _inproj(geom, x, mods, row(norm1_g[l]), w_in_p)
        out_a = _swa(geom, q_a, k_a, v_a, gq_a, gk_a, bdq, bdk, sink)
        tk = geom.tile(512)
        q_b, k_b, vt_b = _mla_prep(geom, tk, cq, ckv, kr, row(mla_q_norm_g[l]), row(mla_kv_norm_g[l]), wq_p,
                                   mla_w_kv_up[l].astype(BF16), gq_p, gk_p, rope_c, rope_sa, rope_sb)
        out_b = _mla_attn(geom, tk, q_b, k_b, vt_b)
        dw_w = jnp.pad(conv_dw_w[l].reshape(CONV_WIDTH, CONV_CH), ((0, 32 - CONV_WIDTH), (0, 0)))
        out_c = _conv(geom, uc, dw_w, row(conv_dw_b[l]), row(conv_gn_g[l]), row(conv_gn_b[l]), bdc,
                      conv_w_pw2[l].astype(BF16), row(conv_b_pw2[l]))
        x1, h2, logits = _outproj(geom, x, out_a, out_b, out_c, row(out_norm_a[l]), row(out_norm_b[l]),
                                  row(out_norm_c[l]), w_out[l].astype(BF16), mods, row(norm2_g[l]),
                                  wr_hi, wr_lo, br)
        rt, cnt = _route(geom, tm_moe, logits)
        start, tile_exp, tile_idx, n_used = _plan(geom, tm_moe, cnt, n_tiles_max)
        slots = _smem_slots(_slots(geom, tm_moe, rt, start, tri), tm_moe)
        xs = _scatter(geom, tm_moe, slots, h2, xs)
        ys = _experts(xs, tile_exp, tile_idx, n_used, moe_w_gate[l].astype(BF16), moe_w_up[l].astype(BF16),
                      moe_w_down[l].astype(BF16))
        x = _combine(geom, tm_moe, slots, rt, x1, mods, ys)

    y_prompt = x[:geom.tp].reshape(bp, sp, d)
    y_sample = x[geom.tp:].reshape(bs, ss, d)
    return (y_prompt, y_sample)
```

```python
import functools

import numpy as np
import jax
import jax.numpy as jnp
from jax import lax
from jax.experimental import pallas as pl
from jax.experimental.pallas import tpu as pltpu

F32 = jnp.float32
BF16 = jnp.bfloat16

EPS = 1e-6
NEG_INF = -1e30
LOG2E = 1.4426950408889634

D_MODEL = 1024
HEAD_DIM = 64
SWA_HEADS = 6
SWA_KV_HEADS = 2
SWA_GROUP = SWA_HEADS // SWA_KV_HEADS
WINDOW = 128
BLOCK = WINDOW
MLA_HEADS = 6
MLA_Q_RANK = 256
MLA_KV_RANK = 128
MLA_NOPE = 64
MLA_ROPE = 32
MLA_QK = MLA_NOPE + MLA_ROPE
MLA_V = 64
MLA_VT_ROWS = MLA_V + 16
ROPE_BASE = 10000.0
CONV_CH = 256
CONV_GROUPS = 4
CONV_WIDTH = 31
CONV_PAD = (CONV_WIDTH - 1) // 2
A_Q = SWA_HEADS * HEAD_DIM
A_KV = SWA_KV_HEADS * HEAD_DIM
A_IN = A_Q + 2 * A_KV
B_IN = MLA_Q_RANK + MLA_KV_RANK + MLA_ROPE
N_GROUPS = 4
EXPERTS_PER_GROUP = 8
N_EXPERTS = N_GROUPS * EXPERTS_PER_GROUP
D_EXPERT = 256
N_MOD = 6

LANES = 128
SLAB = D_MODEL // 128
TME = 256
ROW_UNROLL = 8
HALO = 16
VMEM_LIMIT = 48 * 1024 * 1024


class _Geom:
    def __init__(self, bp, sp, bs, ss):
        self.bp, self.sp, self.bs, self.ss = bp, sp, bs, ss
        self.tp = bp * sp
        self.t = bp * sp + bs * ss
        self.nb = bp + bs

    def tile(self, target):
        t = target
        while self.sp % t or self.ss % t:
            t //= 2
        return t

    def batch(self, i, tm):
        npt = self.tp // tm
        return jnp.where(i < npt, i // (self.sp // tm), self.bp + (i - npt) // (self.ss // tm))

    def pos(self, i, tm):
        npt = self.tp // tm
        return jnp.where(i < npt, i % (self.sp // tm), (i - npt) % (self.ss // tm))

    def is_last(self, i, tm):
        npt = self.tp // tm
        return jnp.where(i < npt, i % (self.sp // tm) == self.sp // tm - 1,
                         (i - npt) % (self.ss // tm) == self.ss // tm - 1)


def _cparams(sem):
    return pltpu.CompilerParams(dimension_semantics=sem, vmem_limit_bytes=VMEM_LIMIT)


def _silu(x):
    return x * jax.nn.sigmoid(x)


def _dot(a, b):
    return jnp.dot(a, b, preferred_element_type=F32)


def _dot_nt(a, b):
    return lax.dot_general(a, b, (((1,), (1,)), ((), ())), preferred_element_type=F32)


def _split_dot(x, w):
    hi = x.astype(BF16)
    lo = (x - hi.astype(F32)).astype(BF16)
    return _dot(hi, w) + _dot(lo, w)


def _mod_kernel(c_ref, w_ref, b_ref, o_ref):
    c = c_ref[...]
    o_ref[...] = _dot(_silu(c).astype(BF16), w_ref[...].astype(BF16)) + b_ref[...]


def _modulation(c_pad, w_ada, b_ada):
    L, d, n = w_ada.shape
    tn = 768
    rows = c_pad.shape[0]
    return pl.pallas_call(
        _mod_kernel,
        grid=(L, n // tn),
        in_specs=[pl.BlockSpec((rows, d), lambda l, j: (0, 0)),
                  pl.BlockSpec((None, d, tn), lambda l, j: (l, 0, j)),
                  pl.BlockSpec((None, 1, tn), lambda l, j: (l, 0, j))],
        out_specs=pl.BlockSpec((None, rows, tn), lambda l, j: (l, 0, j)),
        out_shape=jax.ShapeDtypeStruct((L, rows, n), F32),
        compiler_params=_cparams(("arbitrary", "arbitrary")),
        name="adaln_mod",
    )(c_pad, w_ada, b_ada.reshape(L, 1, n))


_IN_SEGS = (("q", A_Q), ("k", A_KV), ("v", A_KV), ("cq", MLA_Q_RANK), ("ckv", MLA_KV_RANK), ("kr", LANES),
            ("uc", 2 * CONV_CH))


def _inproj_kernel(x_ref, sh_ref, sc_ref, g_ref, w_ref, *out_refs):
    x = x_ref[...]
    ms = jnp.mean(x * x, axis=-1, keepdims=True)
    h = x * lax.rsqrt(ms + EPS) * g_ref[...]
    h = (h * (1.0 + sc_ref[...]) + sh_ref[...]).astype(BF16)
    off = 0
    for (_, width), o_ref in zip(_IN_SEGS, out_refs):
        o_ref[...] = _dot(h, w_ref[:, off:off + width]).astype(o_ref.dtype)
        off += width


def _mod_spec(geom, tm, k):
    return pl.BlockSpec((None, 1, D_MODEL), lambda i: (geom.batch(i, tm) * N_MOD + k, 0, 0))


def _inproj(geom, x, mods, g1, w_in_p):
    tm = geom.tile(512)
    nw = w_in_p.shape[1]
    return pl.pallas_call(
        _inproj_kernel,
        grid=(geom.t // tm,),
        in_specs=[pl.BlockSpec((tm, D_MODEL), lambda i: (i, 0)),
                  _mod_spec(geom, tm, 0), _mod_spec(geom, tm, 1),
                  pl.BlockSpec((1, D_MODEL), lambda i: (0, 0)),
                  pl.BlockSpec((D_MODEL, nw), lambda i: (0, 0))],
        out_specs=[pl.BlockSpec((tm, w), lambda i: (i, 0)) for _, w in _IN_SEGS],
        out_shape=[jax.ShapeDtypeStruct((geom.t, w), BF16) for _, w in _IN_SEGS],
        compiler_params=_cparams(("parallel",)),
        name="norm1_inproj",
    )(x, mods, mods, g1, w_in_p)


def _swa_kernel(geom, nb, q_ref, kp_ref, kc_ref, kn_ref, vp_ref, vc_ref, vn_ref, gq_ref, gk_ref,
                bdq_ref, bdk_ref, sink_ref, bias_ref, o_ref):
    i = pl.program_id(0)
    first = (geom.pos(i, nb * BLOCK) == 0).astype(jnp.int32)
    last = geom.is_last(i, nb * BLOCK).astype(jnp.int32)

    q = q_ref[...].astype(F32)
    msq = _dot((q * q).astype(BF16), bdq_ref[...]) * (1.0 / HEAD_DIM)
    q_t = (q * lax.rsqrt(msq + EPS) * gq_ref[...] * (HEAD_DIM ** -0.5 * LOG2E)).T.astype(BF16)

    k_all = jnp.concatenate([kp_ref[...], kc_ref[...], kn_ref[...]], axis=0).astype(F32)
    msk = _dot((k_all * k_all).astype(BF16), bdk_ref[...]) * (1.0 / HEAD_DIM)
    k_all = (k_all * lax.rsqrt(msk + EPS) * gk_ref[...]).astype(BF16)
    v_all = jnp.concatenate([vp_ref[...], vc_ref[...], vn_ref[...]], axis=0).astype(F32)
    ones_rows = 16
    v_t = jnp.concatenate([v_all.T, jnp.ones((ones_rows, v_all.shape[0]), F32)], axis=0).astype(BF16)
    zeros = jnp.zeros((HEAD_DIM, SWA_GROUP * BLOCK), BF16)

    for b in range(nb):
        cls = (first if b == 0 else 0) + (2 * last if b == nb - 1 else 0)
        kb = k_all[b * BLOCK:(b + 3) * BLOCK, :]
        vb = v_t[:, b * BLOCK:(b + 3) * BLOCK]
        outs = []
        for g in range(SWA_KV_HEADS):
            heads = range(g * SWA_GROUP, (g + 1) * SWA_GROUP)
            qg = jnp.concatenate([q_t[h * HEAD_DIM:(h + 1) * HEAD_DIM, b * BLOCK:(b + 1) * BLOCK] for h in heads],
                                 axis=1)
            q_pad = jnp.concatenate([qg if j == g else zeros for j in range(SWA_KV_HEADS)], axis=0)
            s = _dot(kb, q_pad) + bias_ref[cls, g]
            sink = sink_ref[g]
            m = jnp.maximum(jnp.max(s, axis=0, keepdims=True), sink)
            p = jnp.exp2(s - m).astype(BF16)
            o = _dot(vb, p)
            denom = o[A_KV:A_KV + 1, :] + jnp.exp2(sink - m)
            og = o[g * HEAD_DIM:(g + 1) * HEAD_DIM, :] / denom
            outs += [og[:, j * BLOCK:(j + 1) * BLOCK] for j in range(SWA_GROUP)]
        o_ref[b * BLOCK:(b + 1) * BLOCK, :] = jnp.concatenate(outs, axis=0).T.astype(o_ref.dtype)


def _swa_bias():
    k = np.arange(3 * BLOCK)[:, None]
    q = np.arange(BLOCK)[None, :]
    rel = np.abs(k - BLOCK - q)
    out = np.zeros((4, SWA_KV_HEADS, 3 * BLOCK, SWA_GROUP * BLOCK), np.float32)
    for c in range(4):
        k_lo = BLOCK if c & 1 else 0
        k_hi = 2 * BLOCK if c & 2 else 3 * BLOCK
        valid = (rel <= WINDOW) & (k >= k_lo) & (k < k_hi)
        for g in range(SWA_KV_HEADS):
            for j in range(SWA_GROUP):
                slope = 2.0 ** (-8.0 * (g * SWA_GROUP + j + 1) / SWA_HEADS)
                out[c, g, :, j * BLOCK:(j + 1) * BLOCK] = np.where(valid, -slope * LOG2E * rel, NEG_INF)
    return jnp.asarray(out)


def _swa(geom, q, k, v, gq, gk, bdq, bdk, sink, bias):
    ts = geom.tile(512)
    nb = ts // BLOCK
    nblk = geom.t // BLOCK
    prev = lambda i: (jnp.maximum(i * nb - 1, 0), 0)
    cur = lambda i: (i, 0)
    nxt = lambda i: (jnp.minimum((i + 1) * nb, nblk - 1), 0)
    halo = lambda f: pl.BlockSpec((BLOCK, A_KV), f)
    const = lambda shape: pl.BlockSpec(shape, lambda i: (0,) * len(shape))
    return pl.pallas_call(
        functools.partial(_swa_kernel, geom, nb),
        grid=(geom.t // ts,),
        in_specs=[pl.BlockSpec((ts, A_Q), cur),
                  halo(prev), pl.BlockSpec((ts, A_KV), cur), halo(nxt),
                  halo(prev), pl.BlockSpec((ts, A_KV), cur), halo(nxt),
                  const((1, A_Q)), const((1, A_KV)), const((A_Q, A_Q)), const((A_KV, A_KV)),
                  const((SWA_KV_HEADS, 1, SWA_GROUP * BLOCK)), const(bias.shape)],
        out_specs=pl.BlockSpec((ts, A_Q), cur),
        out_shape=jax.ShapeDtypeStruct((geom.t, A_Q), BF16),
        compiler_params=_cparams(("parallel",)),
        name="swa_attention",
    )(q, k, k, k, v, v, v, gq, gk, bdq, bdk, sink, bias)


def _rope(x, c_ref, sa_ref, sb_ref):
    return x * c_ref[...] + pltpu.roll(x, LANES - MLA_ROPE // 2, 1) * sa_ref[...] + \
        pltpu.roll(x, MLA_ROPE // 2, 1) * sb_ref[...]


def _mla_prep_kernel(cq_ref, ckv_ref, kr_ref, gqn_ref, gkvn_ref, wq_ref, wkv_ref, gq_ref, gk_ref,
                     c_ref, sa_ref, sb_ref, q_out, k_out, vt_out):
    cq = cq_ref[...].astype(F32)
    qn = (cq * lax.rsqrt(jnp.mean(cq * cq, axis=-1, keepdims=True) + EPS) * gqn_ref[...]).astype(BF16)
    ckv = ckv_ref[...].astype(F32)
    kvn = (ckv * lax.rsqrt(jnp.mean(ckv * ckv, axis=-1, keepdims=True) + EPS) * gkvn_ref[...]).astype(BF16)
    kr = kr_ref[...].astype(F32)
    lane = lax.broadcasted_iota(jnp.int32, (1, LANES), 1)
    ones = jnp.ones((MLA_VT_ROWS - MLA_V, cq.shape[0]), F32)
    for h in range(MLA_HEADS):
        sl = slice(h * LANES, (h + 1) * LANES)
        qh = _dot(qn, wq_ref[:, sl])
        qh = qh * lax.rsqrt(jnp.sum(qh * qh, axis=-1, keepdims=True) * (1.0 / MLA_QK) + EPS) * gq_ref[...]
        q_out[h] = (_rope(qh, c_ref, sa_ref, sb_ref) * (MLA_QK ** -0.5 * LOG2E)).astype(BF16)
        kvh = _dot(kvn, wkv_ref[:, sl])
        vt_out[h, 0] = jnp.concatenate([kvh.T[MLA_NOPE:, :], ones], axis=0).astype(BF16)
        kh = jnp.where(lane < MLA_NOPE, kvh, 0.0) + kr
        kh = kh * lax.rsqrt(jnp.sum(kh * kh, axis=-1, keepdims=True) * (1.0 / MLA_QK) + EPS) * gk_ref[...]
        k_out[h] = _rope(kh, c_ref, sa_ref, sb_ref).astype(BF16)


def _mla_prep(geom, tk, cq, ckv, kr, gqn, gkvn, wq_p, wkv, gq_p, gk_p, rope_c, rope_sa, rope_sb):
    tm = tk
    tok = lambda w: pl.BlockSpec((tm, w), lambda i: (i, 0))
    const = lambda shape: pl.BlockSpec(shape, lambda i: (0, 0))
    rope = pl.BlockSpec((tm, LANES), lambda i: (geom.pos(i, tm), 0))
    hm = pl.BlockSpec((MLA_HEADS, tm, LANES), lambda i: (0, i, 0))
    hm_t = pl.BlockSpec((MLA_HEADS, 1, MLA_VT_ROWS, tm), lambda i: (0, i, 0, 0))
    return pl.pallas_call(
        _mla_prep_kernel,
        grid=(geom.t // tm,),
        in_specs=[tok(MLA_Q_RANK), tok(MLA_KV_RANK), tok(LANES),
                  const((1, MLA_Q_RANK)), const((1, MLA_KV_RANK)),
                  const((MLA_Q_RANK, MLA_HEADS * LANES)), const((MLA_KV_RANK, MLA_HEADS * LANES)),
                  const((1, LANES)), const((1, LANES)), rope, rope, rope],
        out_specs=[hm, hm, hm_t],
        out_shape=[jax.ShapeDtypeStruct((MLA_HEADS, geom.t, LANES), BF16),
                   jax.ShapeDtypeStruct((MLA_HEADS, geom.t, LANES), BF16),
                   jax.ShapeDtypeStruct((MLA_HEADS, geom.t // tm, MLA_VT_ROWS, tm), BF16)],
        compiler_params=_cparams(("parallel",)),
        name="mla_prep",
    )(cq, ckv, kr, gqn, gkvn, wq_p, wkv, gq_p, gk_p, rope_c, rope_sa, rope_sb)


def _mla_attn_kernel(nk, q_ref, k_ref, vt_ref, o_ref, s_ref, mc_ref, p_ref, al_ref, acc_ref):
    tk = vt_ref.shape[3]
    tq = q_ref.shape[1]

    def stage_a(chunk, slot):
        off = chunk * tk
        if not isinstance(off, int):
            off = pl.multiple_of(off, tk)
        for hh in range(2):
            s = _dot_nt(k_ref[hh, pl.ds(off, tk), :], q_ref[hh])
            s_ref[slot, hh] = s
            mc_ref[slot, hh] = jnp.max(s, axis=0, keepdims=True)

    def stage_b(slot, m):
        m_out = []
        for hh in range(2):
            m_new = jnp.maximum(m[hh], mc_ref[slot, hh])
            al_ref[slot, hh] = jnp.exp2(m[hh] - m_new)
            p_ref[slot, hh] = jnp.exp2(s_ref[slot, hh] - m_new).astype(BF16)
            m_out.append(m_new)
        return tuple(m_out)

    def stage_c(chunk, slot):
        for hh in range(2):
            acc_ref[hh] = acc_ref[hh] * al_ref[slot, hh] + _dot(vt_ref[hh, chunk], p_ref[slot, hh])

    stage_a(0, 0)
    p_ref[1] = jnp.zeros(p_ref.shape[1:], BF16)
    al_ref[1] = jnp.ones(al_ref.shape[1:], F32)
    acc_ref[...] = jnp.zeros(acc_ref.shape, F32)
    m = (jnp.full((1, tq), NEG_INF, F32),) * 2

    def body(jj, m):
        j = 2 * jj
        stage_a(j + 1, 1)
        m = stage_b(0, m)
        stage_c(jnp.maximum(j - 1, 0), 1)
        stage_a(j + 2, 0)
        m = stage_b(1, m)
        stage_c(j, 0)
        return m

    m = lax.fori_loop(0, nk // 2 - 1, body, m)
    stage_a(nk - 1, 1)
    m = stage_b(0, m)
    stage_c(max(nk - 3, 0), 1)
    m = stage_b(1, m)
    stage_c(nk - 2, 0)
    stage_c(nk - 1, 1)
    o_t = jnp.concatenate([acc_ref[hh][:MLA_V, :] / acc_ref[hh][MLA_V:MLA_V + 1, :] for hh in range(2)], axis=0)
    o_ref[...] = o_t.T.astype(o_ref.dtype)


def _mla_attn_group(geom, tk, q, k, vt, out_prev, nseq, s, tok0):
    tq = min(512, s)
    nq = s // tq
    nk = s // tk
    assert nk % 2 == 0
    qb0 = tok0 // tq
    sb0 = tok0 // s
    args = [q, k, vt]
    in_specs = [pl.BlockSpec((2, tq, LANES), lambda b, hp, i: (hp, qb0 + b * nq + i, 0)),
                pl.BlockSpec((2, s, LANES), lambda b, hp, i: (hp, sb0 + b, 0)),
                pl.BlockSpec((2, nk, MLA_VT_ROWS, tk), lambda b, hp, i: (hp, sb0 + b, 0, 0))]
    aliases = {}
    kern = functools.partial(_mla_attn_kernel, nk)
    if out_prev is not None:
        args.append(out_prev)
        in_specs.append(pl.BlockSpec(memory_space=pl.ANY))
        aliases = {3: 0}
        kern = lambda q_ref, k_ref, vt_ref, prev_ref, *rest, _k=kern: _k(q_ref, k_ref, vt_ref, *rest)
    return pl.pallas_call(
        kern,
        grid=(nseq, MLA_HEADS // 2, nq),
        in_specs=in_specs,
        out_specs=pl.BlockSpec((tq, LANES), lambda b, hp, i: (qb0 + b * nq + i, hp)),
        out_shape=jax.ShapeDtypeStruct((geom.t, MLA_HEADS * MLA_V), BF16),
        scratch_shapes=[pltpu.VMEM((2, 2, tk, tq), F32), pltpu.VMEM((2, 2, 1, tq), F32),
                        pltpu.VMEM((2, 2, tk, tq), BF16), pltpu.VMEM((2, 2, 1, tq), F32),
                        pltpu.VMEM((2, MLA_VT_ROWS, tq), F32)],
        input_output_aliases=aliases,
        compiler_params=_cparams(("parallel", "parallel", "arbitrary")),
        name="mla_attention",
    )(*args)


def _mla_attn(geom, tk, q, k, vt):
    assert geom.tp % geom.ss == 0
    out = _mla_attn_group(geom, tk, q, k, vt, None, geom.bp, geom.sp, 0)
    return _mla_attn_group(geom, tk, q, k, vt, out, geom.bs, geom.ss, geom.tp)


def _conv_kernel(geom, tc, up_ref, uc_ref, un_ref, w_ref, b_ref, gng_ref, gnb_ref, bd_ref, wpw_ref, bpw_ref,
                 o_ref, hs_ref):
    i = pl.program_id(0)
    first = geom.pos(i, tc) == 0
    last = geom.is_last(i, tc)

    def glu(u):
        u = u.astype(F32)
        return u[:, :CONV_CH] * jax.nn.sigmoid(u[:, CONV_CH:])

    hs_ref[0:HALO, :] = jnp.where(first, 0.0, glu(up_ref[...]))
    hs_ref[HALO:HALO + tc, :] = glu(uc_ref[...])
    hs_ref[HALO + tc:HALO + tc + HALO, :] = jnp.where(last, 0.0, glu(un_ref[...]))

    rows = 64 if tc % 64 == 0 else tc
    for r0 in range(0, tc, rows):
        acc = jnp.zeros((rows, CONV_CH), F32) + b_ref[...]
        for j in range(CONV_WIDTH):
            start = HALO + r0 + j - CONV_PAD
            acc = acc + hs_ref[start:start + rows, :] * w_ref[j:j + 1, :]
        mu = _split_dot(acc, bd_ref[...]) * (CONV_GROUPS / CONV_CH)
        d = acc - mu
        var = _split_dot(d * d, bd_ref[...]) * (CONV_GROUPS / CONV_CH)
        hn = d * lax.rsqrt(var + EPS) * gng_ref[...] + gnb_ref[...]
        y = _dot(_silu(hn).astype(BF16), wpw_ref[...]) + bpw_ref[...]
        o_ref[r0:r0 + rows, :] = y.astype(o_ref.dtype)


def _conv(geom, uc, dw_w, dw_b, gn_g, gn_b, bd, w_pw2, b_pw2):
    tc = geom.tile(256)
    hb = tc // HALO
    nh = geom.t // HALO
    const = lambda shape: pl.BlockSpec(shape, lambda i: (0, 0))
    return pl.pallas_call(
        functools.partial(_conv_kernel, geom, tc),
        grid=(geom.t // tc,),
        in_specs=[pl.BlockSpec((HALO, 2 * CONV_CH), lambda i: (jnp.maximum(i * hb - 1, 0), 0)),
                  pl.BlockSpec((tc, 2 * CONV_CH), lambda i: (i, 0)),
                  pl.BlockSpec((HALO, 2 * CONV_CH), lambda i: (jnp.minimum((i + 1) * hb, nh - 1), 0)),
                  const((32, CONV_CH)), const((1, CONV_CH)), const((1, CONV_CH)), const((1, CONV_CH)),
                  const((CONV_CH, CONV_CH)), const((CONV_CH, CONV_CH)), const((1, CONV_CH))],
        out_specs=pl.BlockSpec((tc, CONV_CH), lambda i: (i, 0)),
        out_shape=jax.ShapeDtypeStruct((geom.t, CONV_CH), BF16),
        scratch_shapes=[pltpu.VMEM((tc + 2 * HALO, CONV_CH), F32)],
        compiler_params=_cparams(("parallel",)),
        name="conformer_conv",
    )(uc, uc, uc, dw_w, dw_b, gn_g, gn_b, bd, w_pw2, b_pw2)


def _rms_rows(x, g):
    return x * lax.rsqrt(jnp.mean(x * x, axis=-1, keepdims=True) + EPS) * g


def _store_slabs(ref, x, base=0):
    rows = x.shape[0]
    for c in range(SLAB):
        ref[pl.ds(base + c, rows, stride=SLAB), :] = x[:, c * LANES:(c + 1) * LANES]


def _load_slabs(ref, rows, base=0):
    return jnp.concatenate([ref[pl.ds(base + c, rows, stride=SLAB), :] for c in range(SLAB)], axis=1)


def _slab_row(ref, r):
    return ref.at[pl.ds(pl.multiple_of(r * SLAB, SLAB), SLAB)]


def _outproj_kernel(x_ref, oa_ref, ob_ref, oc_ref, ga_ref, gb_ref, gc_ref, w_ref, gate_ref, sh_ref, sc_ref,
                    g2_ref, wr_hi_ref, wr_lo_ref, br_ref, x1_ref, h2_ref, lg_ref):
    na = _rms_rows(oa_ref[...].astype(F32), ga_ref[...]).astype(BF16)
    nb = _rms_rows(ob_ref[...].astype(F32), gb_ref[...]).astype(BF16)
    nc = _rms_rows(oc_ref[...].astype(F32), gc_ref[...]).astype(BF16)
    wa = A_Q
    wb = wa + MLA_HEADS * MLA_V
    y = _dot(na, w_ref[0:wa, :]) + _dot(nb, w_ref[wa:wb, :]) + _dot(nc, w_ref[wb:, :])
    x1 = x_ref[...] + gate_ref[...] * y
    x1_ref[...] = x1
    h2 = _rms_rows(x1, g2_ref[...]) * (1.0 + sc_ref[...]) + sh_ref[...]
    hi = h2.astype(BF16)
    lo = (h2 - hi.astype(F32)).astype(BF16)
    _store_slabs(h2_ref, h2)
    lg_ref[...] = _dot(hi, wr_hi_ref[...]) + _dot(lo, wr_hi_ref[...]) + _dot(hi, wr_lo_ref[...]) + br_ref[...]


def _outproj(geom, x, oa, ob, oc, ga, gb, gc, w_out, mods, g2, wr_hi, wr_lo, br):
    tm = geom.tile(256)
    tok = lambda w: pl.BlockSpec((tm, w), lambda i: (i, 0))
    const = lambda shape: pl.BlockSpec(shape, lambda i: (0, 0))
    wb = MLA_HEADS * MLA_V
    return pl.pallas_call(
        _outproj_kernel,
        grid=(geom.t // tm,),
        in_specs=[tok(D_MODEL), tok(A_Q), tok(wb), tok(CONV_CH),
                  const((1, A_Q)), const((1, wb)), const((1, CONV_CH)),
                  const((D_MODEL, D_MODEL)),
                  _mod_spec(geom, tm, 2), _mod_spec(geom, tm, 3), _mod_spec(geom, tm, 4),
                  const((1, D_MODEL)), const((D_MODEL, LANES)), const((D_MODEL, LANES)), const((1, LANES))],
        out_specs=[tok(D_MODEL), pl.BlockSpec((tm * SLAB, LANES), lambda i: (i, 0)), tok(LANES)],
        out_shape=[jax.ShapeDtypeStruct((geom.t, D_MODEL), F32),
                   jax.ShapeDtypeStruct((geom.t * SLAB, LANES), F32),
                   jax.ShapeDtypeStruct((geom.t, LANES), F32)],
        compiler_params=_cparams(("parallel",)),
        name="merge_outproj_norm2",
    )(x, oa, ob, oc, ga, gb, gc, w_out, mods, mods, mods, g2, wr_hi, wr_lo, br)


def _route_kernel(lg_ref, rt_ref, cnt_ref):
    lg = lg_ref[...]
    lane = lax.broadcasted_iota(jnp.int32, lg.shape, 1)
    lanef = lane.astype(F32)
    big = float(2 * LANES)
    is_g = jnp.logical_and(lane >= N_EXPERTS, lane < N_EXPERTS + N_GROUPS)
    gl = jnp.where(is_g, lg, NEG_INF)
    gmax = jnp.max(gl, axis=-1, keepdims=True)
    gsum = jnp.sum(jnp.exp(gl - gmax), axis=-1, keepdims=True)
    g_val = 1.0 / gsum
    g_idx = jnp.min(jnp.where(jnp.logical_and(is_g, gl == gmax), lanef, big), axis=-1, keepdims=True) - N_EXPERTS
    lo = g_idx * EXPERTS_PER_GROUP
    is_e = jnp.logical_and(lanef >= lo, lanef < lo + EXPERTS_PER_GROUP)
    el = jnp.where(is_e, lg, NEG_INF)
    emax = jnp.max(el, axis=-1, keepdims=True)
    ee = jnp.exp(el - emax)
    e_prob = ee / jnp.sum(ee, axis=-1, keepdims=True)
    p1 = jnp.where(is_e, e_prob, -1.0)
    v1 = jnp.max(p1, axis=-1, keepdims=True)
    i1 = jnp.min(jnp.where(p1 == v1, lanef, big), axis=-1, keepdims=True)
    p2 = jnp.where(lanef == i1, -1.0, p1)
    v2 = jnp.max(p2, axis=-1, keepdims=True)
    i2 = jnp.min(jnp.where(p2 == v2, lanef, big), axis=-1, keepdims=True)
    scale = g_val / (v1 + v2)
    rt_ref[...] = (jnp.where(lane == 0, i1, 0.0) + jnp.where(lane == 1, i2, 0.0) +
                   jnp.where(lane == 2, v1 * scale, 0.0) + jnp.where(lane == 3, v2 * scale, 0.0))
    chosen = jnp.logical_or(lanef == i1, lanef == i2)
    cnt = jnp.sum(jnp.where(chosen, 1.0, 0.0), axis=0, keepdims=True)
    cnt_ref[...] = jnp.broadcast_to(cnt, cnt_ref.shape)


def _route(geom, tm, logits):
    spec = pl.BlockSpec((tm, LANES), lambda i: (i, 0))
    n = geom.t // tm
    return pl.pallas_call(
        _route_kernel,
        grid=(n,),
        in_specs=[spec],
        out_specs=[spec, pl.BlockSpec((8, LANES), lambda i: (i, 0))],
        out_shape=[jax.ShapeDtypeStruct((geom.t, LANES), F32), jax.ShapeDtypeStruct((n * 8, LANES), F32)],
        compiler_params=_cparams(("parallel",)),
        name="moe_route",
    )(logits)


def _slots_kernel(rt_ref, start_ref, tri_ref, sl_ref):
    rt = rt_ref[...]
    lane = lax.broadcasted_iota(jnp.int32, rt.shape, 1)
    lanef = lane.astype(F32)
    oh1 = lanef == rt[:, 0:1]
    oh2 = lanef == rt[:, 1:2]
    chosen = jnp.where(jnp.logical_or(oh1, oh2), 1.0, 0.0).astype(BF16)
    first = _dot(tri_ref[...], chosen) + start_ref[...]
    s1 = jnp.sum(jnp.where(oh1, first, 0.0), axis=-1, keepdims=True)
    s2 = jnp.sum(jnp.where(oh2, first, 0.0), axis=-1, keepdims=True)
    sl_ref[...] = jnp.where(lane == 0, s1, 0.0) + jnp.where(lane == 1, s2, 0.0)


def _slots(geom, tm, rt, start, tri):
    spec = pl.BlockSpec((tm, LANES), lambda i: (i, 0))
    return pl.pallas_call(
        _slots_kernel,
        grid=(geom.t // tm,),
        in_specs=[spec, pl.BlockSpec((None, 1, LANES), lambda i: (i, 0, 0)), pl.BlockSpec((tm, tm), lambda i: (0, 0))],
        out_specs=spec,
        out_shape=jax.ShapeDtypeStruct((geom.t, LANES), F32),
        compiler_params=_cparams(("parallel",)),
        name="moe_slots",
    )(rt, start, tri)


def _plan(geom, tm, cnt, n_tiles_max):
    n = geom.t // tm
    cnt_t = cnt.reshape(n, 8, LANES)[:, 0, :]
    incl = jnp.cumsum(cnt_t, axis=0)
    tiles_e = jnp.ceil(incl[-1] / TME)
    end_e = jnp.cumsum(tiles_e)
    start = (incl - cnt_t + ((end_e - tiles_e) * TME)[None, :]).reshape(n, 1, LANES)
    n_used = end_e[-1].astype(jnp.int32)
    tile_idx = jnp.minimum(jnp.arange(n_tiles_max, dtype=jnp.int32), n_used - 1)
    tile_exp = jnp.sum(tile_idx[:, None] >= end_e[None, :N_EXPERTS].astype(jnp.int32), axis=1).astype(jnp.int32)
    return start, tile_exp, tile_idx, n_used.reshape(1)


def _smem_slots(slots_f, tm):
    n = slots_f.shape[0] // tm
    return slots_f[:, :2].astype(jnp.int32).reshape(n, tm, 2).transpose(0, 2, 1).reshape(n, 1, 2 * tm)


def _scatter_kernel(sl_ref, h_ref, xs_in, xs_out, sem):
    del xs_in
    tm = h_ref.shape[0] // SLAB

    def body(rr, carry):
        for j in range(ROW_UNROLL):
            r = rr * ROW_UNROLL + j
            pltpu.make_async_copy(_slab_row(h_ref, r), _slab_row(xs_out, sl_ref[0, r]), sem).start(priority=0)
            pltpu.make_async_copy(_slab_row(h_ref, r), _slab_row(xs_out, sl_ref[0, tm + r]), sem).start(priority=1)
        return carry

    lax.fori_loop(0, tm // ROW_UNROLL, body, 0)
    for _ in range(2):
        pltpu.make_async_copy(h_ref, xs_out.at[pl.ds(0, tm * SLAB)], sem).wait()


def _scatter(geom, tm, slots_smem, h2, xs):
    return pl.pallas_call(
        _scatter_kernel,
        grid=(geom.t // tm,),
        in_specs=[pl.BlockSpec((None, 1, 2 * tm), lambda i: (i, 0, 0), memory_space=pltpu.SMEM),
                  pl.BlockSpec((tm * SLAB, LANES), lambda i: (i, 0)),
                  pl.BlockSpec(memory_space=pl.ANY)],
        out_specs=pl.BlockSpec(memory_space=pl.ANY),
        out_shape=jax.ShapeDtypeStruct(xs.shape, xs.dtype),
        scratch_shapes=[pltpu.SemaphoreType.DMA(())],
        input_output_aliases={2: 0},
        compiler_params=_cparams(("arbitrary",)),
        name="moe_scatter",
    )(slots_smem, h2, xs)


def _experts_kernel(te_ref, ti_ref, nu_ref, xs_ref, wg_ref, wu_ref, wd_ref, ys_ref):
    del te_ref, ti_ref

    @pl.when(pl.program_id(0) < nu_ref[0])
    def _():
        x = _load_slabs(xs_ref, TME).astype(BF16)
        act = (_silu(_dot(x, wg_ref[...])) * _dot(x, wu_ref[...])).astype(BF16)
        _store_slabs(ys_ref, _dot(act, wd_ref[...]))


def _experts(xs, tile_exp, tile_idx, n_used, wg, wu, wd):
    n_tiles_max = xs.shape[0] // (TME * SLAB)
    slab = pl.BlockSpec((TME * SLAB, LANES), lambda n, te, ti, nu: (ti[n], 0))
    wspec = lambda a, b: pl.BlockSpec((None, a, b), lambda n, te, ti, nu: (te[n], 0, 0))
    return pl.pallas_call(
        _experts_kernel,
        grid_spec=pltpu.PrefetchScalarGridSpec(
            num_scalar_prefetch=3,
            grid=(n_tiles_max,),
            in_specs=[slab, wspec(D_MODEL, D_EXPERT), wspec(D_MODEL, D_EXPERT), wspec(D_EXPERT, D_MODEL)],
            out_specs=slab),
        out_shape=jax.ShapeDtypeStruct(xs.shape, F32),
        compiler_params=_cparams(("arbitrary",)),
        name="moe_experts",
    )(tile_exp, tile_idx, n_used, xs, wg, wu, wd)


def _combine_kernel(sl_cur, sl_nxt, rt_ref, x1_ref, gate_ref, ys_hbm, o_ref, buf1, buf2, sem):
    i = pl.program_id(0)
    tm = x1_ref.shape[0]

    def issue(sl_ref, slot):
        def body(rr, carry):
            for j in range(ROW_UNROLL):
                r = rr * ROW_UNROLL + j
                dst = slot * tm + r
                pltpu.make_async_copy(_slab_row(ys_hbm, sl_ref[0, r]), _slab_row(buf1, dst),
                                      sem.at[slot]).start(priority=0)
                pltpu.make_async_copy(_slab_row(ys_hbm, sl_ref[0, tm + r]), _slab_row(buf2, dst),
                                      sem.at[slot]).start(priority=1)
            return carry
        lax.fori_loop(0, tm // ROW_UNROLL, body, 0)

    @pl.when(i == 0)
    def _():
        issue(sl_cur, 0)

    @pl.when(i + 1 < pl.num_programs(0))
    def _():
        issue(sl_nxt, (i + 1) % 2)

    slot = i % 2
    base = pl.multiple_of(slot * (tm * SLAB), tm * SLAB)
    for buf in (buf1, buf2):
        pltpu.make_async_copy(ys_hbm.at[pl.ds(0, tm * SLAB)], buf.at[pl.ds(base, tm * SLAB)], sem.at[slot]).wait()
    rt = rt_ref[...]
    y = rt[:, 2:3] * _load_slabs(buf1, tm, base) + rt[:, 3:4] * _load_slabs(buf2, tm, base)
    o_ref[...] = x1_ref[...] + gate_ref[...] * y


def _combine(geom, tm, slots_smem, rt, x1, mods, ys):
    n = geom.t // tm
    tok = lambda w: pl.BlockSpec((tm, w), lambda i: (i, 0))
    smem = lambda f: pl.BlockSpec((None, 1, 2 * tm), f, memory_space=pltpu.SMEM)
    return pl.pallas_call(
        _combine_kernel,
        grid=(n,),
        in_specs=[smem(lambda i: (i, 0, 0)), smem(lambda i: (jnp.minimum(i + 1, n - 1), 0, 0)),
                  tok(LANES), tok(D_MODEL), _mod_spec(geom, tm, 5),
                  pl.BlockSpec(memory_space=pl.ANY)],
        out_specs=tok(D_MODEL),
        out_shape=jax.ShapeDtypeStruct((geom.t, D_MODEL), F32),
        scratch_shapes=[pltpu.VMEM((2 * tm * SLAB, LANES), F32), pltpu.VMEM((2 * tm * SLAB, LANES), F32),
                        pltpu.SemaphoreType.DMA((2,))],
        compiler_params=_cparams(("arbitrary",)),
        name="moe_combine",
    )(slots_smem, slots_smem, rt, x1, mods, ys)


def _block_diag_ones(n, blk):
    idx = np.arange(n) // blk
    return jnp.asarray((idx[:, None] == idx[None, :]).astype(np.float32), dtype=BF16)


def _rope_tables(smax):
    half = MLA_ROPE // 2
    inv_freq = 1.0 / (ROPE_BASE ** (jnp.arange(0, MLA_ROPE, 2, dtype=F32) / MLA_ROPE))
    ang = jnp.arange(smax, dtype=F32)[:, None] * inv_freq[None, :]
    cos, sin = jnp.cos(ang), jnp.sin(ang)
    ones = jnp.ones((smax, MLA_NOPE), F32)
    zeros = lambda w: jnp.zeros((smax, w), F32)
    pad = LANES - MLA_QK
    c = jnp.concatenate([ones, cos, cos, zeros(pad)], axis=1)
    sa = jnp.concatenate([zeros(MLA_NOPE), -sin, zeros(half), zeros(pad)], axis=1)
    sb = jnp.concatenate([zeros(MLA_NOPE), zeros(half), sin, zeros(pad)], axis=1)
    return c, sa, sb


def _pad_cols(w, n):
    return jnp.pad(w, ((0, 0), (0, n - w.shape[1])))


def _layer_weights(l, w_in, mla_w_q_up, mla_q_gain, mla_k_gain, swa_q_gain, swa_k_gain, swa_sink,
                   moe_w_group, moe_b_group, moe_w_expert, moe_b_expert):
    wi = w_in[l]
    kr = jnp.pad(wi[:, A_IN + MLA_Q_RANK + MLA_KV_RANK:A_IN + B_IN], ((0, 0), (MLA_NOPE, LANES - MLA_QK)))
    w_in_p = jnp.concatenate([wi[:, :A_IN], wi[:, A_IN:A_IN + MLA_Q_RANK + MLA_KV_RANK], kr, wi[:, A_IN + B_IN:]],
                             axis=1).astype(BF16)
    wq = mla_w_q_up[l].reshape(MLA_Q_RANK, MLA_HEADS, MLA_QK)
    wq_p = jnp.pad(wq, ((0, 0), (0, 0), (0, LANES - MLA_QK))).reshape(MLA_Q_RANK, MLA_HEADS * LANES).astype(BF16)
    gq_p = _pad_cols(mla_q_gain[l][None, :], LANES)
    gk_p = _pad_cols(mla_k_gain[l][None, :], LANES)
    gq_a = jnp.tile(swa_q_gain[l], SWA_HEADS)[None, :]
    gk_a = jnp.tile(swa_k_gain[l], SWA_KV_HEADS)[None, :]
    sink = jnp.repeat(swa_sink[l] * LOG2E, BLOCK).reshape(SWA_KV_HEADS, 1, SWA_GROUP * BLOCK)
    wr = _pad_cols(jnp.concatenate([moe_w_expert[l], moe_w_group[l]], axis=1), LANES)
    wr_hi = wr.astype(BF16)
    wr_lo = (wr - wr_hi.astype(F32)).astype(BF16)
    br = _pad_cols(jnp.concatenate([moe_b_expert[l], moe_b_group[l]])[None, :], LANES)
    return w_in_p, wq_p, gq_p, gk_p, gq_a, gk_a, sink, wr_hi, wr_lo, br


def kernel(x_prompt, x_sample, c_prompt, c_sample, w_ada, b_ada, norm1_g, norm2_g, w_in, swa_q_gain, swa_k_gain, swa_sink, mla_q_norm_g, mla_w_q_up, mla_kv_norm_g, mla_w_kv_up, mla_q_gain, mla_k_gain, conv_dw_w, conv_dw_b, conv_gn_g, conv_gn_b, conv_w_pw2, conv_b_pw2, out_norm_a, out_norm_b, out_norm_c, w_out, moe_w_group, moe_b_group, moe_w_expert, moe_b_expert, moe_w_gate, moe_w_up, moe_w_down):
    bp, sp, d = x_prompt.shape
    bs, ss, _ = x_sample.shape
    assert d == D_MODEL
    geom = _Geom(bp, sp, bs, ss)
    depth = w_ada.shape[0]

    x = jnp.concatenate([x_prompt.reshape(bp * sp, d), x_sample.reshape(bs * ss, d)], axis=0)
    c = jnp.concatenate([c_prompt, c_sample], axis=0)
    rows = -(-geom.nb // 8) * 8
    c_pad = jnp.pad(c, ((0, rows - geom.nb), (0, 0)))
    mods_all = _modulation(c_pad, w_ada, b_ada)

    rope_c, rope_sa, rope_sb = _rope_tables(max(sp, ss))
    bdq = _block_diag_ones(A_Q, HEAD_DIM)
    bdk = _block_diag_ones(A_KV, HEAD_DIM)
    bdc = _block_diag_ones(CONV_CH, CONV_CH // CONV_GROUPS)
    swa_bias = _swa_bias()
    row = lambda v: v[None, :]
    tm_moe = geom.tile(256)
    n_tiles_max = 2 * geom.t // TME + N_EXPERTS
    tri = jnp.asarray(np.tril(np.ones((tm_moe, tm_moe), np.float32), -1), dtype=BF16)
    xs = jnp.zeros((n_tiles_max * TME * SLAB, LANES), F32)

    for l in range(depth):
        (w_in_p, wq_p, gq_p, gk_p, gq_a, gk_a, sink, wr_hi, wr_lo, br) = _layer_weights(
            l, w_in, mla_w_q_up, mla_q_gain, mla_k_gain, swa_q_gain, swa_k_gain, swa_sink,
            moe_w_group, moe_b_group, moe_w_expert, moe_b_expert)
        mods = mods_all[l, :geom.nb].reshape(geom.nb * N_MOD, 1, D_MODEL)

        q_a, k_a, v_a, cq, ckv, kr, uc = _inproj(geom, x, mods, row(norm1_g[l]), w_in_p)
        out_a = _swa(geom, q_a, k_a, v_a, gq_a, gk_a, bdq, bdk, sink, swa_bias)
        tk = geom.tile(512)
        q_b, k_b, vt_b = _mla_prep(geom, tk, cq, ckv, kr, row(mla_q_norm_g[l]), row(mla_kv_norm_g[l]), wq_p,
                                   mla_w_kv_up[l].astype(BF16), gq_p, gk_p, rope_c, rope_sa, rope_sb)
        out_b = _mla_attn(geom, tk, q_b, k_b, vt_b)
        dw_w = jnp.pad(conv_dw_w[l].reshape(CONV_WIDTH, CONV_CH), ((0, 32 - CONV_WIDTH), (0, 0)))
        out_c = _conv(geom, uc, dw_w, row(conv_dw_b[l]), row(conv_gn_g[l]), row(conv_gn_b[l]), bdc,
                      conv_w_pw2[l].astype(BF16), row(conv_b_pw2[l]))
        x1, h2, logits = _outproj(geom, x, out_a, out_b, out_c, row(out_norm_a[l]), row(out_norm_b[l]),
                                  row(out_norm_c[l]), w_out[l].astype(BF16), mods, row(norm2_g[l]),
                                  wr_hi, wr_lo, br)
        rt, cnt = _route(geom, tm_moe, logits)
        start, tile_exp, tile_idx, n_used = _plan(geom, tm_moe, cnt, n_tiles_max)
        slots = _smem_slots(_slots(geom, tm_moe, rt, start, tri), tm_moe)
        xs = _scatter(geom, tm_moe, slots, h2, xs)
        ys = _experts(xs, tile_exp, tile_idx, n_used, moe_w_gate[l].astype(BF16), moe_w_up[l].astype(BF16),
                      moe_w_down[l].astype(BF16))
        x = _combine(geom, tm_moe, slots, rt, x1, mods, ys)

    y_prompt = x[:geom.tp].reshape(bp, sp, d)
    y_sample = x[geom.tp:].reshape(bs, ss, d)
    return (y_prompt, y_sample)
```

```python
import functools

import numpy as np
import jax
import jax.numpy as jnp
from jax import lax
from jax.experimental import pallas as pl
from jax.experimental.pallas import tpu as pltpu

F32 = jnp.float32
BF16 = jnp.bfloat16

EPS = 1e-6
NEG_INF = -1e30
LOG2E = 1.4426950408889634

D_MODEL = 1024
HEAD_DIM = 64
SWA_HEADS = 6
SWA_KV_HEADS = 2
SWA_GROUP = SWA_HEADS // SWA_KV_HEADS
WINDOW = 128
BLOCK = WINDOW
MLA_HEADS = 6
MLA_Q_RANK = 256
MLA_KV_RANK = 128
MLA_NOPE = 64
MLA_ROPE = 32
MLA_QK = MLA_NOPE + MLA_ROPE
MLA_V = 64
MLA_VT_ROWS = MLA_V + 16
ROPE_BASE = 10000.0
CONV_CH = 256
CONV_GROUPS = 4
CONV_WIDTH = 31
CONV_PAD = (CONV_WIDTH - 1) // 2
A_Q = SWA_HEADS * HEAD_DIM
A_KV = SWA_KV_HEADS * HEAD_DIM
A_IN = A_Q + 2 * A_KV
B_IN = MLA_Q_RANK + MLA_KV_RANK + MLA_ROPE
N_GROUPS = 4
EXPERTS_PER_GROUP = 8
N_EXPERTS = N_GROUPS * EXPERTS_PER_GROUP
D_EXPERT = 256
N_MOD = 6

LANES = 128
SUBLANES = 8
SLAB = D_MODEL // 128
TME = 256
ROW_UNROLL = 8
HALO = 16
VMEM_LIMIT = 48 * 1024 * 1024


class _Geom:
    def __init__(self, bp, sp, bs, ss):
        self.bp, self.sp, self.bs, self.ss = bp, sp, bs, ss
        self.tp = bp * sp
        self.t = bp * sp + bs * ss
        self.nb = bp + bs

    def tile(self, target):
        t = target
        while self.sp % t or self.ss % t:
            t //= 2
        return t

    def batch(self, i, tm):
        npt = self.tp // tm
        return jnp.where(i < npt, i // (self.sp // tm), self.bp + (i - npt) // (self.ss // tm))

    def pos(self, i, tm):
        npt = self.tp // tm
        return jnp.where(i < npt, i % (self.sp // tm), (i - npt) % (self.ss // tm))

    def is_last(self, i, tm):
        npt = self.tp // tm
        return jnp.where(i < npt, i % (self.sp // tm) == self.sp // tm - 1,
                         (i - npt) % (self.ss // tm) == self.ss // tm - 1)


def _cparams(sem):
    return pltpu.CompilerParams(dimension_semantics=sem, vmem_limit_bytes=VMEM_LIMIT)


def _silu(x):
    return x * jax.nn.sigmoid(x)


def _dot(a, b):
    return jnp.dot(a, b, preferred_element_type=F32)


def _dot_nt(a, b):
    return lax.dot_general(a, b, (((1,), (1,)), ((), ())), preferred_element_type=F32)


def _split_dot(x, w):
    hi = x.astype(BF16)
    lo = (x - hi.astype(F32)).astype(BF16)
    return _dot(hi, w) + _dot(lo, w)


def _mod_kernel(c_ref, w_ref, b_ref, o_ref):
    c = c_ref[...]
    o_ref[...] = _dot(_silu(c).astype(BF16), w_ref[...].astype(BF16)) + b_ref[...]


def _modulation(c_pad, w_ada, b_ada):
    L, d, n = w_ada.shape
    tn = 768
    rows = c_pad.shape[0]
    return pl.pallas_call(
        _mod_kernel,
        grid=(L, n // tn),
        in_specs=[pl.BlockSpec((rows, d), lambda l, j: (0, 0)),
                  pl.BlockSpec((None, d, tn), lambda l, j: (l, 0, j)),
                  pl.BlockSpec((None, 1, tn), lambda l, j: (l, 0, j))],
        out_specs=pl.BlockSpec((None, rows, tn), lambda l, j: (l, 0, j)),
        out_shape=jax.ShapeDtypeStruct((L, rows, n), F32),
        compiler_params=_cparams(("arbitrary", "arbitrary")),
        name="adaln_mod",
    )(c_pad, w_ada, b_ada.reshape(L, 1, n))


_IN_SEGS = (("q", A_Q), ("k", A_KV), ("v", A_KV), ("cq", MLA_Q_RANK), ("ckv", MLA_KV_RANK), ("kr", LANES),
            ("uc", 2 * CONV_CH))


def _inproj_kernel(x_ref, sh_ref, sc_ref, g_ref, w_ref, *out_refs):
    x = x_ref[...]
    ms = jnp.mean(x * x, axis=-1, keepdims=True)
    h = x * lax.rsqrt(ms + EPS) * g_ref[...]
    h = (h * (1.0 + sc_ref[...]) + sh_ref[...]).astype(BF16)
    off = 0
    for (_, width), o_ref in zip(_IN_SEGS, out_refs):
        o_ref[...] = _dot(h, w_ref[:, off:off + width]).astype(o_ref.dtype)
        off += width


def _mod_spec(geom, tm, k):
    return pl.BlockSpec((None, 1, D_MODEL), lambda i: (geom.batch(i, tm) * N_MOD + k, 0, 0))


def _inproj(geom, x, mods, g1, w_in_p):
    tm = geom.tile(512)
    nw = w_in_p.shape[1]
    return pl.pallas_call(
        _inproj_kernel,
        grid=(geom.t // tm,),
        in_specs=[pl.BlockSpec((tm, D_MODEL), lambda i: (i, 0)),
                  _mod_spec(geom, tm, 0), _mod_spec(geom, tm, 1),
                  pl.BlockSpec((1, D_MODEL), lambda i: (0, 0)),
                  pl.BlockSpec((D_MODEL, nw), lambda i: (0, 0))],
        out_specs=[pl.BlockSpec((tm, w), lambda i: (i, 0)) for _, w in _IN_SEGS],
        out_shape=[jax.ShapeDtypeStruct((geom.t, w), BF16) for _, w in _IN_SEGS],
        compiler_params=_cparams(("parallel",)),
        name="norm1_inproj",
    )(x, mods, mods, g1, w_in_p)


def _swa_kernel(geom, nb, q_ref, kp_ref, kc_ref, kn_ref, vp_ref, vc_ref, vn_ref, gq_ref, gk_ref,
                bdq_ref, bdk_ref, sink_ref, bias_ref, o_ref):
    i = pl.program_id(0)
    first = (geom.pos(i, nb * BLOCK) == 0).astype(jnp.int32)
    last = geom.is_last(i, nb * BLOCK).astype(jnp.int32)

    q = q_ref[...].astype(F32)
    msq = _dot((q * q).astype(BF16), bdq_ref[...]) * (1.0 / HEAD_DIM)
    q_t = (q * lax.rsqrt(msq + EPS) * gq_ref[...] * (HEAD_DIM ** -0.5 * LOG2E)).T.astype(BF16)

    k_all = jnp.concatenate([kp_ref[...], kc_ref[...], kn_ref[...]], axis=0).astype(F32)
    msk = _dot((k_all * k_all).astype(BF16), bdk_ref[...]) * (1.0 / HEAD_DIM)
    k_all = (k_all * lax.rsqrt(msk + EPS) * gk_ref[...]).astype(BF16)
    v_all = jnp.concatenate([vp_ref[...], vc_ref[...], vn_ref[...]], axis=0).astype(F32)
    ones_rows = 16
    v_t = jnp.concatenate([v_all.T, jnp.ones((ones_rows, v_all.shape[0]), F32)], axis=0).astype(BF16)
    zeros = jnp.zeros((HEAD_DIM, SWA_GROUP * BLOCK), BF16)

    for b in range(nb):
        cls = (first if b == 0 else 0) + (2 * last if b == nb - 1 else 0)
        kb = k_all[b * BLOCK:(b + 3) * BLOCK, :]
        vb = v_t[:, b * BLOCK:(b + 3) * BLOCK]
        outs = []
        for g in range(SWA_KV_HEADS):
            heads = range(g * SWA_GROUP, (g + 1) * SWA_GROUP)
            qg = jnp.concatenate([q_t[h * HEAD_DIM:(h + 1) * HEAD_DIM, b * BLOCK:(b + 1) * BLOCK] for h in heads],
                                 axis=1)
            q_pad = jnp.concatenate([qg if j == g else zeros for j in range(SWA_KV_HEADS)], axis=0)
            s = _dot(kb, q_pad) + bias_ref[cls, g]
            sink = sink_ref[g]
            m = jnp.maximum(jnp.max(s, axis=0, keepdims=True), sink)
            p = jnp.exp2(s - m).astype(BF16)
            o = _dot(vb, p)
            denom = o[A_KV:A_KV + 1, :] + jnp.exp2(sink - m)
            og = o[g * HEAD_DIM:(g + 1) * HEAD_DIM, :] / denom
            outs += [og[:, j * BLOCK:(j + 1) * BLOCK] for j in range(SWA_GROUP)]
        o_ref[b * BLOCK:(b + 1) * BLOCK, :] = jnp.concatenate(outs, axis=0).T.astype(o_ref.dtype)


def _swa_bias():
    k = np.arange(3 * BLOCK)[:, None]
    q = np.arange(BLOCK)[None, :]
    rel = np.abs(k - BLOCK - q)
    out = np.zeros((4, SWA_KV_HEADS, 3 * BLOCK, SWA_GROUP * BLOCK), np.float32)
    for c in range(4):
        k_lo = BLOCK if c & 1 else 0
        k_hi = 2 * BLOCK if c & 2 else 3 * BLOCK
        valid = (rel <= WINDOW) & (k >= k_lo) & (k < k_hi)
        for g in range(SWA_KV_HEADS):
            for j in range(SWA_GROUP):
                slope = 2.0 ** (-8.0 * (g * SWA_GROUP + j + 1) / SWA_HEADS)
                out[c, g, :, j * BLOCK:(j + 1) * BLOCK] = np.where(valid, -slope * LOG2E * rel, NEG_INF)
    return jnp.asarray(out)


def _swa(geom, q, k, v, gq, gk, bdq, bdk, sink, bias):
    ts = geom.tile(512)
    nb = ts // BLOCK
    nblk = geom.t // BLOCK
    prev = lambda i: (jnp.maximum(i * nb - 1, 0), 0)
    cur = lambda i: (i, 0)
    nxt = lambda i: (jnp.minimum((i + 1) * nb, nblk - 1), 0)
    halo = lambda f: pl.BlockSpec((BLOCK, A_KV), f)
    const = lambda shape: pl.BlockSpec(shape, lambda i: (0,) * len(shape))
    return pl.pallas_call(
        functools.partial(_swa_kernel, geom, nb),
        grid=(geom.t // ts,),
        in_specs=[pl.BlockSpec((ts, A_Q), cur),
                  halo(prev), pl.BlockSpec((ts, A_KV), cur), halo(nxt),
                  halo(prev), pl.BlockSpec((ts, A_KV), cur), halo(nxt),
                  const((1, A_Q)), const((1, A_KV)), const((A_Q, A_Q)), const((A_KV, A_KV)),
                  const((SWA_KV_HEADS, 1, SWA_GROUP * BLOCK)), const(bias.shape)],
        out_specs=pl.BlockSpec((ts, A_Q), cur),
        out_shape=jax.ShapeDtypeStruct((geom.t, A_Q), BF16),
        compiler_params=_cparams(("parallel",)),
        name="swa_attention",
    )(q, k, k, k, v, v, v, gq, gk, bdq, bdk, sink, bias)


def _swap_rope_halves(a):
    lo, hi = MLA_NOPE, MLA_NOPE + MLA_ROPE // 2
    z = jnp.zeros_like(a)
    return jnp.concatenate([z[..., :lo], a[..., hi:MLA_QK], a[..., lo:hi], z[..., MLA_QK:]], axis=-1)


def _mla_prep_kernel(cq_ref, ckv_ref, kr_ref, gqn_ref, gkvn_ref, wq_ref, wkv_ref, wvt_ref, perm_ref, ones_ref,
                     gq_ref, gqs_ref, gk_ref, gks_ref, c_ref, s_ref, q_out, k_out, vt_out):
    cq = cq_ref[...].astype(F32)
    qn = (cq * lax.rsqrt(jnp.mean(cq * cq, axis=-1, keepdims=True) + EPS) * gqn_ref[...]).astype(BF16)
    ckv = ckv_ref[...].astype(F32)
    kvn = (ckv * lax.rsqrt(jnp.mean(ckv * ckv, axis=-1, keepdims=True) + EPS) * gkvn_ref[...]).astype(BF16)
    kr_b = kr_ref[...]
    kr = kr_b.astype(F32)
    lane = lax.broadcasted_iota(jnp.int32, (1, LANES), 1)
    ones = jnp.ones((MLA_VT_ROWS - MLA_V, cq.shape[0]), BF16)
    c = c_ref[...]
    sn = s_ref[...]
    q_c = c * (gq_ref[...] * (MLA_QK ** -0.5 * LOG2E))
    q_s = sn * (gqs_ref[...] * (MLA_QK ** -0.5 * LOG2E))
    k_c = c * gk_ref[...]
    k_rot = _dot(kr_b, perm_ref[...]) * (sn * gks_ref[...])
    inv_d = 1.0 / MLA_QK
    for hp in range(MLA_HEADS // 2):
        kv2 = _dot(kvn, wkv_ref[:, hp * 2 * LANES:(hp + 1) * 2 * LANES])
        for hh in range(2):
            h = 2 * hp + hh
            xq = _dot(qn, wq_ref[:, h * 2 * LANES:(h + 1) * 2 * LANES])
            x, xs = xq[:, :LANES], xq[:, LANES:]
            xk = jnp.where(lane < MLA_NOPE, kv2[:, hh * LANES:(hh + 1) * LANES], 0.0) + kr
            ss = _dot(jnp.concatenate([x * x, xk * xk], axis=1).astype(BF16), ones_ref[...])
            rq = lax.rsqrt(ss[:, :LANES] * inv_d + EPS)
            rk = lax.rsqrt(ss[:, LANES:] * inv_d + EPS)
            q_out[h] = (rq * (x * q_c + xs * q_s)).astype(BF16)
            k_out[h] = (rk * (xk * k_c + k_rot)).astype(BF16)
            vt_out[h, 0] = jnp.concatenate([_dot_nt(wvt_ref[h], kvn).astype(BF16), ones], axis=0)


def _mla_prep(geom, tk, cq, ckv, kr, gqn, gkvn, wq2, wkv, wvt, perm, ones2, gq_p, gq_s, gk_p, gk_s, rope_c, rope_s):
    tm = tk
    tok = lambda w: pl.BlockSpec((tm, w), lambda i: (i, 0))
    const = lambda shape: pl.BlockSpec(shape, lambda i: (0,) * len(shape))
    rope = pl.BlockSpec((tm, LANES), lambda i: (geom.pos(i, tm), 0))
    hm = pl.BlockSpec((MLA_HEADS, tm, LANES), lambda i: (0, i, 0))
    hm_t = pl.BlockSpec((MLA_HEADS, 1, MLA_VT_ROWS, tm), lambda i: (0, i, 0, 0))
    return pl.pallas_call(
        _mla_prep_kernel,
        grid=(geom.t // tm,),
        in_specs=[tok(MLA_Q_RANK), tok(MLA_KV_RANK), tok(LANES),
                  const((1, MLA_Q_RANK)), const((1, MLA_KV_RANK)),
                  const(wq2.shape), const(wkv.shape), const(wvt.shape), const(perm.shape), const(ones2.shape),
                  const((1, LANES)), const((1, LANES)), const((1, LANES)), const((1, LANES)), rope, rope],
        out_specs=[hm, hm, hm_t],
        out_shape=[jax.ShapeDtypeStruct((MLA_HEADS, geom.t, LANES), BF16),
                   jax.ShapeDtypeStruct((MLA_HEADS, geom.t, LANES), BF16),
                   jax.ShapeDtypeStruct((MLA_HEADS, geom.t // tm, MLA_VT_ROWS, tm), BF16)],
        compiler_params=_cparams(("parallel",)),
        name="mla_prep",
    )(cq, ckv, kr, gqn, gkvn, wq2, wkv, wvt, perm, ones2, gq_p, gq_s, gk_p, gk_s, rope_c, rope_s)


def _mla_attn_kernel(nk, q_ref, k_ref, vt_ref, o_ref, s_ref, mc_ref, p_ref, al_ref, acc_ref):
    tk = vt_ref.shape[3]
    tq = q_ref.shape[1]

    def stage_a(chunk, slot):
        off = chunk * tk
        if not isinstance(off, int):
            off = pl.multiple_of(off, tk)
        for hh in range(2):
            s = _dot_nt(k_ref[hh, pl.ds(off, tk), :], q_ref[hh])
            s_ref[slot, hh] = s
            mc_ref[slot, hh] = jnp.max(s, axis=0, keepdims=True)

    def stage_b(slot, m):
        m_out = []
        for hh in range(2):
            m_new = jnp.maximum(m[hh], mc_ref[slot, hh])
            al_ref[slot, hh] = jnp.exp2(m[hh] - m_new)
            p_ref[slot, hh] = jnp.exp2(s_ref[slot, hh] - m_new).astype(BF16)
            m_out.append(m_new)
        return tuple(m_out)

    def stage_c(chunk, slot):
        for hh in range(2):
            acc_ref[hh] = acc_ref[hh] * al_ref[slot, hh] + _dot(vt_ref[hh, chunk], p_ref[slot, hh])

    stage_a(0, 0)
    p_ref[1] = jnp.zeros(p_ref.shape[1:], BF16)
    al_ref[1] = jnp.ones(al_ref.shape[1:], F32)
    acc_ref[...] = jnp.zeros(acc_ref.shape, F32)
    m = (jnp.full((1, tq), NEG_INF, F32),) * 2

    def body(jj, m):
        j = 2 * jj
        stage_a(j + 1, 1)
        m = stage_b(0, m)
        stage_c(jnp.maximum(j - 1, 0), 1)
        stage_a(j + 2, 0)
        m = stage_b(1, m)
        stage_c(j, 0)
        return m

    m = lax.fori_loop(0, nk // 2 - 1, body, m)
    stage_a(nk - 1, 1)
    m = stage_b(0, m)
    stage_c(max(nk - 3, 0), 1)
    m = stage_b(1, m)
    stage_c(nk - 2, 0)
    stage_c(nk - 1, 1)
    o_t = jnp.concatenate([acc_ref[hh][:MLA_V, :] / acc_ref[hh][MLA_V:MLA_V + 1, :] for hh in range(2)], axis=0)
    o_ref[...] = o_t.T.astype(o_ref.dtype)


def _mla_attn_group(geom, tk, q, k, vt, out_prev, nseq, s, tok0):
    tq = min(1024, s)
    nq = s // tq
    nk = s // tk
    assert nk % 2 == 0
    qb0 = tok0 // tq
    sb0 = tok0 // s
    args = [q, k, vt]
    in_specs = [pl.BlockSpec((2, tq, LANES), lambda b, hp, i: (hp, qb0 + b * nq + i, 0)),
                pl.BlockSpec((2, s, LANES), lambda b, hp, i: (hp, sb0 + b, 0)),
                pl.BlockSpec((2, nk, MLA_VT_ROWS, tk), lambda b, hp, i: (hp, sb0 + b, 0, 0))]
    aliases = {}
    kern = functools.partial(_mla_attn_kernel, nk)
    if out_prev is not None:
        args.append(out_prev)
        in_specs.append(pl.BlockSpec(memory_space=pl.ANY))
        aliases = {3: 0}
        kern = lambda q_ref, k_ref, vt_ref, prev_ref, *rest, _k=kern: _k(q_ref, k_ref, vt_ref, *rest)
    return pl.pallas_call(
        kern,
        grid=(nseq, MLA_HEADS // 2, nq),
        in_specs=in_specs,
        out_specs=pl.BlockSpec((tq, LANES), lambda b, hp, i: (qb0 + b * nq + i, hp)),
        out_shape=jax.ShapeDtypeStruct((geom.t, MLA_HEADS * MLA_V), BF16),
        scratch_shapes=[pltpu.VMEM((2, 2, tk, tq), F32), pltpu.VMEM((2, 2, 1, tq), F32),
                        pltpu.VMEM((2, 2, tk, tq), BF16), pltpu.VMEM((2, 2, 1, tq), F32),
                        pltpu.VMEM((2, MLA_VT_ROWS, tq), F32)],
        input_output_aliases=aliases,
        compiler_params=_cparams(("parallel", "parallel", "arbitrary")),
        name="mla_attention",
    )(*args)


def _mla_attn(geom, tk, q, k, vt):
    assert geom.tp % geom.ss == 0
    out = _mla_attn_group(geom, tk, q, k, vt, None, geom.bp, geom.sp, 0)
    return _mla_attn_group(geom, tk, q, k, vt, out, geom.bs, geom.ss, geom.tp)


def _conv_kernel(geom, tc, up_ref, uc_ref, un_ref, w_ref, b_ref, gng_ref, gnb_ref, bd_ref, wpw_ref, bpw_ref,
                 o_ref, hs_ref):
    i = pl.program_id(0)
    first = geom.pos(i, tc) == 0
    last = geom.is_last(i, tc)

    def glu(u):
        u = u.astype(F32)
        return u[:, :CONV_CH] * jax.nn.sigmoid(u[:, CONV_CH:])

    hs_ref[0, 0:HALO, :] = jnp.where(first, 0.0, glu(up_ref[...]))
    hs_ref[0, HALO:HALO + tc, :] = glu(uc_ref[...])
    hs_ref[0, HALO + tc:HALO + tc + HALO, :] = jnp.where(last, 0.0, glu(un_ref[...]))
    span = tc + 2 * HALO - SUBLANES
    for sft in range(1, SUBLANES):
        hs_ref[sft, 0:span, :] = hs_ref[0, sft:sft + span, :]

    rows = 64 if tc % 64 == 0 else tc
    for r0 in range(0, tc, rows):
        acc = jnp.zeros((rows, CONV_CH), F32) + b_ref[...]
        for j in range(CONV_WIDTH):
            start = HALO + r0 + j - CONV_PAD
            sft = start % SUBLANES
            acc = acc + hs_ref[sft, start - sft:start - sft + rows, :] * w_ref[j:j + 1, :]
        mu = _split_dot(acc, bd_ref[...]) * (CONV_GROUPS / CONV_CH)
        d = acc - mu
        var = _split_dot(d * d, bd_ref[...]) * (CONV_GROUPS / CONV_CH)
        hn = d * lax.rsqrt(var + EPS) * gng_ref[...] + gnb_ref[...]
        y = _dot(_silu(hn).astype(BF16), wpw_ref[...]) + bpw_ref[...]
        o_ref[r0:r0 + rows, :] = y.astype(o_ref.dtype)


def _conv(geom, uc, dw_w, dw_b, gn_g, gn_b, bd, w_pw2, b_pw2):
    tc = geom.tile(256)
    hb = tc // HALO
    nh = geom.t // HALO
    const = lambda shape: pl.BlockSpec(shape, lambda i: (0, 0))
    return pl.pallas_call(
        functools.partial(_conv_kernel, geom, tc),
        grid=(geom.t // tc,),
        in_specs=[pl.BlockSpec((HALO, 2 * CONV_CH), lambda i: (jnp.maximum(i * hb - 1, 0), 0)),
                  pl.BlockSpec((tc, 2 * CONV_CH), lambda i: (i, 0)),
                  pl.BlockSpec((HALO, 2 * CONV_CH), lambda i: (jnp.minimum((i + 1) * hb, nh - 1), 0)),
                  const((32, CONV_CH)), const((1, CONV_CH)), const((1, CONV_CH)), const((1, CONV_CH)),
                  const((CONV_CH, CONV_CH)), const((CONV_CH, CONV_CH)), const((1, CONV_CH))],
        out_specs=pl.BlockSpec((tc, CONV_CH), lambda i: (i, 0)),
        out_shape=jax.ShapeDtypeStruct((geom.t, CONV_CH), BF16),
        scratch_shapes=[pltpu.VMEM((SUBLANES, tc + 2 * HALO, CONV_CH), F32)],
        compiler_params=_cparams(("parallel",)),
        name="conformer_conv",
    )(uc, uc, uc, dw_w, dw_b, gn_g, gn_b, bd, w_pw2, b_pw2)


def _rms_rows(x, g):
    return x * lax.rsqrt(jnp.mean(x * x, axis=-1, keepdims=True) + EPS) * g


def _store_slabs(ref, x, base=0):
    rows = x.shape[0]
    for c in range(SLAB):
        ref[pl.ds(base + c, rows, stride=SLAB), :] = x[:, c * LANES:(c + 1) * LANES]


def _load_slabs(ref, rows, base=0):
    return jnp.concatenate([ref[pl.ds(base + c, rows, stride=SLAB), :] for c in range(SLAB)], axis=1)


def _slab_row(ref, r):
    return ref.at[pl.ds(pl.multiple_of(r * SLAB, SLAB), SLAB)]


def _outproj_kernel(x_ref, oa_ref, ob_ref, oc_ref, ga_ref, gb_ref, gc_ref, w_ref, gate_ref, sh_ref, sc_ref,
                    g2_ref, wr_hi_ref, wr_lo_ref, br_ref, x1_ref, h2_ref, lg_ref):
    na = _rms_rows(oa_ref[...].astype(F32), ga_ref[...]).astype(BF16)
    nb = _rms_rows(ob_ref[...].astype(F32), gb_ref[...]).astype(BF16)
    nc = _rms_rows(oc_ref[...].astype(F32), gc_ref[...]).astype(BF16)
    wa = A_Q
    wb = wa + MLA_HEADS * MLA_V
    y = _dot(na, w_ref[0:wa, :]) + _dot(nb, w_ref[wa:wb, :]) + _dot(nc, w_ref[wb:, :])
    x1 = x_ref[...] + gate_ref[...] * y
    x1_ref[...] = x1
    h2 = _rms_rows(x1, g2_ref[...]) * (1.0 + sc_ref[...]) + sh_ref[...]
    hi = h2.astype(BF16)
    lo = (h2 - hi.astype(F32)).astype(BF16)
    _store_slabs(h2_ref, h2)
    lg = _dot(hi, wr_hi_ref[...]) + _dot(lo, wr_hi_ref[...]) + _dot(hi, wr_lo_ref[...]) + br_ref[...]
    lg_ref[...] = lg.T


def _outproj(geom, x, oa, ob, oc, ga, gb, gc, w_out, mods, g2, wr_hi, wr_lo, br):
    tm = geom.tile(256)
    tok = lambda w: pl.BlockSpec((tm, w), lambda i: (i, 0))
    const = lambda shape: pl.BlockSpec(shape, lambda i: (0, 0))
    wb = MLA_HEADS * MLA_V
    return pl.pallas_call(
        _outproj_kernel,
        grid=(geom.t // tm,),
        in_specs=[tok(D_MODEL), tok(A_Q), tok(wb), tok(CONV_CH),
                  const((1, A_Q)), const((1, wb)), const((1, CONV_CH)),
                  const((D_MODEL, D_MODEL)),
                  _mod_spec(geom, tm, 2), _mod_spec(geom, tm, 3), _mod_spec(geom, tm, 4),
                  const((1, D_MODEL)), const((D_MODEL, LANES)), const((D_MODEL, LANES)), const((1, LANES))],
        out_specs=[tok(D_MODEL), pl.BlockSpec((tm * SLAB, LANES), lambda i: (i, 0)),
                   pl.BlockSpec((LANES, tm), lambda i: (0, i))],
        out_shape=[jax.ShapeDtypeStruct((geom.t, D_MODEL), F32),
                   jax.ShapeDtypeStruct((geom.t * SLAB, LANES), F32),
                   jax.ShapeDtypeStruct((LANES, geom.t), F32)],
        compiler_params=_cparams(("parallel",)),
        name="merge_outproj_norm2",
    )(x, oa, ob, oc, ga, gb, gc, w_out, mods, mods, mods, g2, wr_hi, wr_lo, br)


ROUTE_ROWS = 40


def _route_kernel(lg_ref, sel_ref, rt_ref, cnt_ref):
    lg = lg_ref[...]
    row = lax.broadcasted_iota(jnp.int32, lg.shape, 0)
    rowf = row.astype(F32)
    big = float(2 * LANES)
    is_g = jnp.logical_and(row >= N_EXPERTS, row < N_EXPERTS + N_GROUPS)
    gl = jnp.where(is_g, lg, NEG_INF)
    gmax = jnp.max(gl, axis=0, keepdims=True)
    gsum = jnp.sum(jnp.exp(gl - gmax), axis=0, keepdims=True)
    g_val = 1.0 / gsum
    g_idx = jnp.min(jnp.where(jnp.logical_and(is_g, gl == gmax), rowf, big), axis=0, keepdims=True) - N_EXPERTS
    lo = g_idx * EXPERTS_PER_GROUP
    is_e = jnp.logical_and(rowf >= lo, rowf < lo + EXPERTS_PER_GROUP)
    el = jnp.where(is_e, lg, NEG_INF)
    emax = jnp.max(el, axis=0, keepdims=True)
    ee = jnp.exp(el - emax)
    e_prob = ee / jnp.sum(ee, axis=0, keepdims=True)
    p1 = jnp.where(is_e, e_prob, -1.0)
    v1 = jnp.max(p1, axis=0, keepdims=True)
    i1 = jnp.min(jnp.where(p1 == v1, rowf, big), axis=0, keepdims=True)
    p2 = jnp.where(rowf == i1, -1.0, p1)
    v2 = jnp.max(p2, axis=0, keepdims=True)
    i2 = jnp.min(jnp.where(p2 == v2, rowf, big), axis=0, keepdims=True)
    scale = g_val / (v1 + v2)
    r8 = lax.broadcasted_iota(jnp.int32, rt_ref.shape, 0)
    rt_ref[...] = (jnp.where(r8 == 0, i1, 0.0) + jnp.where(r8 == 1, i2, 0.0) +
                   jnp.where(r8 == 2, v1 * scale, 0.0) + jnp.where(r8 == 3, v2 * scale, 0.0))
    chosen = jnp.where(jnp.logical_or(rowf == i1, rowf == i2), 1.0, 0.0).astype(BF16)
    cnt_ref[...] = _dot(chosen, sel_ref[...])


def _route(geom, tr, logits_t, sel):
    n = geom.t // tr
    return pl.pallas_call(
        _route_kernel,
        grid=(n,),
        in_specs=[pl.BlockSpec((ROUTE_ROWS, tr), lambda i: (0, i)), pl.BlockSpec(sel.shape, lambda i: (0, 0))],
        out_specs=[pl.BlockSpec((SUBLANES, tr), lambda i: (0, i)),
                   pl.BlockSpec((None, ROUTE_ROWS, LANES), lambda i: (i, 0, 0))],
        out_shape=[jax.ShapeDtypeStruct((SUBLANES, geom.t), F32), jax.ShapeDtypeStruct((n, ROUTE_ROWS, LANES), F32)],
        compiler_params=_cparams(("parallel",)),
        name="moe_route",
    )(logits_t, sel)


def _slots_kernel(tm, rt_ref, st_ref, tri_ref, sl_ref):
    rt = rt_ref[...]
    rowf = lax.broadcasted_iota(jnp.int32, (ROUTE_ROWS, rt.shape[1]), 0).astype(F32)
    oh1 = rowf == rt[0:1, :]
    oh2 = rowf == rt[1:2, :]
    chosen = jnp.where(jnp.logical_or(oh1, oh2), 1.0, 0.0).astype(BF16)
    for j in range(rt.shape[1] // tm):
        cols = slice(j * tm, (j + 1) * tm)
        first = _dot(chosen[:, cols], tri_ref[...]) + st_ref[:, j:j + 1]
        s1 = jnp.sum(jnp.where(oh1[:, cols], first, 0.0), axis=0, keepdims=True)
        s2 = jnp.sum(jnp.where(oh2[:, cols], first, 0.0), axis=0, keepdims=True)
        sl_ref[j] = jnp.concatenate([s1, s2], axis=1).astype(jnp.int32)


def _slots(geom, tr, tm, rt_t, start3, tri):
    ns = tr // tm
    return pl.pallas_call(
        functools.partial(_slots_kernel, tm),
        grid=(geom.t // tr,),
        in_specs=[pl.BlockSpec((SUBLANES, tr), lambda i: (0, i)),
                  pl.BlockSpec((None, ROUTE_ROWS, LANES), lambda i: (i, 0, 0)),
                  pl.BlockSpec((tm, tm), lambda i: (0, 0))],
        out_specs=pl.BlockSpec((ns, 1, 2 * tm), lambda i: (i, 0, 0)),
        out_shape=jax.ShapeDtypeStruct((geom.t // tm, 1, 2 * tm), jnp.int32),
        compiler_params=_cparams(("parallel",)),
        name="moe_slots",
    )(rt_t, start3, tri)


def _plan(cnt3, ns, n_tiles_max):
    n_steps = cnt3.shape[0]
    cnt_t = cnt3[:, :, :ns].transpose(0, 2, 1).reshape(n_steps * ns, ROUTE_ROWS)
    incl = jnp.cumsum(cnt_t, axis=0)
    tiles_e = jnp.ceil(incl[-1] / TME)
    end_e = jnp.cumsum(tiles_e)
    start = incl - cnt_t + ((end_e - tiles_e) * TME)[None, :]
    start3 = jnp.pad(start.reshape(n_steps, ns, ROUTE_ROWS).transpose(0, 2, 1), ((0, 0), (0, 0), (0, LANES - ns)))
    n_used = end_e[-1].astype(jnp.int32)
    tile_idx = jnp.minimum(jnp.arange(n_tiles_max, dtype=jnp.int32), n_used - 1)
    tile_exp = jnp.sum(tile_idx[:, None] >= end_e[None, :N_EXPERTS].astype(jnp.int32), axis=1).astype(jnp.int32)
    return start3, tile_exp, tile_idx, n_used.reshape(1)


def _scatter_kernel(sl_ref, h_ref, xs_in, xs_out, sem):
    del xs_in
    tm = h_ref.shape[0] // SLAB

    def body(rr, carry):
        for j in range(ROW_UNROLL):
            r = rr * ROW_UNROLL + j
            pltpu.make_async_copy(_slab_row(h_ref, r), _slab_row(xs_out, sl_ref[0, r]), sem).start(priority=0)
            pltpu.make_async_copy(_slab_row(h_ref, r), _slab_row(xs_out, sl_ref[0, tm + r]), sem).start(priority=1)
        return carry

    lax.fori_loop(0, tm // ROW_UNROLL, body, 0)
    for _ in range(2):
        pltpu.make_async_copy(h_ref, xs_out.at[pl.ds(0, tm * SLAB)], sem).wait()


def _scatter(geom, tm, slots_smem, h2, xs):
    return pl.pallas_call(
        _scatter_kernel,
        grid=(geom.t // tm,),
        in_specs=[pl.BlockSpec((None, 1, 2 * tm), lambda i: (i, 0, 0), memory_space=pltpu.SMEM),
                  pl.BlockSpec((tm * SLAB, LANES), lambda i: (i, 0)),
                  pl.BlockSpec(memory_space=pl.ANY)],
        out_specs=pl.BlockSpec(memory_space=pl.ANY),
        out_shape=jax.ShapeDtypeStruct(xs.shape, xs.dtype),
        scratch_shapes=[pltpu.SemaphoreType.DMA(())],
        input_output_aliases={2: 0},
        compiler_params=_cparams(("arbitrary",)),
        name="moe_scatter",
    )(slots_smem, h2, xs)


def _experts_kernel(te_ref, ti_ref, nu_ref, xs_ref, wg_ref, wu_ref, wd_ref, ys_ref):
    del te_ref, ti_ref

    @pl.when(pl.program_id(0) < nu_ref[0])
    def _():
        x = _load_slabs(xs_ref, TME).astype(BF16)
        act = (_silu(_dot(x, wg_ref[...])) * _dot(x, wu_ref[...])).astype(BF16)
        _store_slabs(ys_ref, _dot(act, wd_ref[...]))


def _experts(xs, tile_exp, tile_idx, n_used, wg, wu, wd):
    n_tiles_max = xs.shape[0] // (TME * SLAB)
    slab = pl.BlockSpec((TME * SLAB, LANES), lambda n, te, ti, nu: (ti[n], 0))
    wspec = lambda a, b: pl.BlockSpec((None, a, b), lambda n, te, ti, nu: (te[n], 0, 0))
    return pl.pallas_call(
        _experts_kernel,
        grid_spec=pltpu.PrefetchScalarGridSpec(
            num_scalar_prefetch=3,
            grid=(n_tiles_max,),
            in_specs=[slab, wspec(D_MODEL, D_EXPERT), wspec(D_MODEL, D_EXPERT), wspec(D_EXPERT, D_MODEL)],
            out_specs=slab),
        out_shape=jax.ShapeDtypeStruct(xs.shape, F32),
        compiler_params=_cparams(("arbitrary",)),
        name="moe_experts",
    )(tile_exp, tile_idx, n_used, xs, wg, wu, wd)


def _combine_kernel(sl_cur, sl_nxt, rt_ref, x1_ref, gate_ref, ys_hbm, o_ref, buf1, buf2, sem):
    i = pl.program_id(0)
    tm = x1_ref.shape[0]

    def issue(sl_ref, slot):
        def body(rr, carry):
            for j in range(ROW_UNROLL):
                r = rr * ROW_UNROLL + j
                dst = slot * tm + r
                pltpu.make_async_copy(_slab_row(ys_hbm, sl_ref[0, r]), _slab_row(buf1, dst),
                                      sem.at[slot]).start(priority=0)
                pltpu.make_async_copy(_slab_row(ys_hbm, sl_ref[0, tm + r]), _slab_row(buf2, dst),
                                      sem.at[slot]).start(priority=1)
            return carry
        lax.fori_loop(0, tm // ROW_UNROLL, body, 0)

    @pl.when(i == 0)
    def _():
        issue(sl_cur, 0)

    @pl.when(i + 1 < pl.num_programs(0))
    def _():
        issue(sl_nxt, (i + 1) % 2)

    slot = i % 2
    base = pl.multiple_of(slot * (tm * SLAB), tm * SLAB)
    for buf in (buf1, buf2):
        pltpu.make_async_copy(ys_hbm.at[pl.ds(0, tm * SLAB)], buf.at[pl.ds(base, tm * SLAB)], sem.at[slot]).wait()
    w = jnp.concatenate([rt_ref[...], jnp.zeros((LANES - SUBLANES, tm), F32)], axis=0).T
    y = w[:, 2:3] * _load_slabs(buf1, tm, base) + w[:, 3:4] * _load_slabs(buf2, tm, base)
    o_ref[...] = x1_ref[...] + gate_ref[...] * y


def _combine(geom, tm, slots_smem, rt, x1, mods, ys):
    n = geom.t // tm
    tok = lambda w: pl.BlockSpec((tm, w), lambda i: (i, 0))
    smem = lambda f: pl.BlockSpec((None, 1, 2 * tm), f, memory_space=pltpu.SMEM)
    return pl.pallas_call(
        _combine_kernel,
        grid=(n,),
        in_specs=[smem(lambda i: (i, 0, 0)), smem(lambda i: (jnp.minimum(i + 1, n - 1), 0, 0)),
                  pl.BlockSpec((SUBLANES, tm), lambda i: (0, i)), tok(D_MODEL), _mod_spec(geom, tm, 5),
                  pl.BlockSpec(memory_space=pl.ANY)],
        out_specs=tok(D_MODEL),
        out_shape=jax.ShapeDtypeStruct((geom.t, D_MODEL), F32),
        scratch_shapes=[pltpu.VMEM((2 * tm * SLAB, LANES), F32), pltpu.VMEM((2 * tm * SLAB, LANES), F32),
                        pltpu.SemaphoreType.DMA((2,))],
        compiler_params=_cparams(("arbitrary",)),
        name="moe_combine",
    )(slots_smem, slots_smem, rt, x1, mods, ys)


def _block_diag_ones(n, blk):
    idx = np.arange(n) // blk
    return jnp.asarray((idx[:, None] == idx[None, :]).astype(np.float32), dtype=BF16)


def _rope_tables(smax):
    inv_freq = 1.0 / (ROPE_BASE ** (jnp.arange(0, MLA_ROPE, 2, dtype=F32) / MLA_ROPE))
    ang = jnp.arange(smax, dtype=F32)[:, None] * inv_freq[None, :]
    cos, sin = jnp.cos(ang), jnp.sin(ang)
    zeros = lambda w: jnp.zeros((smax, w), F32)
    pad = LANES - MLA_QK
    c = jnp.concatenate([jnp.ones((smax, MLA_NOPE), F32), cos, cos, zeros(pad)], axis=1)
    s = jnp.concatenate([zeros(MLA_NOPE), -sin, sin, zeros(pad)], axis=1)
    return c, s


def _pad_cols(w, n):
    return jnp.pad(w, ((0, 0), (0, n - w.shape[1])))


def _layer_weights(l, w_in, mla_w_q_up, mla_w_kv_up, mla_q_gain, mla_k_gain, swa_q_gain, swa_k_gain, swa_sink,
                   moe_w_group, moe_b_group, moe_w_expert, moe_b_expert):
    wi = w_in[l]
    kr = jnp.pad(wi[:, A_IN + MLA_Q_RANK + MLA_KV_RANK:A_IN + B_IN], ((0, 0), (MLA_NOPE, LANES - MLA_QK)))
    w_in_p = jnp.concatenate([wi[:, :A_IN], wi[:, A_IN:A_IN + MLA_Q_RANK + MLA_KV_RANK], kr, wi[:, A_IN + B_IN:]],
                             axis=1).astype(BF16)
    wq = mla_w_q_up[l].reshape(MLA_Q_RANK, MLA_HEADS, MLA_QK)
    wq_p = jnp.pad(wq, ((0, 0), (0, 0), (0, LANES - MLA_QK)))
    wq2 = jnp.concatenate([wq_p, _swap_rope_halves(wq_p)], axis=2).reshape(MLA_Q_RANK, MLA_HEADS * 2 * LANES).astype(BF16)
    wvt = mla_w_kv_up[l].reshape(MLA_KV_RANK, MLA_HEADS, MLA_NOPE + MLA_V)[:, :, MLA_NOPE:].transpose(1, 2, 0).astype(BF16)
    gq_p = _pad_cols(mla_q_gain[l][None, :], LANES)
    gk_p = _pad_cols(mla_k_gain[l][None, :], LANES)
    gq_s, gk_s = _swap_rope_halves(gq_p), _swap_rope_halves(gk_p)
    gq_a = jnp.tile(swa_q_gain[l], SWA_HEADS)[None, :]
    gk_a = jnp.tile(swa_k_gain[l], SWA_KV_HEADS)[None, :]
    sink = jnp.repeat(swa_sink[l] * LOG2E, BLOCK).reshape(SWA_KV_HEADS, 1, SWA_GROUP * BLOCK)
    wr = _pad_cols(jnp.concatenate([moe_w_expert[l], moe_w_group[l]], axis=1), LANES)
    wr_hi = wr.astype(BF16)
    wr_lo = (wr - wr_hi.astype(F32)).astype(BF16)
    br = _pad_cols(jnp.concatenate([moe_b_expert[l], moe_b_group[l]])[None, :], LANES)
    return w_in_p, wq2, wvt, gq_p, gq_s, gk_p, gk_s, gq_a, gk_a, sink, wr_hi, wr_lo, br


def kernel(x_prompt, x_sample, c_prompt, c_sample, w_ada, b_ada, norm1_g, norm2_g, w_in, swa_q_gain, swa_k_gain, swa_sink, mla_q_norm_g, mla_w_q_up, mla_kv_norm_g, mla_w_kv_up, mla_q_gain, mla_k_gain, conv_dw_w, conv_dw_b, conv_gn_g, conv_gn_b, conv_w_pw2, conv_b_pw2, out_norm_a, out_norm_b, out_norm_c, w_out, moe_w_group, moe_b_group, moe_w_expert, moe_b_expert, moe_w_gate, moe_w_up, moe_w_down):
    bp, sp, d = x_prompt.shape
    bs, ss, _ = x_sample.shape
    assert d == D_MODEL
    geom = _Geom(bp, sp, bs, ss)
    depth = w_ada.shape[0]

    x = jnp.concatenate([x_prompt.reshape(bp * sp, d), x_sample.reshape(bs * ss, d)], axis=0)
    c = jnp.concatenate([c_prompt, c_sample], axis=0)
    rows = -(-geom.nb // 8) * 8
    c_pad = jnp.pad(c, ((0, rows - geom.nb), (0, 0)))
    mods_all = _modulation(c_pad, w_ada, b_ada)

    rope_c, rope_s = _rope_tables(max(sp, ss))
    rope_perm = _swap_rope_halves(jnp.eye(LANES, dtype=F32)).astype(BF16)
    ones2 = _block_diag_ones(2 * LANES, LANES)
    bdq = _block_diag_ones(A_Q, HEAD_DIM)
    bdk = _block_diag_ones(A_KV, HEAD_DIM)
    bdc = _block_diag_ones(CONV_CH, CONV_CH // CONV_GROUPS)
    swa_bias = _swa_bias()
    row = lambda v: v[None, :]
    tm_moe = geom.tile(256)
    n_tiles_max = 2 * geom.t // TME + N_EXPERTS
    tr_moe = geom.tile(2048)
    tri = jnp.asarray(np.triu(np.ones((tm_moe, tm_moe), np.float32), 1), dtype=BF16)
    sel = jnp.asarray(np.arange(tr_moe)[:, None] // tm_moe == np.arange(LANES)[None, :], dtype=BF16)
    xs = jnp.zeros((n_tiles_max * TME * SLAB, LANES), F32)

    for l in range(depth):
        (w_in_p, wq2, wvt, gq_p, gq_s, gk_p, gk_s, gq_a, gk_a, sink, wr_hi, wr_lo, br) = _layer_weights(
            l, w_in, mla_w_q_up, mla_w_kv_up, mla_q_gain, mla_k_gain, swa_q_gain, swa_k_gain, swa_sink,
            moe_w_group, moe_b_group, moe_w_expert, moe_b_expert)
        mods = mods_all[l, :geom.nb].reshape(geom.nb * N_MOD, 1, D_MODEL)

        q_a, k_a, v_a, cq, ckv, kr, uc = _inproj(geom, x, mods, row(norm1_g[l]), w_in_p)
        out_a = _swa(geom, q_a, k_a, v_a, gq_a, gk_a, bdq, bdk, sink, swa_bias)
        tk = geom.tile(512)
        q_b, k_b, vt_b = _mla_prep(geom, tk, cq, ckv, kr, row(mla_q_norm_g[l]), row(mla_kv_norm_g[l]), wq2,
                                   mla_w_kv_up[l].astype(BF16), wvt, rope_perm, ones2, gq_p, gq_s, gk_p, gk_s,
                                   rope_c, rope_s)
        out_b = _mla_attn(geom, tk, q_b, k_b, vt_b)
        dw_w = jnp.pad(conv_dw_w[l].reshape(CONV_WIDTH, CONV_CH), ((0, 32 - CONV_WIDTH), (0, 0)))
        out_c = _conv(geom, uc, dw_w, row(conv_dw_b[l]), row(conv_gn_g[l]), row(conv_gn_b[l]), bdc,
                      conv_w_pw2[l].astype(BF16), row(conv_b_pw2[l]))
        x1, h2, logits = _outproj(geom, x, out_a, out_b, out_c, row(out_norm_a[l]), row(out_norm_b[l]),
                                  row(out_norm_c[l]), w_out[l].astype(BF16), mods, row(norm2_g[l]),
                                  wr_hi, wr_lo, br)
        rt, cnt3 = _route(geom, tr_moe, logits, sel)
        start3, tile_exp, tile_idx, n_used = _plan(cnt3, tr_moe // tm_moe, n_tiles_max)
        slots = _slots(geom, tr_moe, tm_moe, rt, start3, tri)
        xs = _scatter(geom, tm_moe, slots, h2, xs)
        ys = _experts(xs, tile_exp, tile_idx, n_used, moe_w_gate[l].astype(BF16), moe_w_up[l].astype(BF16),
                      moe_w_down[l].astype(BF16))
        x = _combine(geom, tm_moe, slots, rt, x1, mods, ys)

    y_prompt = x[:geom.tp].reshape(bp, sp, d)
    y_sample = x[geom.tp:].reshape(bs, ss, d)
    return (y_prompt, y_sample)
```

```python
import functools

import numpy as np
import jax
import jax.numpy as jnp
from jax import lax
from jax.experimental import pallas as pl
from jax.experimental.pallas import tpu as pltpu

F32 = jnp.float32
BF16 = jnp.bfloat16

EPS = 1e-6
NEG_INF = -1e30
LOG2E = 1.4426950408889634

D_MODEL = 1024
HEAD_DIM = 64
SWA_HEADS = 6
SWA_KV_HEADS = 2
SWA_GROUP = SWA_HEADS // SWA_KV_HEADS
WINDOW = 128
BLOCK = WINDOW
MLA_HEADS = 6
MLA_Q_RANK = 256
MLA_KV_RANK = 128
MLA_NOPE = 64
MLA_ROPE = 32
MLA_QK = MLA_NOPE + MLA_ROPE
MLA_V = 64
MLA_VT_ROWS = MLA_V + 16
ROPE_BASE = 10000.0
CONV_CH = 256
CONV_GROUPS = 4
CONV_WIDTH = 31
CONV_PAD = (CONV_WIDTH - 1) // 2
A_Q = SWA_HEADS * HEAD_DIM
A_KV = SWA_KV_HEADS * HEAD_DIM
A_IN = A_Q + 2 * A_KV
B_IN = MLA_Q_RANK + MLA_KV_RANK + MLA_ROPE
N_GROUPS = 4
EXPERTS_PER_GROUP = 8
N_EXPERTS = N_GROUPS * EXPERTS_PER_GROUP
D_EXPERT = 256
N_MOD = 6

LANES = 128
SUBLANES = 8
SLAB = D_MODEL // 128
TME = 256
ROW_UNROLL = 8
HALO = 16
VMEM_LIMIT = 48 * 1024 * 1024


class _Geom:
    def __init__(self, bp, sp, bs, ss):
        self.bp, self.sp, self.bs, self.ss = bp, sp, bs, ss
        self.tp = bp * sp
        self.t = bp * sp + bs * ss
        self.nb = bp + bs

    def tile(self, target):
        t = target
        while self.sp % t or self.ss % t:
            t //= 2
        return t

    def batch(self, i, tm):
        npt = self.tp // tm
        return jnp.where(i < npt, i // (self.sp // tm), self.bp + (i - npt) // (self.ss // tm))

    def pos(self, i, tm):
        npt = self.tp // tm
        return jnp.where(i < npt, i % (self.sp // tm), (i - npt) % (self.ss // tm))

    def is_last(self, i, tm):
        npt = self.tp // tm
        return jnp.where(i < npt, i % (self.sp // tm) == self.sp // tm - 1,
                         (i - npt) % (self.ss // tm) == self.ss // tm - 1)


def _cparams(sem):
    return pltpu.CompilerParams(dimension_semantics=sem, vmem_limit_bytes=VMEM_LIMIT)


def _silu(x):
    return x * jax.nn.sigmoid(x)


def _dot(a, b):
    return jnp.dot(a, b, preferred_element_type=F32)


def _dot_nt(a, b):
    return lax.dot_general(a, b, (((1,), (1,)), ((), ())), preferred_element_type=F32)


def _split_dot(x, w):
    hi = x.astype(BF16)
    lo = (x - hi.astype(F32)).astype(BF16)
    return _dot(hi, w) + _dot(lo, w)


def _mod_kernel(c_ref, w_ref, b_ref, o_ref):
    c = c_ref[...]
    o_ref[...] = _dot(_silu(c).astype(BF16), w_ref[...].astype(BF16)) + b_ref[...]


def _modulation(c_pad, w_ada, b_ada):
    L, d, n = w_ada.shape
    tn = 768
    rows = c_pad.shape[0]
    return pl.pallas_call(
        _mod_kernel,
        grid=(L, n // tn),
        in_specs=[pl.BlockSpec((rows, d), lambda l, j: (0, 0)),
                  pl.BlockSpec((None, d, tn), lambda l, j: (l, 0, j)),
                  pl.BlockSpec((None, 1, tn), lambda l, j: (l, 0, j))],
        out_specs=pl.BlockSpec((None, rows, tn), lambda l, j: (l, 0, j)),
        out_shape=jax.ShapeDtypeStruct((L, rows, n), F32),
        compiler_params=_cparams(("arbitrary", "arbitrary")),
        name="adaln_mod",
    )(c_pad, w_ada, b_ada.reshape(L, 1, n))


_IN_SEGS = (("q", A_Q), ("k", A_KV), ("v", A_KV), ("cq", MLA_Q_RANK), ("ckv", MLA_KV_RANK), ("kr", LANES),
            ("uc", 2 * CONV_CH))


def _norm1_project(x, sh_ref, sc_ref, g_ref, w_ref, out_refs):
    ms = jnp.mean(x * x, axis=-1, keepdims=True)
    h = x * lax.rsqrt(ms + EPS) * g_ref[...]
    h = (h * (1.0 + sc_ref[...]) + sh_ref[...]).astype(BF16)
    off = 0
    for (_, width), o_ref in zip(_IN_SEGS, out_refs):
        o_ref[...] = _dot(h, w_ref[:, off:off + width]).astype(o_ref.dtype)
        off += width


def _inproj_kernel(x_ref, sh_ref, sc_ref, g_ref, w_ref, *out_refs):
    _norm1_project(x_ref[...], sh_ref, sc_ref, g_ref, w_ref, out_refs)


def _mod_spec(geom, tm, k):
    return pl.BlockSpec((None, 1, D_MODEL), lambda i: (geom.batch(i, tm) * N_MOD + k, 0, 0))


def _inproj(geom, x, mods, g1, w_in_p):
    tm = geom.tile(512)
    nw = w_in_p.shape[1]
    return pl.pallas_call(
        _inproj_kernel,
        grid=(geom.t // tm,),
        in_specs=[pl.BlockSpec((tm, D_MODEL), lambda i: (i, 0)),
                  _mod_spec(geom, tm, 0), _mod_spec(geom, tm, 1),
                  pl.BlockSpec((1, D_MODEL), lambda i: (0, 0)),
                  pl.BlockSpec((D_MODEL, nw), lambda i: (0, 0))],
        out_specs=[pl.BlockSpec((tm, w), lambda i: (i, 0)) for _, w in _IN_SEGS],
        out_shape=[jax.ShapeDtypeStruct((geom.t, w), BF16) for _, w in _IN_SEGS],
        compiler_params=_cparams(("parallel",)),
        name="norm1_inproj",
    )(x, mods, mods, g1, w_in_p)


def _swa_kernel(geom, nb, q_ref, kp_ref, kc_ref, kn_ref, vp_ref, vc_ref, vn_ref, gq_ref, gk_ref,
                bdq_ref, bdk_ref, sink_ref, bias_ref, o_ref):
    i = pl.program_id(0)
    first = (geom.pos(i, nb * BLOCK) == 0).astype(jnp.int32)
    last = geom.is_last(i, nb * BLOCK).astype(jnp.int32)

    q = q_ref[...].astype(F32)
    msq = _dot((q * q).astype(BF16), bdq_ref[...]) * (1.0 / HEAD_DIM)
    q_t = (q * lax.rsqrt(msq + EPS) * gq_ref[...] * (HEAD_DIM ** -0.5 * LOG2E)).T.astype(BF16)

    k_all = jnp.concatenate([kp_ref[...], kc_ref[...], kn_ref[...]], axis=0).astype(F32)
    msk = _dot((k_all * k_all).astype(BF16), bdk_ref[...]) * (1.0 / HEAD_DIM)
    k_all = (k_all * lax.rsqrt(msk + EPS) * gk_ref[...]).astype(BF16)
    v_all = jnp.concatenate([vp_ref[...], vc_ref[...], vn_ref[...]], axis=0).astype(F32)
    ones_rows = 16
    v_t = jnp.concatenate([v_all.T, jnp.ones((ones_rows, v_all.shape[0]), F32)], axis=0).astype(BF16)
    zeros = jnp.zeros((HEAD_DIM, SWA_GROUP * BLOCK), BF16)

    for b in range(nb):
        cls = (first if b == 0 else 0) + (2 * last if b == nb - 1 else 0)
        kb = k_all[b * BLOCK:(b + 3) * BLOCK, :]
        vb = v_t[:, b * BLOCK:(b + 3) * BLOCK]
        outs = []
        for g in range(SWA_KV_HEADS):
            heads = range(g * SWA_GROUP, (g + 1) * SWA_GROUP)
            qg = jnp.concatenate([q_t[h * HEAD_DIM:(h + 1) * HEAD_DIM, b * BLOCK:(b + 1) * BLOCK] for h in heads],
                                 axis=1)
            q_pad = jnp.concatenate([qg if j == g else zeros for j in range(SWA_KV_HEADS)], axis=0)
            s = _dot(kb, q_pad) + bias_ref[cls, g]
            sink = sink_ref[g]
            m = jnp.maximum(jnp.max(s, axis=0, keepdims=True), sink)
            p = jnp.exp2(s - m).astype(BF16)
            o = _dot(vb, p)
            denom = o[A_KV:A_KV + 1, :] + jnp.exp2(sink - m)
            og = o[g * HEAD_DIM:(g + 1) * HEAD_DIM, :] / denom
            outs += [og[:, j * BLOCK:(j + 1) * BLOCK] for j in range(SWA_GROUP)]
        o_ref[b * BLOCK:(b + 1) * BLOCK, :] = jnp.concatenate(outs, axis=0).T.astype(o_ref.dtype)


def _swa_bias():
    k = np.arange(3 * BLOCK)[:, None]
    q = np.arange(BLOCK)[None, :]
    rel = np.abs(k - BLOCK - q)
    out = np.zeros((4, SWA_KV_HEADS, 3 * BLOCK, SWA_GROUP * BLOCK), np.float32)
    for c in range(4):
        k_lo = BLOCK if c & 1 else 0
        k_hi = 2 * BLOCK if c & 2 else 3 * BLOCK
        valid = (rel <= WINDOW) & (k >= k_lo) & (k < k_hi)
        for g in range(SWA_KV_HEADS):
            for j in range(SWA_GROUP):
                slope = 2.0 ** (-8.0 * (g * SWA_GROUP + j + 1) / SWA_HEADS)
                out[c, g, :, j * BLOCK:(j + 1) * BLOCK] = np.where(valid, -slope * LOG2E * rel, NEG_INF)
    return jnp.asarray(out)


def _swa(geom, q, k, v, gq, gk, bdq, bdk, sink, bias):
    ts = geom.tile(512)
    nb = ts // BLOCK
    nblk = geom.t // BLOCK
    prev = lambda i: (jnp.maximum(i * nb - 1, 0), 0)
    cur = lambda i: (i, 0)
    nxt = lambda i: (jnp.minimum((i + 1) * nb, nblk - 1), 0)
    halo = lambda f: pl.BlockSpec((BLOCK, A_KV), f)
    const = lambda shape: pl.BlockSpec(shape, lambda i: (0,) * len(shape))
    return pl.pallas_call(
        functools.partial(_swa_kernel, geom, nb),
        grid=(geom.t // ts,),
        in_specs=[pl.BlockSpec((ts, A_Q), cur),
                  halo(prev), pl.BlockSpec((ts, A_KV), cur), halo(nxt),
                  halo(prev), pl.BlockSpec((ts, A_KV), cur), halo(nxt),
                  const((1, A_Q)), const((1, A_KV)), const((A_Q, A_Q)), const((A_KV, A_KV)),
                  const((SWA_KV_HEADS, 1, SWA_GROUP * BLOCK)), const(bias.shape)],
        out_specs=pl.BlockSpec((ts, A_Q), cur),
        out_shape=jax.ShapeDtypeStruct((geom.t, A_Q), BF16),
        compiler_params=_cparams(("parallel",)),
        name="swa_attention",
    )(q, k, k, k, v, v, v, gq, gk, bdq, bdk, sink, bias)


def _swap_rope_halves(a):
    lo, hi = MLA_NOPE, MLA_NOPE + MLA_ROPE // 2
    z = jnp.zeros_like(a)
    return jnp.concatenate([z[..., :lo], a[..., hi:MLA_QK], a[..., lo:hi], z[..., MLA_QK:]], axis=-1)


def _mla_prep_kernel(cq_ref, ckv_ref, kr_ref, gqn_ref, gkvn_ref, wq_ref, wkv_ref, wvt_ref, perm_ref, ones_ref,
                     gq_ref, gqs_ref, gk_ref, gks_ref, c_ref, s_ref, q_out, k_out, vt_out):
    cq = cq_ref[...].astype(F32)
    qn = (cq * lax.rsqrt(jnp.mean(cq * cq, axis=-1, keepdims=True) + EPS) * gqn_ref[...]).astype(BF16)
    ckv = ckv_ref[...].astype(F32)
    kvn = (ckv * lax.rsqrt(jnp.mean(ckv * ckv, axis=-1, keepdims=True) + EPS) * gkvn_ref[...]).astype(BF16)
    kr_b = kr_ref[...]
    kr = kr_b.astype(F32)
    lane = lax.broadcasted_iota(jnp.int32, (1, LANES), 1)
    ones = jnp.ones((MLA_VT_ROWS - MLA_V, cq.shape[0]), BF16)
    c = c_ref[...]
    sn = s_ref[...]
    q_c = c * (gq_ref[...] * (MLA_QK ** -0.5 * LOG2E))
    q_s = sn * (gqs_ref[...] * (MLA_QK ** -0.5 * LOG2E))
    k_c = c * gk_ref[...]
    k_rot = _dot(kr_b, perm_ref[...]) * (sn * gks_ref[...])
    inv_d = 1.0 / MLA_QK
    for hp in range(MLA_HEADS // 2):
        kv2 = _dot(kvn, wkv_ref[:, hp * 2 * LANES:(hp + 1) * 2 * LANES])
        for hh in range(2):
            h = 2 * hp + hh
            xq = _dot(qn, wq_ref[:, h * 2 * LANES:(h + 1) * 2 * LANES])
            x, xs = xq[:, :LANES], xq[:, LANES:]
            xk = jnp.where(lane < MLA_NOPE, kv2[:, hh * LANES:(hh + 1) * LANES], 0.0) + kr
            ss = _dot(jnp.concatenate([x * x, xk * xk], axis=1).astype(BF16), ones_ref[...])
            rq = lax.rsqrt(ss[:, :LANES] * inv_d + EPS)
            rk = lax.rsqrt(ss[:, LANES:] * inv_d + EPS)
            q_out[h] = (rq * (x * q_c + xs * q_s)).astype(BF16)
            k_out[h] = (rk * (xk * k_c + k_rot)).astype(BF16)
            vt_out[h, 0] = jnp.concatenate([_dot_nt(wvt_ref[h], kvn).astype(BF16), ones], axis=0)


def _mla_prep(geom, tk, cq, ckv, kr, gqn, gkvn, wq2, wkv, wvt, perm, ones2, gq_p, gq_s, gk_p, gk_s, rope_c, rope_s):
    tm = tk
    tok = lambda w: pl.BlockSpec((tm, w), lambda i: (i, 0))
    const = lambda shape: pl.BlockSpec(shape, lambda i: (0,) * len(shape))
    rope = pl.BlockSpec((tm, LANES), lambda i: (geom.pos(i, tm), 0))
    hm = pl.BlockSpec((MLA_HEADS, tm, LANES), lambda i: (0, i, 0))
    hm_t = pl.BlockSpec((MLA_HEADS, 1, MLA_VT_ROWS, tm), lambda i: (0, i, 0, 0))
    return pl.pallas_call(
        _mla_prep_kernel,
        grid=(geom.t // tm,),
        in_specs=[tok(MLA_Q_RANK), tok(MLA_KV_RANK), tok(LANES),
                  const((1, MLA_Q_RANK)), const((1, MLA_KV_RANK)),
                  const(wq2.shape), const(wkv.shape), const(wvt.shape), const(perm.shape), const(ones2.shape),
                  const((1, LANES)), const((1, LANES)), const((1, LANES)), const((1, LANES)), rope, rope],
        out_specs=[hm, hm, hm_t],
        out_shape=[jax.ShapeDtypeStruct((MLA_HEADS, geom.t, LANES), BF16),
                   jax.ShapeDtypeStruct((MLA_HEADS, geom.t, LANES), BF16),
                   jax.ShapeDtypeStruct((MLA_HEADS, geom.t // tm, MLA_VT_ROWS, tm), BF16)],
        compiler_params=_cparams(("parallel",)),
        name="mla_prep",
    )(cq, ckv, kr, gqn, gkvn, wq2, wkv, wvt, perm, ones2, gq_p, gq_s, gk_p, gk_s, rope_c, rope_s)


def _mla_attn_kernel(nk, q_ref, k_ref, vt_ref, o_ref, s_ref, mc_ref, p_ref, al_ref, acc_ref):
    tk = vt_ref.shape[3]
    tq = q_ref.shape[1]

    def stage_a(chunk, slot):
        off = chunk * tk
        if not isinstance(off, int):
            off = pl.multiple_of(off, tk)
        for hh in range(2):
            s = _dot_nt(k_ref[hh, pl.ds(off, tk), :], q_ref[hh])
            s_ref[slot, hh] = s
            mc_ref[slot, hh] = jnp.max(s, axis=0, keepdims=True)

    def stage_b(slot, m):
        m_out = []
        for hh in range(2):
            m_new = jnp.maximum(m[hh], mc_ref[slot, hh])
            al_ref[slot, hh] = jnp.exp2(m[hh] - m_new)
            p_ref[slot, hh] = jnp.exp2(s_ref[slot, hh] - m_new).astype(BF16)
            m_out.append(m_new)
        return tuple(m_out)

    def stage_c(chunk, slot):
        for hh in range(2):
            acc_ref[hh] = acc_ref[hh] * al_ref[slot, hh] + _dot(vt_ref[hh, chunk], p_ref[slot, hh])

    stage_a(0, 0)
    p_ref[1] = jnp.zeros(p_ref.shape[1:], BF16)
    al_ref[1] = jnp.ones(al_ref.shape[1:], F32)
    acc_ref[...] = jnp.zeros(acc_ref.shape, F32)
    m = (jnp.full((1, tq), NEG_INF, F32),) * 2

    def body(jj, m):
        j = 2 * jj
        stage_a(j + 1, 1)
        m = stage_b(0, m)
        stage_c(jnp.maximum(j - 1, 0), 1)
        stage_a(j + 2, 0)
        m = stage_b(1, m)
        stage_c(j, 0)
        return m

    m = lax.fori_loop(0, nk // 2 - 1, body, m)
    stage_a(nk - 1, 1)
    m = stage_b(0, m)
    stage_c(max(nk - 3, 0), 1)
    m = stage_b(1, m)
    stage_c(nk - 2, 0)
    stage_c(nk - 1, 1)
    o_t = jnp.concatenate([acc_ref[hh][:MLA_V, :] / acc_ref[hh][MLA_V:MLA_V + 1, :] for hh in range(2)], axis=0)
    o_ref[...] = o_t.T.astype(o_ref.dtype)


def _mla_attn_group(geom, tk, q, k, vt, out_prev, nseq, s, tok0):
    tq = min(1024, s)
    nq = s // tq
    nk = s // tk
    assert nk % 2 == 0
    qb0 = tok0 // tq
    sb0 = tok0 // s
    args = [q, k, vt]
    in_specs = [pl.BlockSpec((2, tq, LANES), lambda b, hp, i: (hp, qb0 + b * nq + i, 0)),
                pl.BlockSpec((2, s, LANES), lambda b, hp, i: (hp, sb0 + b, 0)),
                pl.BlockSpec((2, nk, MLA_VT_ROWS, tk), lambda b, hp, i: (hp, sb0 + b, 0, 0))]
    aliases = {}
    kern = functools.partial(_mla_attn_kernel, nk)
    if out_prev is not None:
        args.append(out_prev)
        in_specs.append(pl.BlockSpec(memory_space=pl.ANY))
        aliases = {3: 0}
        kern = lambda q_ref, k_ref, vt_ref, prev_ref, *rest, _k=kern: _k(q_ref, k_ref, vt_ref, *rest)
    return pl.pallas_call(
        kern,
        grid=(nseq, MLA_HEADS // 2, nq),
        in_specs=in_specs,
        out_specs=pl.BlockSpec((tq, LANES), lambda b, hp, i: (qb0 + b * nq + i, hp)),
        out_shape=jax.ShapeDtypeStruct((geom.t, MLA_HEADS * MLA_V), BF16),
        scratch_shapes=[pltpu.VMEM((2, 2, tk, tq), F32), pltpu.VMEM((2, 2, 1, tq), F32),
                        pltpu.VMEM((2, 2, tk, tq), BF16), pltpu.VMEM((2, 2, 1, tq), F32),
                        pltpu.VMEM((2, MLA_VT_ROWS, tq), F32)],
        input_output_aliases=aliases,
        compiler_params=_cparams(("parallel", "parallel", "arbitrary")),
        name="mla_attention",
    )(*args)


def _mla_attn(geom, tk, q, k, vt):
    assert geom.tp % geom.ss == 0
    out = _mla_attn_group(geom, tk, q, k, vt, None, geom.bp, geom.sp, 0)
    return _mla_attn_group(geom, tk, q, k, vt, out, geom.bs, geom.ss, geom.tp)


def _conv_kernel(geom, tc, up_ref, uc_ref, un_ref, w_ref, b_ref, gng_ref, gnb_ref, bd_ref, wpw_ref, bpw_ref,
                 o_ref, hs_ref):
    i = pl.program_id(0)
    first = geom.pos(i, tc) == 0
    last = geom.is_last(i, tc)

    def glu(u):
        u = u.astype(F32)
        return u[:, :CONV_CH] * jax.nn.sigmoid(u[:, CONV_CH:])

    hs_ref[0, 0:HALO, :] = jnp.where(first, 0.0, glu(up_ref[...]))
    hs_ref[0, HALO:HALO + tc, :] = glu(uc_ref[...])
    hs_ref[0, HALO + tc:HALO + tc + HALO, :] = jnp.where(last, 0.0, glu(un_ref[...]))
    span = tc + 2 * HALO - SUBLANES
    for sft in range(1, SUBLANES):
        hs_ref[sft, 0:span, :] = hs_ref[0, sft:sft + span, :]

    rows = 64 if tc % 64 == 0 else tc
    for r0 in range(0, tc, rows):
        acc = jnp.zeros((rows, CONV_CH), F32) + b_ref[...]
        for j in range(CONV_WIDTH):
            start = HALO + r0 + j - CONV_PAD
            sft = start % SUBLANES
            acc = acc + hs_ref[sft, start - sft:start - sft + rows, :] * w_ref[j:j + 1, :]
        mu = _split_dot(acc, bd_ref[...]) * (CONV_GROUPS / CONV_CH)
        d = acc - mu
        var = _split_dot(d * d, bd_ref[...]) * (CONV_GROUPS / CONV_CH)
        hn = d * lax.rsqrt(var + EPS) * gng_ref[...] + gnb_ref[...]
        y = _dot(_silu(hn).astype(BF16), wpw_ref[...]) + bpw_ref[...]
        o_ref[r0:r0 + rows, :] = y.astype(o_ref.dtype)


def _conv(geom, uc, dw_w, dw_b, gn_g, gn_b, bd, w_pw2, b_pw2):
    tc = geom.tile(256)
    hb = tc // HALO
    nh = geom.t // HALO
    const = lambda shape: pl.BlockSpec(shape, lambda i: (0, 0))
    return pl.pallas_call(
        functools.partial(_conv_kernel, geom, tc),
        grid=(geom.t // tc,),
        in_specs=[pl.BlockSpec((HALO, 2 * CONV_CH), lambda i: (jnp.maximum(i * hb - 1, 0), 0)),
                  pl.BlockSpec((tc, 2 * CONV_CH), lambda i: (i, 0)),
                  pl.BlockSpec((HALO, 2 * CONV_CH), lambda i: (jnp.minimum((i + 1) * hb, nh - 1), 0)),
                  const((32, CONV_CH)), const((1, CONV_CH)), const((1, CONV_CH)), const((1, CONV_CH)),
                  const((CONV_CH, CONV_CH)), const((CONV_CH, CONV_CH)), const((1, CONV_CH))],
        out_specs=pl.BlockSpec((tc, CONV_CH), lambda i: (i, 0)),
        out_shape=jax.ShapeDtypeStruct((geom.t, CONV_CH), BF16),
        scratch_shapes=[pltpu.VMEM((SUBLANES, tc + 2 * HALO, CONV_CH), F32)],
        compiler_params=_cparams(("parallel",)),
        name="conformer_conv",
    )(uc, uc, uc, dw_w, dw_b, gn_g, gn_b, bd, w_pw2, b_pw2)


def _rms_rows(x, g):
    return x * lax.rsqrt(jnp.mean(x * x, axis=-1, keepdims=True) + EPS) * g


def _store_slabs(ref, x, base=0):
    rows = x.shape[0]
    for c in range(SLAB):
        ref[pl.ds(base + c, rows, stride=SLAB), :] = x[:, c * LANES:(c + 1) * LANES]


def _load_slabs(ref, rows, base=0):
    return jnp.concatenate([ref[pl.ds(base + c, rows, stride=SLAB), :] for c in range(SLAB)], axis=1)


def _slab_row(ref, r):
    return ref.at[pl.ds(pl.multiple_of(r * SLAB, SLAB), SLAB)]


def _outproj_kernel(x_ref, oa_ref, ob_ref, oc_ref, ga_ref, gb_ref, gc_ref, w_ref, gate_ref, sh_ref, sc_ref,
                    g2_ref, wr_hi_ref, wr_lo_ref, br_ref, x1_ref, h2_ref, lg_ref):
    na = _rms_rows(oa_ref[...].astype(F32), ga_ref[...]).astype(BF16)
    nb = _rms_rows(ob_ref[...].astype(F32), gb_ref[...]).astype(BF16)
    nc = _rms_rows(oc_ref[...].astype(F32), gc_ref[...]).astype(BF16)
    wa = A_Q
    wb = wa + MLA_HEADS * MLA_V
    y = _dot(na, w_ref[0:wa, :]) + _dot(nb, w_ref[wa:wb, :]) + _dot(nc, w_ref[wb:, :])
    x1 = x_ref[...] + gate_ref[...] * y
    x1_ref[...] = x1
    h2 = _rms_rows(x1, g2_ref[...]) * (1.0 + sc_ref[...]) + sh_ref[...]
    hi = h2.astype(BF16)
    lo = (h2 - hi.astype(F32)).astype(BF16)
    _store_slabs(h2_ref, h2)
    lg = _dot(hi, wr_hi_ref[...]) + _dot(lo, wr_hi_ref[...]) + _dot(hi, wr_lo_ref[...]) + br_ref[...]
    lg_ref[...] = lg.T


def _outproj(geom, x, oa, ob, oc, ga, gb, gc, w_out, mods, g2, wr_hi, wr_lo, br):
    tm = geom.tile(256)
    tok = lambda w: pl.BlockSpec((tm, w), lambda i: (i, 0))
    const = lambda shape: pl.BlockSpec(shape, lambda i: (0, 0))
    wb = MLA_HEADS * MLA_V
    return pl.pallas_call(
        _outproj_kernel,
        grid=(geom.t // tm,),
        in_specs=[tok(D_MODEL), tok(A_Q), tok(wb), tok(CONV_CH),
                  const((1, A_Q)), const((1, wb)), const((1, CONV_CH)),
                  const((D_MODEL, D_MODEL)),
                  _mod_spec(geom, tm, 2), _mod_spec(geom, tm, 3), _mod_spec(geom, tm, 4),
                  const((1, D_MODEL)), const((D_MODEL, LANES)), const((D_MODEL, LANES)), const((1, LANES))],
        out_specs=[tok(D_MODEL), pl.BlockSpec((tm * SLAB, LANES), lambda i: (i, 0)),
                   pl.BlockSpec((LANES, tm), lambda i: (0, i))],
        out_shape=[jax.ShapeDtypeStruct((geom.t, D_MODEL), F32),
                   jax.ShapeDtypeStruct((geom.t * SLAB, LANES), F32),
                   jax.ShapeDtypeStruct((LANES, geom.t), F32)],
        compiler_params=_cparams(("parallel",)),
        name="merge_outproj_norm2",
    )(x, oa, ob, oc, ga, gb, gc, w_out, mods, mods, mods, g2, wr_hi, wr_lo, br)


ROUTE_ROWS = 40


def _route_kernel(lg_ref, sel_ref, rt_ref, cnt_ref):
    lg = lg_ref[...]
    row = lax.broadcasted_iota(jnp.int32, lg.shape, 0)
    rowf = row.astype(F32)
    big = float(2 * LANES)
    is_g = jnp.logical_and(row >= N_EXPERTS, row < N_EXPERTS + N_GROUPS)
    gl = jnp.where(is_g, lg, NEG_INF)
    gmax = jnp.max(gl, axis=0, keepdims=True)
    gsum = jnp.sum(jnp.exp(gl - gmax), axis=0, keepdims=True)
    g_val = 1.0 / gsum
    g_idx = jnp.min(jnp.where(jnp.logical_and(is_g, gl == gmax), rowf, big), axis=0, keepdims=True) - N_EXPERTS
    lo = g_idx * EXPERTS_PER_GROUP
    is_e = jnp.logical_and(rowf >= lo, rowf < lo + EXPERTS_PER_GROUP)
    el = jnp.where(is_e, lg, NEG_INF)
    emax = jnp.max(el, axis=0, keepdims=True)
    ee = jnp.exp(el - emax)
    e_prob = ee / jnp.sum(ee, axis=0, keepdims=True)
    p1 = jnp.where(is_e, e_prob, -1.0)
    v1 = jnp.max(p1, axis=0, keepdims=True)
    i1 = jnp.min(jnp.where(p1 == v1, rowf, big), axis=0, keepdims=True)
    p2 = jnp.where(rowf == i1, -1.0, p1)
    v2 = jnp.max(p2, axis=0, keepdims=True)
    i2 = jnp.min(jnp.where(p2 == v2, rowf, big), axis=0, keepdims=True)
    scale = g_val / (v1 + v2)
    r8 = lax.broadcasted_iota(jnp.int32, rt_ref.shape, 0)
    rt_ref[...] = (jnp.where(r8 == 0, i1, 0.0) + jnp.where(r8 == 1, i2, 0.0) +
                   jnp.where(r8 == 2, v1 * scale, 0.0) + jnp.where(r8 == 3, v2 * scale, 0.0))
    chosen = jnp.where(jnp.logical_or(rowf == i1, rowf == i2), 1.0, 0.0).astype(BF16)
    cnt_ref[...] = _dot(chosen, sel_ref[...])


def _route(geom, tr, logits_t, sel):
    n = geom.t // tr
    return pl.pallas_call(
        _route_kernel,
        grid=(n,),
        in_specs=[pl.BlockSpec((ROUTE_ROWS, tr), lambda i: (0, i)), pl.BlockSpec(sel.shape, lambda i: (0, 0))],
        out_specs=[pl.BlockSpec((SUBLANES, tr), lambda i: (0, i)),
                   pl.BlockSpec((None, ROUTE_ROWS, LANES), lambda i: (i, 0, 0))],
        out_shape=[jax.ShapeDtypeStruct((SUBLANES, geom.t), F32), jax.ShapeDtypeStruct((n, ROUTE_ROWS, LANES), F32)],
        compiler_params=_cparams(("parallel",)),
        name="moe_route",
    )(logits_t, sel)


def _slots_kernel(tm, rt_ref, st_ref, tri_ref, sl_ref):
    rt = rt_ref[...]
    rowf = lax.broadcasted_iota(jnp.int32, (ROUTE_ROWS, rt.shape[1]), 0).astype(F32)
    oh1 = rowf == rt[0:1, :]
    oh2 = rowf == rt[1:2, :]
    chosen = jnp.where(jnp.logical_or(oh1, oh2), 1.0, 0.0).astype(BF16)
    for j in range(rt.shape[1] // tm):
        cols = slice(j * tm, (j + 1) * tm)
        first = _dot(chosen[:, cols], tri_ref[...]) + st_ref[:, j:j + 1]
        s1 = jnp.sum(jnp.where(oh1[:, cols], first, 0.0), axis=0, keepdims=True)
        s2 = jnp.sum(jnp.where(oh2[:, cols], first, 0.0), axis=0, keepdims=True)
        sl_ref[j] = jnp.concatenate([s1, s2], axis=1).astype(jnp.int32)


def _slots(geom, tr, tm, rt_t, start3, tri):
    ns = tr // tm
    return pl.pallas_call(
        functools.partial(_slots_kernel, tm),
        grid=(geom.t // tr,),
        in_specs=[pl.BlockSpec((SUBLANES, tr), lambda i: (0, i)),
                  pl.BlockSpec((None, ROUTE_ROWS, LANES), lambda i: (i, 0, 0)),
                  pl.BlockSpec((tm, tm), lambda i: (0, 0))],
        out_specs=pl.BlockSpec((ns, 1, 2 * tm), lambda i: (i, 0, 0)),
        out_shape=jax.ShapeDtypeStruct((geom.t // tm, 1, 2 * tm), jnp.int32),
        compiler_params=_cparams(("parallel",)),
        name="moe_slots",
    )(rt_t, start3, tri)


def _plan(cnt3, ns, n_tiles_max):
    n_steps = cnt3.shape[0]
    cnt_t = cnt3[:, :, :ns].transpose(0, 2, 1).reshape(n_steps * ns, ROUTE_ROWS)
    incl = jnp.cumsum(cnt_t, axis=0)
    tiles_e = jnp.ceil(incl[-1] / TME)
    end_e = jnp.cumsum(tiles_e)
    start = incl - cnt_t + ((end_e - tiles_e) * TME)[None, :]
    start3 = jnp.pad(start.reshape(n_steps, ns, ROUTE_ROWS).transpose(0, 2, 1), ((0, 0), (0, 0), (0, LANES - ns)))
    n_used = end_e[-1].astype(jnp.int32)
    tile_idx = jnp.minimum(jnp.arange(n_tiles_max, dtype=jnp.int32), n_used - 1)
    tile_exp = jnp.sum(tile_idx[:, None] >= end_e[None, :N_EXPERTS].astype(jnp.int32), axis=1).astype(jnp.int32)
    return start3, tile_exp, tile_idx, n_used.reshape(1)


def _scatter_kernel(sl_ref, h_ref, xs_in, xs_out, sem):
    del xs_in
    tm = h_ref.shape[0] // SLAB

    def body(rr, carry):
        for j in range(ROW_UNROLL):
            r = rr * ROW_UNROLL + j
            pltpu.make_async_copy(_slab_row(h_ref, r), _slab_row(xs_out, sl_ref[0, r]), sem).start(priority=0)
            pltpu.make_async_copy(_slab_row(h_ref, r), _slab_row(xs_out, sl_ref[0, tm + r]), sem).start(priority=1)
        return carry

    lax.fori_loop(0, tm // ROW_UNROLL, body, 0)
    for _ in range(2):
        pltpu.make_async_copy(h_ref, xs_out.at[pl.ds(0, tm * SLAB)], sem).wait()


def _scatter(geom, tm, slots_smem, h2, xs):
    return pl.pallas_call(
        _scatter_kernel,
        grid=(geom.t // tm,),
        in_specs=[pl.BlockSpec((None, 1, 2 * tm), lambda i: (i, 0, 0), memory_space=pltpu.SMEM),
                  pl.BlockSpec((tm * SLAB, LANES), lambda i: (i, 0)),
                  pl.BlockSpec(memory_space=pl.ANY)],
        out_specs=pl.BlockSpec(memory_space=pl.ANY),
        out_shape=jax.ShapeDtypeStruct(xs.shape, xs.dtype),
        scratch_shapes=[pltpu.SemaphoreType.DMA(())],
        input_output_aliases={2: 0},
        compiler_params=_cparams(("arbitrary",)),
        name="moe_scatter",
    )(slots_smem, h2, xs)


def _experts_kernel(te_ref, ti_ref, nu_ref, xs_ref, wg_ref, wu_ref, wd_ref, ys_ref):
    del te_ref, ti_ref

    @pl.when(pl.program_id(0) < nu_ref[0])
    def _():
        x = _load_slabs(xs_ref, TME).astype(BF16)
        act = (_silu(_dot(x, wg_ref[...])) * _dot(x, wu_ref[...])).astype(BF16)
        _store_slabs(ys_ref, _dot(act, wd_ref[...]))


def _experts(xs, tile_exp, tile_idx, n_used, wg, wu, wd):
    n_tiles_max = xs.shape[0] // (TME * SLAB)
    slab = pl.BlockSpec((TME * SLAB, LANES), lambda n, te, ti, nu: (ti[n], 0))
    wspec = lambda a, b: pl.BlockSpec((None, a, b), lambda n, te, ti, nu: (te[n], 0, 0))
    return pl.pallas_call(
        _experts_kernel,
        grid_spec=pltpu.PrefetchScalarGridSpec(
            num_scalar_prefetch=3,
            grid=(n_tiles_max,),
            in_specs=[slab, wspec(D_MODEL, D_EXPERT), wspec(D_MODEL, D_EXPERT), wspec(D_EXPERT, D_MODEL)],
            out_specs=slab),
        out_shape=jax.ShapeDtypeStruct(xs.shape, F32),
        compiler_params=_cparams(("arbitrary",)),
        name="moe_experts",
    )(tile_exp, tile_idx, n_used, xs, wg, wu, wd)


def _combine_kernel(geom, final, sl_cur, sl_nxt, rt_ref, x1_ref, gate_ref, ys_hbm, *refs):
    buf1, buf2, sem = refs[-3:]
    i = pl.program_id(0)
    tm = x1_ref.shape[0]

    def issue_row(sl_ref, slot, r):
        dst = slot * tm + r
        pltpu.make_async_copy(_slab_row(ys_hbm, sl_ref[0, r]), _slab_row(buf1, dst), sem.at[slot]).start(priority=0)
        pltpu.make_async_copy(_slab_row(ys_hbm, sl_ref[0, tm + r]), _slab_row(buf2, dst),
                              sem.at[slot]).start(priority=1)

    def wait_slot(slot):
        base = pl.multiple_of(slot * (tm * SLAB), tm * SLAB)
        for buf in (buf1, buf2):
            pltpu.make_async_copy(ys_hbm.at[pl.ds(0, tm * SLAB)], buf.at[pl.ds(base, tm * SLAB)],
                                  sem.at[slot]).wait()
        return base

    @pl.when(i == 0)
    def _():
        def body(rr, carry):
            for j in range(ROW_UNROLL):
                issue_row(sl_cur, 0, rr * ROW_UNROLL + j)
            return carry
        lax.fori_loop(0, tm // ROW_UNROLL, body, 0)

    slot = i % 2
    base = wait_slot(slot)
    for r in range(tm):
        issue_row(sl_nxt, 1 - slot, r)
    w = jnp.concatenate([rt_ref[...], jnp.zeros((LANES - SUBLANES, tm), F32)], axis=0).T
    y = w[:, 2:3] * _load_slabs(buf1, tm, base) + w[:, 3:4] * _load_slabs(buf2, tm, base)
    x_new = x1_ref[...] + gate_ref[...] * y
    if final:
        out_p, out_s = refs[:2]
        n_prompt = geom.tp // tm

        @pl.when(i < n_prompt)
        def _():
            out_p[...] = x_new

        @pl.when(i >= n_prompt)
        def _():
            out_s[...] = x_new
    else:
        sh_ref, sc_ref, g_ref, w_ref, o_ref = refs[:5]
        o_ref[...] = x_new
        _norm1_project(x_new, sh_ref, sc_ref, g_ref, w_ref, refs[5:-3])

    @pl.when(i == pl.num_programs(0) - 1)
    def _():
        wait_slot(1 - slot)


def _combine(geom, tm, slots_smem, rt, x1, mods, ys, nxt=None):
    n = geom.t // tm
    n_prompt = geom.tp // tm
    tok = lambda w: pl.BlockSpec((tm, w), lambda i: (i, 0))
    smem = lambda f: pl.BlockSpec((None, 1, 2 * tm), f, memory_space=pltpu.SMEM)
    args = [slots_smem, slots_smem, rt, x1, mods, ys]
    in_specs = [smem(lambda i: (i, 0, 0)), smem(lambda i: (jnp.minimum(i + 1, n - 1), 0, 0)),
                pl.BlockSpec((SUBLANES, tm), lambda i: (0, i)), tok(D_MODEL), _mod_spec(geom, tm, 5),
                pl.BlockSpec(memory_space=pl.ANY)]
    if nxt is None:
        out_specs = [pl.BlockSpec((tm, D_MODEL), lambda i: (jnp.minimum(i, n_prompt - 1), 0)),
                     pl.BlockSpec((tm, D_MODEL), lambda i: (jnp.maximum(i - n_prompt, 0), 0))]
        out_shape = [jax.ShapeDtypeStruct((geom.tp, D_MODEL), F32),
                     jax.ShapeDtypeStruct((geom.t - geom.tp, D_MODEL), F32)]
    else:
        mods_n, g1_n, w_in_n = nxt
        args += [mods_n, mods_n, g1_n, w_in_n]
        in_specs += [_mod_spec(geom, tm, 0), _mod_spec(geom, tm, 1), pl.BlockSpec((1, D_MODEL), lambda i: (0, 0)),
                     pl.BlockSpec(w_in_n.shape, lambda i: (0, 0))]
        out_specs = [tok(D_MODEL)] + [tok(w) for _, w in _IN_SEGS]
        out_shape = [jax.ShapeDtypeStruct((geom.t, D_MODEL), F32)] + \
            [jax.ShapeDtypeStruct((geom.t, w), BF16) for _, w in _IN_SEGS]
    return pl.pallas_call(
        functools.partial(_combine_kernel, geom, nxt is None),
        grid=(n,),
        in_specs=in_specs,
        out_specs=out_specs,
        out_shape=out_shape,
        scratch_shapes=[pltpu.VMEM((2 * tm * SLAB, LANES), F32), pltpu.VMEM((2 * tm * SLAB, LANES), F32),
                        pltpu.SemaphoreType.DMA((2,))],
        compiler_params=_cparams(("arbitrary",)),
        name="moe_combine",
    )(*args)


def _block_diag_ones(n, blk):
    idx = np.arange(n) // blk
    return jnp.asarray((idx[:, None] == idx[None, :]).astype(np.float32), dtype=BF16)


def _rope_tables(smax):
    inv_freq = 1.0 / (ROPE_BASE ** (jnp.arange(0, MLA_ROPE, 2, dtype=F32) / MLA_ROPE))
    ang = jnp.arange(smax, dtype=F32)[:, None] * inv_freq[None, :]
    cos, sin = jnp.cos(ang), jnp.sin(ang)
    zeros = lambda w: jnp.zeros((smax, w), F32)
    pad = LANES - MLA_QK
    c = jnp.concatenate([jnp.ones((smax, MLA_NOPE), F32), cos, cos, zeros(pad)], axis=1)
    s = jnp.concatenate([zeros(MLA_NOPE), -sin, sin, zeros(pad)], axis=1)
    return c, s


def _pad_cols(w, n):
    return jnp.pad(w, ((0, 0), (0, n - w.shape[1])))


def _layer_weights(l, w_in, mla_w_q_up, mla_w_kv_up, mla_q_gain, mla_k_gain, swa_q_gain, swa_k_gain, swa_sink,
                   moe_w_group, moe_b_group, moe_w_expert, moe_b_expert):
    wi = w_in[l]
    kr = jnp.pad(wi[:, A_IN + MLA_Q_RANK + MLA_KV_RANK:A_IN + B_IN], ((0, 0), (MLA_NOPE, LANES - MLA_QK)))
    w_in_p = jnp.concatenate([wi[:, :A_IN], wi[:, A_IN:A_IN + MLA_Q_RANK + MLA_KV_RANK], kr, wi[:, A_IN + B_IN:]],
                             axis=1).astype(BF16)
    wq = mla_w_q_up[l].reshape(MLA_Q_RANK, MLA_HEADS, MLA_QK)
    wq_p = jnp.pad(wq, ((0, 0), (0, 0), (0, LANES - MLA_QK)))
    wq2 = jnp.concatenate([wq_p, _swap_rope_halves(wq_p)], axis=2).reshape(MLA_Q_RANK, MLA_HEADS * 2 * LANES).astype(BF16)
    wvt = mla_w_kv_up[l].reshape(MLA_KV_RANK, MLA_HEADS, MLA_NOPE + MLA_V)[:, :, MLA_NOPE:].transpose(1, 2, 0).astype(BF16)
    gq_p = _pad_cols(mla_q_gain[l][None, :], LANES)
    gk_p = _pad_cols(mla_k_gain[l][None, :], LANES)
    gq_s, gk_s = _swap_rope_halves(gq_p), _swap_rope_halves(gk_p)
    gq_a = jnp.tile(swa_q_gain[l], SWA_HEADS)[None, :]
    gk_a = jnp.tile(swa_k_gain[l], SWA_KV_HEADS)[None, :]
    sink = jnp.repeat(swa_sink[l] * LOG2E, BLOCK).reshape(SWA_KV_HEADS, 1, SWA_GROUP * BLOCK)
    wr = _pad_cols(jnp.concatenate([moe_w_expert[l], moe_w_group[l]], axis=1), LANES)
    wr_hi = wr.astype(BF16)
    wr_lo = (wr - wr_hi.astype(F32)).astype(BF16)
    br = _pad_cols(jnp.concatenate([moe_b_expert[l], moe_b_group[l]])[None, :], LANES)
    return w_in_p, wq2, wvt, gq_p, gq_s, gk_p, gk_s, gq_a, gk_a, sink, wr_hi, wr_lo, br


def kernel(x_prompt, x_sample, c_prompt, c_sample, w_ada, b_ada, norm1_g, norm2_g, w_in, swa_q_gain, swa_k_gain, swa_sink, mla_q_norm_g, mla_w_q_up, mla_kv_norm_g, mla_w_kv_up, mla_q_gain, mla_k_gain, conv_dw_w, conv_dw_b, conv_gn_g, conv_gn_b, conv_w_pw2, conv_b_pw2, out_norm_a, out_norm_b, out_norm_c, w_out, moe_w_group, moe_b_group, moe_w_expert, moe_b_expert, moe_w_gate, moe_w_up, moe_w_down):
    bp, sp, d = x_prompt.shape
    bs, ss, _ = x_sample.shape
    assert d == D_MODEL
    geom = _Geom(bp, sp, bs, ss)
    depth = w_ada.shape[0]

    x = jnp.concatenate([x_prompt.reshape(bp * sp, d), x_sample.reshape(bs * ss, d)], axis=0)
    c = jnp.concatenate([c_prompt, c_sample], axis=0)
    rows = -(-geom.nb // 8) * 8
    c_pad = jnp.pad(c, ((0, rows - geom.nb), (0, 0)))
    mods_all = _modulation(c_pad, w_ada, b_ada)

    rope_c, rope_s = _rope_tables(max(sp, ss))
    rope_perm = _swap_rope_halves(jnp.eye(LANES, dtype=F32)).astype(BF16)
    ones2 = _block_diag_ones(2 * LANES, LANES)
    bdq = _block_diag_ones(A_Q, HEAD_DIM)
    bdk = _block_diag_ones(A_KV, HEAD_DIM)
    bdc = _block_diag_ones(CONV_CH, CONV_CH // CONV_GROUPS)
    swa_bias = _swa_bias()
    row = lambda v: v[None, :]
    tm_moe = geom.tile(256)
    n_tiles_max = 2 * geom.t // TME + N_EXPERTS
    tr_moe = geom.tile(2048)
    tri = jnp.asarray(np.triu(np.ones((tm_moe, tm_moe), np.float32), 1), dtype=BF16)
    sel = jnp.asarray(np.arange(tr_moe)[:, None] // tm_moe == np.arange(LANES)[None, :], dtype=BF16)
    xs = jnp.zeros((n_tiles_max * TME * SLAB, LANES), F32)

    weights = [_layer_weights(l, w_in, mla_w_q_up, mla_w_kv_up, mla_q_gain, mla_k_gain, swa_q_gain, swa_k_gain,
                              swa_sink, moe_w_group, moe_b_group, moe_w_expert, moe_b_expert) for l in range(depth)]
    mods_l = [mods_all[l, :geom.nb].reshape(geom.nb * N_MOD, 1, D_MODEL) for l in range(depth)]
    proj = _inproj(geom, x, mods_l[0], row(norm1_g[0]), weights[0][0])

    for l in range(depth):
        (w_in_p, wq2, wvt, gq_p, gq_s, gk_p, gk_s, gq_a, gk_a, sink, wr_hi, wr_lo, br) = weights[l]
        mods = mods_l[l]
        q_a, k_a, v_a, cq, ckv, kr, uc = proj
        out_a = _swa(geom, q_a, k_a, v_a, gq_a, gk_a, bdq, bdk, sink, swa_bias)
        tk = geom.tile(512)
        q_b, k_b, vt_b = _mla_prep(geom, tk, cq, ckv, kr, row(mla_q_norm_g[l]), row(mla_kv_norm_g[l]), wq2,
                                   mla_w_kv_up[l].astype(BF16), wvt, rope_perm, ones2, gq_p, gq_s, gk_p, gk_s,
                                   rope_c, rope_s)
        out_b = _mla_attn(geom, tk, q_b, k_b, vt_b)
        dw_w = jnp.pad(conv_dw_w[l].reshape(CONV_WIDTH, CONV_CH), ((0, 32 - CONV_WIDTH), (0, 0)))
        out_c = _conv(geom, uc, dw_w, row(conv_dw_b[l]), row(conv_gn_g[l]), row(conv_gn_b[l]), bdc,
                      conv_w_pw2[l].astype(BF16), row(conv_b_pw2[l]))
        x1, h2, logits = _outproj(geom, x, out_a, out_b, out_c, row(out_norm_a[l]), row(out_norm_b[l]),
                                  row(out_norm_c[l]), w_out[l].astype(BF16), mods, row(norm2_g[l]),
                                  wr_hi, wr_lo, br)
        rt, cnt3 = _route(geom, tr_moe, logits, sel)
        start3, tile_exp, tile_idx, n_used = _plan(cnt3, tr_moe // tm_moe, n_tiles_max)
        slots = _slots(geom, tr_moe, tm_moe, rt, start3, tri)
        xs = _scatter(geom, tm_moe, slots, h2, xs)
        ys = _experts(xs, tile_exp, tile_idx, n_used, moe_w_gate[l].astype(BF16), moe_w_up[l].astype(BF16),
                      moe_w_down[l].astype(BF16))
        if l + 1 < depth:
            x, *proj = _combine(geom, tm_moe, slots, rt, x1, mods, ys,
                                nxt=(mods_l[l + 1], row(norm1_g[l + 1]), weights[l + 1][0]))
        else:
            y_prompt, y_sample = _combine(geom, tm_moe, slots, rt, x1, mods, ys)

    return (y_prompt.reshape(bp, sp, d), y_sample.reshape(bs, ss, d))
```

```python
import functools

import numpy as np
import jax
import jax.numpy as jnp
from jax import lax
from jax.experimental import pallas as pl
from jax.experimental.pallas import tpu as pltpu

F32 = jnp.float32
BF16 = jnp.bfloat16

EPS = 1e-6
NEG_INF = -1e30
LOG2E = 1.4426950408889634

D_MODEL = 1024
HEAD_DIM = 64
SWA_HEADS = 6
SWA_KV_HEADS = 2
SWA_GROUP = SWA_HEADS // SWA_KV_HEADS
WINDOW = 128
BLOCK = WINDOW
MLA_HEADS = 6
MLA_Q_RANK = 256
MLA_KV_RANK = 128
MLA_NOPE = 64
MLA_ROPE = 32
MLA_QK = MLA_NOPE + MLA_ROPE
MLA_V = 64
MLA_VT_ROWS = MLA_V + 16
ROPE_BASE = 10000.0
CONV_CH = 256
CONV_GROUPS = 4
CONV_WIDTH = 31
CONV_PAD = (CONV_WIDTH - 1) // 2
A_Q = SWA_HEADS * HEAD_DIM
A_KV = SWA_KV_HEADS * HEAD_DIM
A_IN = A_Q + 2 * A_KV
B_IN = MLA_Q_RANK + MLA_KV_RANK + MLA_ROPE
N_GROUPS = 4
EXPERTS_PER_GROUP = 8
N_EXPERTS = N_GROUPS * EXPERTS_PER_GROUP
D_EXPERT = 256
N_MOD = 6

LANES = 128
SUBLANES = 8
SLAB = D_MODEL // 128
TME = 512
ROW_UNROLL = 8
HALO = 16
VMEM_LIMIT = 48 * 1024 * 1024


class _Geom:
    def __init__(self, bp, sp, bs, ss):
        self.bp, self.sp, self.bs, self.ss = bp, sp, bs, ss
        self.tp = bp * sp
        self.t = bp * sp + bs * ss
        self.nb = bp + bs

    def tile(self, target):
        t = target
        while self.sp % t or self.ss % t:
            t //= 2
        return t

    def batch(self, i, tm):
        npt = self.tp // tm
        return jnp.where(i < npt, i // (self.sp // tm), self.bp + (i - npt) // (self.ss // tm))

    def pos(self, i, tm):
        npt = self.tp // tm
        return jnp.where(i < npt, i % (self.sp // tm), (i - npt) % (self.ss // tm))

    def is_last(self, i, tm):
        npt = self.tp // tm
        return jnp.where(i < npt, i % (self.sp // tm) == self.sp // tm - 1,
                         (i - npt) % (self.ss // tm) == self.ss // tm - 1)


def _cparams(sem):
    return pltpu.CompilerParams(dimension_semantics=sem, vmem_limit_bytes=VMEM_LIMIT)


def _silu(x):
    return x * jax.nn.sigmoid(x)


def _dot(a, b):
    return jnp.dot(a, b, preferred_element_type=F32)


def _dot_nt(a, b):
    return lax.dot_general(a, b, (((1,), (1,)), ((), ())), preferred_element_type=F32)


def _split_dot(x, w):
    hi = x.astype(BF16)
    lo = (x - hi.astype(F32)).astype(BF16)
    return _dot(hi, w) + _dot(lo, w)


def _mod_kernel(c_ref, w_ref, b_ref, o_ref):
    c = c_ref[...]
    o_ref[...] = _dot(_silu(c).astype(BF16), w_ref[...].astype(BF16)) + b_ref[...]


def _modulation(c_pad, w_ada, b_ada):
    L, d, n = w_ada.shape
    tn = 768
    rows = c_pad.shape[0]
    return pl.pallas_call(
        _mod_kernel,
        grid=(L, n // tn),
        in_specs=[pl.BlockSpec((rows, d), lambda l, j: (0, 0)),
                  pl.BlockSpec((None, d, tn), lambda l, j: (l, 0, j)),
                  pl.BlockSpec((None, 1, tn), lambda l, j: (l, 0, j))],
        out_specs=pl.BlockSpec((None, rows, tn), lambda l, j: (l, 0, j)),
        out_shape=jax.ShapeDtypeStruct((L, rows, n), F32),
        compiler_params=_cparams(("arbitrary", "arbitrary")),
        name="adaln_mod",
    )(c_pad, w_ada, b_ada.reshape(L, 1, n))


_IN_SEGS = (("q", A_Q), ("k", A_KV), ("v", A_KV), ("cq", MLA_Q_RANK), ("ckv", MLA_KV_RANK), ("kr", LANES),
            ("uc", 2 * CONV_CH))


def _norm1_project(x, sh_ref, sc_ref, g_ref, w_ref, out_refs):
    ms = jnp.mean(x * x, axis=-1, keepdims=True)
    h = x * lax.rsqrt(ms + EPS) * g_ref[...]
    h = (h * (1.0 + sc_ref[...]) + sh_ref[...]).astype(BF16)
    off = 0
    for (_, width), o_ref in zip(_IN_SEGS, out_refs):
        o_ref[...] = _dot(h, w_ref[:, off:off + width]).astype(o_ref.dtype)
        off += width


def _inproj_kernel(x_ref, sh_ref, sc_ref, g_ref, w_ref, *out_refs):
    _norm1_project(x_ref[...], sh_ref, sc_ref, g_ref, w_ref, out_refs)


def _mod_spec(geom, tm, k):
    return pl.BlockSpec((None, 1, D_MODEL), lambda i: (geom.batch(i, tm) * N_MOD + k, 0, 0))


def _inproj(geom, x, mods, g1, w_in_p):
    tm = geom.tile(512)
    nw = w_in_p.shape[1]
    return pl.pallas_call(
        _inproj_kernel,
        grid=(geom.t // tm,),
        in_specs=[pl.BlockSpec((tm, D_MODEL), lambda i: (i, 0)),
                  _mod_spec(geom, tm, 0), _mod_spec(geom, tm, 1),
                  pl.BlockSpec((1, D_MODEL), lambda i: (0, 0)),
                  pl.BlockSpec((D_MODEL, nw), lambda i: (0, 0))],
        out_specs=[pl.BlockSpec((tm, w), lambda i: (i, 0)) for _, w in _IN_SEGS],
        out_shape=[jax.ShapeDtypeStruct((geom.t, w), BF16) for _, w in _IN_SEGS],
        compiler_params=_cparams(("parallel",)),
        name="norm1_inproj",
    )(x, mods, mods, g1, w_in_p)


def _swa_kernel(geom, nb, q_ref, kp_ref, kc_ref, kn_ref, vp_ref, vc_ref, vn_ref, gq_ref, gk_ref,
                bdq_ref, bdk_ref, sink_ref, bias_ref, o_ref):
    i = pl.program_id(0)
    first = (geom.pos(i, nb * BLOCK) == 0).astype(jnp.int32)
    last = geom.is_last(i, nb * BLOCK).astype(jnp.int32)

    q = q_ref[...].astype(F32)
    msq = _dot((q * q).astype(BF16), bdq_ref[...]) * (1.0 / HEAD_DIM)
    q_t = (q * lax.rsqrt(msq + EPS) * gq_ref[...] * (HEAD_DIM ** -0.5 * LOG2E)).T.astype(BF16)

    k_all = jnp.concatenate([kp_ref[...], kc_ref[...], kn_ref[...]], axis=0).astype(F32)
    msk = _dot((k_all * k_all).astype(BF16), bdk_ref[...]) * (1.0 / HEAD_DIM)
    k_all = (k_all * lax.rsqrt(msk + EPS) * gk_ref[...]).astype(BF16)
    v_all = jnp.concatenate([vp_ref[...], vc_ref[...], vn_ref[...]], axis=0).astype(F32)
    ones_rows = 16
    v_t = jnp.concatenate([v_all.T, jnp.ones((ones_rows, v_all.shape[0]), F32)], axis=0).astype(BF16)
    zeros = jnp.zeros((HEAD_DIM, SWA_GROUP * BLOCK), BF16)

    pairs = [(b, g) for b in range(nb) for g in range(SWA_KV_HEADS)]
    scores, maxes = {}, {}
    for b, g in pairs:
        cls = (first if b == 0 else 0) + (2 * last if b == nb - 1 else 0)
        heads = range(g * SWA_GROUP, (g + 1) * SWA_GROUP)
        qg = jnp.concatenate([q_t[h * HEAD_DIM:(h + 1) * HEAD_DIM, b * BLOCK:(b + 1) * BLOCK] for h in heads],
                             axis=1)
        q_pad = jnp.concatenate([qg if j == g else zeros for j in range(SWA_KV_HEADS)], axis=0)
        s = _dot(k_all[b * BLOCK:(b + 3) * BLOCK, :], q_pad) + bias_ref[cls, g]
        scores[b, g] = s
        maxes[b, g] = jnp.maximum(jnp.max(s, axis=0, keepdims=True), sink_ref[g])
    probs = {bg: jnp.exp2(scores[bg] - maxes[bg]).astype(BF16) for bg in pairs}
    outs = {(b, g): _dot(v_t[:, b * BLOCK:(b + 3) * BLOCK], probs[b, g]) for b, g in pairs}
    for b in range(nb):
        rows = []
        for g in range(SWA_KV_HEADS):
            o = outs[b, g]
            denom = o[A_KV:A_KV + 1, :] + jnp.exp2(sink_ref[g] - maxes[b, g])
            og = o[g * HEAD_DIM:(g + 1) * HEAD_DIM, :] / denom
            rows += [og[:, j * BLOCK:(j + 1) * BLOCK] for j in range(SWA_GROUP)]
        o_ref[b * BLOCK:(b + 1) * BLOCK, :] = jnp.concatenate(rows, axis=0).T.astype(o_ref.dtype)


def _swa_bias():
    k = np.arange(3 * BLOCK)[:, None]
    q = np.arange(BLOCK)[None, :]
    rel = np.abs(k - BLOCK - q)
    out = np.zeros((4, SWA_KV_HEADS, 3 * BLOCK, SWA_GROUP * BLOCK), np.float32)
    for c in range(4):
        k_lo = BLOCK if c & 1 else 0
        k_hi = 2 * BLOCK if c & 2 else 3 * BLOCK
        valid = (rel <= WINDOW) & (k >= k_lo) & (k < k_hi)
        for g in range(SWA_KV_HEADS):
            for j in range(SWA_GROUP):
                slope = 2.0 ** (-8.0 * (g * SWA_GROUP + j + 1) / SWA_HEADS)
                out[c, g, :, j * BLOCK:(j + 1) * BLOCK] = np.where(valid, -slope * LOG2E * rel, NEG_INF)
    return jnp.asarray(out)


def _swa(geom, q, k, v, gq, gk, bdq, bdk, sink, bias):
    ts = geom.tile(512)
    nb = ts // BLOCK
    nblk = geom.t // BLOCK
    prev = lambda i: (jnp.maximum(i * nb - 1, 0), 0)
    cur = lambda i: (i, 0)
    nxt = lambda i: (jnp.minimum((i + 1) * nb, nblk - 1), 0)
    halo = lambda f: pl.BlockSpec((BLOCK, A_KV), f)
    const = lambda shape: pl.BlockSpec(shape, lambda i: (0,) * len(shape))
    return pl.pallas_call(
        functools.partial(_swa_kernel, geom, nb),
        grid=(geom.t // ts,),
        in_specs=[pl.BlockSpec((ts, A_Q), cur),
                  halo(prev), pl.BlockSpec((ts, A_KV), cur), halo(nxt),
                  halo(prev), pl.BlockSpec((ts, A_KV), cur), halo(nxt),
                  const((1, A_Q)), const((1, A_KV)), const((A_Q, A_Q)), const((A_KV, A_KV)),
                  const((SWA_KV_HEADS, 1, SWA_GROUP * BLOCK)), const(bias.shape)],
        out_specs=pl.BlockSpec((ts, A_Q), cur),
        out_shape=jax.ShapeDtypeStruct((geom.t, A_Q), BF16),
        compiler_params=_cparams(("parallel",)),
        name="swa_attention",
    )(q, k, k, k, v, v, v, gq, gk, bdq, bdk, sink, bias)


def _swap_rope_halves(a):
    lo, hi = MLA_NOPE, MLA_NOPE + MLA_ROPE // 2
    z = jnp.zeros_like(a)
    return jnp.concatenate([z[..., :lo], a[..., hi:MLA_QK], a[..., lo:hi], z[..., MLA_QK:]], axis=-1)


def _mla_prep_kernel(cq_ref, ckv_ref, kr_ref, gqn_ref, gkvn_ref, wq_ref, wkv_ref, wvt_ref, perm_ref, ones_ref,
                     gq_ref, gqs_ref, gk_ref, gks_ref, c_ref, s_ref, q_out, k_out, vt_out):
    cq = cq_ref[...].astype(F32)
    qn = (cq * lax.rsqrt(jnp.mean(cq * cq, axis=-1, keepdims=True) + EPS) * gqn_ref[...]).astype(BF16)
    ckv = ckv_ref[...].astype(F32)
    kvn = (ckv * lax.rsqrt(jnp.mean(ckv * ckv, axis=-1, keepdims=True) + EPS) * gkvn_ref[...]).astype(BF16)
    kr_b = kr_ref[...]
    kr = kr_b.astype(F32)
    lane = lax.broadcasted_iota(jnp.int32, (1, LANES), 1)
    ones = jnp.ones((MLA_VT_ROWS - MLA_V, cq.shape[0]), BF16)
    c = c_ref[...]
    sn = s_ref[...]
    q_c = c * (gq_ref[...] * (MLA_QK ** -0.5 * LOG2E))
    q_s = sn * (gqs_ref[...] * (MLA_QK ** -0.5 * LOG2E))
    k_c = c * gk_ref[...]
    k_rot = _dot(kr_b, perm_ref[...]) * (sn * gks_ref[...])
    inv_d = 1.0 / MLA_QK
    for hp in range(MLA_HEADS // 2):
        kv2 = _dot(kvn, wkv_ref[:, hp * 2 * LANES:(hp + 1) * 2 * LANES])
        for hh in range(2):
            h = 2 * hp + hh
            xq = _dot(qn, wq_ref[:, h * 2 * LANES:(h + 1) * 2 * LANES])
            x, xs = xq[:, :LANES], xq[:, LANES:]
            xk = jnp.where(lane < MLA_NOPE, kv2[:, hh * LANES:(hh + 1) * LANES], 0.0) + kr
            ss = _dot(jnp.concatenate([x * x, xk * xk], axis=1).astype(BF16), ones_ref[...])
            rq = lax.rsqrt(ss[:, :LANES] * inv_d + EPS)
            rk = lax.rsqrt(ss[:, LANES:] * inv_d + EPS)
            q_out[h] = (rq * (x * q_c + xs * q_s)).astype(BF16)
            k_out[h] = (rk * (xk * k_c + k_rot)).astype(BF16)
            vt_out[h, 0] = jnp.concatenate([_dot_nt(wvt_ref[h], kvn).astype(BF16), ones], axis=0)


def _mla_prep(geom, tk, cq, ckv, kr, gqn, gkvn, wq2, wkv, wvt, perm, ones2, gq_p, gq_s, gk_p, gk_s, rope_c, rope_s):
    tm = tk
    tok = lambda w: pl.BlockSpec((tm, w), lambda i: (i, 0))
    const = lambda shape: pl.BlockSpec(shape, lambda i: (0,) * len(shape))
    rope = pl.BlockSpec((tm, LANES), lambda i: (geom.pos(i, tm), 0))
    hm = pl.BlockSpec((MLA_HEADS, tm, LANES), lambda i: (0, i, 0))
    hm_t = pl.BlockSpec((MLA_HEADS, 1, MLA_VT_ROWS, tm), lambda i: (0, i, 0, 0))
    return pl.pallas_call(
        _mla_prep_kernel,
        grid=(geom.t // tm,),
        in_specs=[tok(MLA_Q_RANK), tok(MLA_KV_RANK), tok(LANES),
                  const((1, MLA_Q_RANK)), const((1, MLA_KV_RANK)),
                  const(wq2.shape), const(wkv.shape), const(wvt.shape), const(perm.shape), const(ones2.shape),
                  const((1, LANES)), const((1, LANES)), const((1, LANES)), const((1, LANES)), rope, rope],
        out_specs=[hm, hm, hm_t],
        out_shape=[jax.ShapeDtypeStruct((MLA_HEADS, geom.t, LANES), BF16),
                   jax.ShapeDtypeStruct((MLA_HEADS, geom.t, LANES), BF16),
                   jax.ShapeDtypeStruct((MLA_HEADS, geom.t // tm, MLA_VT_ROWS, tm), BF16)],
        compiler_params=_cparams(("parallel",)),
        name="mla_prep",
    )(cq, ckv, kr, gqn, gkvn, wq2, wkv, wvt, perm, ones2, gq_p, gq_s, gk_p, gk_s, rope_c, rope_s)


def _mla_attn_kernel(nk, q_ref, k_ref, vt_ref, o_ref, s_ref, mc_ref, p_ref, al_ref, acc_ref):
    tk = vt_ref.shape[3]
    tq = q_ref.shape[1]

    def stage_a(chunk, slot):
        off = chunk * tk
        if not isinstance(off, int):
            off = pl.multiple_of(off, tk)
        for hh in range(2):
            s = _dot_nt(k_ref[hh, pl.ds(off, tk), :], q_ref[hh])
            s_ref[slot, hh] = s
            mc_ref[slot, hh] = jnp.max(s, axis=0, keepdims=True)

    def stage_b(slot, m):
        m_out = []
        for hh in range(2):
            m_new = jnp.maximum(m[hh], mc_ref[slot, hh])
            al_ref[slot, hh] = jnp.exp2(m[hh] - m_new)
            p_ref[slot, hh] = jnp.exp2(s_ref[slot, hh] - m_new).astype(BF16)
            m_out.append(m_new)
        return tuple(m_out)

    def stage_c(chunk, slot):
        for hh in range(2):
            acc_ref[hh] = acc_ref[hh] * al_ref[slot, hh] + _dot(vt_ref[hh, chunk], p_ref[slot, hh])

    stage_a(0, 0)
    p_ref[1] = jnp.zeros(p_ref.shape[1:], BF16)
    al_ref[1] = jnp.ones(al_ref.shape[1:], F32)
    acc_ref[...] = jnp.zeros(acc_ref.shape, F32)
    m = (jnp.full((1, tq), NEG_INF, F32),) * 2

    def body(jj, m):
        j = 2 * jj
        stage_a(j + 1, 1)
        m = stage_b(0, m)
        stage_c(jnp.maximum(j - 1, 0), 1)
        stage_a(j + 2, 0)
        m = stage_b(1, m)
        stage_c(j, 0)
        return m

    m = lax.fori_loop(0, nk // 2 - 1, body, m)
    stage_a(nk - 1, 1)
    m = stage_b(0, m)
    stage_c(max(nk - 3, 0), 1)
    m = stage_b(1, m)
    stage_c(nk - 2, 0)
    stage_c(nk - 1, 1)
    o_t = jnp.concatenate([acc_ref[hh][:MLA_V, :] / acc_ref[hh][MLA_V:MLA_V + 1, :] for hh in range(2)], axis=0)
    o_ref[...] = o_t.T.astype(o_ref.dtype)


def _mla_attn_group(geom, tk, q, k, vt, out_prev, nseq, s, tok0):
    tq = min(1024, s)
    nq = s // tq
    nk = s // tk
    assert nk % 2 == 0
    qb0 = tok0 // tq
    sb0 = tok0 // s
    args = [q, k, vt]
    in_specs = [pl.BlockSpec((2, tq, LANES), lambda b, hp, i: (hp, qb0 + b * nq + i, 0)),
                pl.BlockSpec((2, s, LANES), lambda b, hp, i: (hp, sb0 + b, 0)),
                pl.BlockSpec((2, nk, MLA_VT_ROWS, tk), lambda b, hp, i: (hp, sb0 + b, 0, 0))]
    aliases = {}
    kern = functools.partial(_mla_attn_kernel, nk)
    if out_prev is not None:
        args.append(out_prev)
        in_specs.append(pl.BlockSpec(memory_space=pl.ANY))
        aliases = {3: 0}
        kern = lambda q_ref, k_ref, vt_ref, prev_ref, *rest, _k=kern: _k(q_ref, k_ref, vt_ref, *rest)
    return pl.pallas_call(
        kern,
        grid=(nseq, MLA_HEADS // 2, nq),
        in_specs=in_specs,
        out_specs=pl.BlockSpec((tq, LANES), lambda b, hp, i: (qb0 + b * nq + i, hp)),
        out_shape=jax.ShapeDtypeStruct((geom.t, MLA_HEADS * MLA_V), BF16),
        scratch_shapes=[pltpu.VMEM((2, 2, tk, tq), F32), pltpu.VMEM((2, 2, 1, tq), F32),
                        pltpu.VMEM((2, 2, tk, tq), BF16), pltpu.VMEM((2, 2, 1, tq), F32),
                        pltpu.VMEM((2, MLA_VT_ROWS, tq), F32)],
        input_output_aliases=aliases,
        compiler_params=_cparams(("parallel", "parallel", "arbitrary")),
        name="mla_attention",
    )(*args)


def _mla_attn(geom, tk, q, k, vt):
    assert geom.tp % geom.ss == 0
    out = _mla_attn_group(geom, tk, q, k, vt, None, geom.bp, geom.sp, 0)
    return _mla_attn_group(geom, tk, q, k, vt, out, geom.bs, geom.ss, geom.tp)


def _conv_kernel(geom, tc, up_ref, uc_ref, un_ref, w_ref, b_ref, gng_ref, gnb_ref, bd_ref, wpw_ref, bpw_ref,
                 o_ref, hs_ref):
    i = pl.program_id(0)
    first = geom.pos(i, tc) == 0
    last = geom.is_last(i, tc)

    def glu(u):
        u = u.astype(F32)
        return u[:, :CONV_CH] * jax.nn.sigmoid(u[:, CONV_CH:])

    hs_ref[0, 0:HALO, :] = jnp.where(first, 0.0, glu(up_ref[...]))
    hs_ref[0, HALO:HALO + tc, :] = glu(uc_ref[...])
    hs_ref[0, HALO + tc:HALO + tc + HALO, :] = jnp.where(last, 0.0, glu(un_ref[...]))
    span = tc + 2 * HALO - SUBLANES
    for sft in range(1, SUBLANES):
        hs_ref[sft, 0:span, :] = hs_ref[0, sft:sft + span, :]

    rows = 64 if tc % 64 == 0 else tc
    chunks = range(0, tc, rows)
    accs = {r0: jnp.zeros((rows, CONV_CH), F32) + b_ref[...] for r0 in chunks}
    for j in range(CONV_WIDTH):
        for r0 in chunks:
            start = HALO + r0 + j - CONV_PAD
            sft = start % SUBLANES
            accs[r0] = accs[r0] + hs_ref[sft, start - sft:start - sft + rows, :] * w_ref[j:j + 1, :]
    for r0 in chunks:
        acc = accs[r0]
        mu = _split_dot(acc, bd_ref[...]) * (CONV_GROUPS / CONV_CH)
        d = acc - mu
        var = _split_dot(d * d, bd_ref[...]) * (CONV_GROUPS / CONV_CH)
        hn = d * lax.rsqrt(var + EPS) * gng_ref[...] + gnb_ref[...]
        y = _dot(_silu(hn).astype(BF16), wpw_ref[...]) + bpw_ref[...]
        o_ref[r0:r0 + rows, :] = y.astype(o_ref.dtype)


def _conv(geom, uc, dw_w, dw_b, gn_g, gn_b, bd, w_pw2, b_pw2):
    tc = geom.tile(256)
    hb = tc // HALO
    nh = geom.t // HALO
    const = lambda shape: pl.BlockSpec(shape, lambda i: (0, 0))
    return pl.pallas_call(
        functools.partial(_conv_kernel, geom, tc),
        grid=(geom.t // tc,),
        in_specs=[pl.BlockSpec((HALO, 2 * CONV_CH), lambda i: (jnp.maximum(i * hb - 1, 0), 0)),
                  pl.BlockSpec((tc, 2 * CONV_CH), lambda i: (i, 0)),
                  pl.BlockSpec((HALO, 2 * CONV_CH), lambda i: (jnp.minimum((i + 1) * hb, nh - 1), 0)),
                  const((32, CONV_CH)), const((1, CONV_CH)), const((1, CONV_CH)), const((1, CONV_CH)),
                  const((CONV_CH, CONV_CH)), const((CONV_CH, CONV_CH)), const((1, CONV_CH))],
        out_specs=pl.BlockSpec((tc, CONV_CH), lambda i: (i, 0)),
        out_shape=jax.ShapeDtypeStruct((geom.t, CONV_CH), BF16),
        scratch_shapes=[pltpu.VMEM((SUBLANES, tc + 2 * HALO, CONV_CH), F32)],
        compiler_params=_cparams(("parallel",)),
        name="conformer_conv",
    )(uc, uc, uc, dw_w, dw_b, gn_g, gn_b, bd, w_pw2, b_pw2)


def _rms_rows(x, g):
    return x * lax.rsqrt(jnp.mean(x * x, axis=-1, keepdims=True) + EPS) * g


def _store_slabs(ref, x, base=0):
    rows = x.shape[0]
    for c in range(SLAB):
        ref[pl.ds(base + c, rows, stride=SLAB), :] = x[:, c * LANES:(c + 1) * LANES]


def _load_slabs(ref, rows, base=0):
    return jnp.concatenate([ref[pl.ds(base + c, rows, stride=SLAB), :] for c in range(SLAB)], axis=1)


def _slab_row(ref, r):
    return ref.at[pl.ds(pl.multiple_of(r * SLAB, SLAB), SLAB)]


def _outproj_kernel(x_ref, oa_ref, ob_ref, oc_ref, ga_ref, gb_ref, gc_ref, w_ref, gate_ref, sh_ref, sc_ref,
                    g2_ref, wr_hi_ref, br_ref, x1_ref, h2_ref, lg_ref):
    na = _rms_rows(oa_ref[...].astype(F32), ga_ref[...]).astype(BF16)
    nb = _rms_rows(ob_ref[...].astype(F32), gb_ref[...]).astype(BF16)
    nc = _rms_rows(oc_ref[...].astype(F32), gc_ref[...]).astype(BF16)
    wa = A_Q
    wb = wa + MLA_HEADS * MLA_V
    y = _dot(na, w_ref[0:wa, :]) + _dot(nb, w_ref[wa:wb, :]) + _dot(nc, w_ref[wb:, :])
    x1 = x_ref[...] + gate_ref[...] * y
    x1_ref[...] = x1
    h2 = _rms_rows(x1, g2_ref[...]) * (1.0 + sc_ref[...]) + sh_ref[...]
    hi = h2.astype(BF16)
    lo = (h2 - hi.astype(F32)).astype(BF16)
    _store_slabs(h2_ref, h2)
    lg2 = _dot(hi, wr_hi_ref[...])
    lg = lg2[:, :LANES] + lg2[:, LANES:] + _dot(lo, wr_hi_ref[:, :LANES]) + br_ref[...]
    lg_ref[...] = lg.T


def _outproj(geom, x, oa, ob, oc, ga, gb, gc, w_out, mods, g2, wr2, br):
    tm = geom.tile(256)
    tok = lambda w: pl.BlockSpec((tm, w), lambda i: (i, 0))
    const = lambda shape: pl.BlockSpec(shape, lambda i: (0, 0))
    wb = MLA_HEADS * MLA_V
    return pl.pallas_call(
        _outproj_kernel,
        grid=(geom.t // tm,),
        in_specs=[tok(D_MODEL), tok(A_Q), tok(wb), tok(CONV_CH),
                  const((1, A_Q)), const((1, wb)), const((1, CONV_CH)),
                  const((D_MODEL, D_MODEL)),
                  _mod_spec(geom, tm, 2), _mod_spec(geom, tm, 3), _mod_spec(geom, tm, 4),
                  const((1, D_MODEL)), const((D_MODEL, 2 * LANES)), const((1, LANES))],
        out_specs=[tok(D_MODEL), pl.BlockSpec((tm * SLAB, LANES), lambda i: (i, 0)),
                   pl.BlockSpec((LANES, tm), lambda i: (0, i))],
        out_shape=[jax.ShapeDtypeStruct((geom.t, D_MODEL), F32),
                   jax.ShapeDtypeStruct((geom.t * SLAB, LANES), F32),
                   jax.ShapeDtypeStruct((LANES, geom.t), F32)],
        compiler_params=_cparams(("parallel",)),
        name="merge_outproj_norm2",
    )(x, oa, ob, oc, ga, gb, gc, w_out, mods, mods, mods, g2, wr2, br)


ROUTE_ROWS = 40


def _route_kernel(lg_ref, sel_ref, rt_ref, cnt_ref):
    lg = lg_ref[...]
    row = lax.broadcasted_iota(jnp.int32, lg.shape, 0)
    rowf = row.astype(F32)
    big = float(2 * LANES)
    is_g = jnp.logical_and(row >= N_EXPERTS, row < N_EXPERTS + N_GROUPS)
    gl = jnp.where(is_g, lg, NEG_INF)
    gmax = jnp.max(gl, axis=0, keepdims=True)
    gsum = jnp.sum(jnp.exp(gl - gmax), axis=0, keepdims=True)
    g_val = 1.0 / gsum
    g_idx = jnp.min(jnp.where(jnp.logical_and(is_g, gl == gmax), rowf, big), axis=0, keepdims=True) - N_EXPERTS
    lo = g_idx * EXPERTS_PER_GROUP
    is_e = jnp.logical_and(rowf >= lo, rowf < lo + EXPERTS_PER_GROUP)
    el = jnp.where(is_e, lg, NEG_INF)
    emax = jnp.max(el, axis=0, keepdims=True)
    ee = jnp.exp(el - emax)
    e_prob = ee / jnp.sum(ee, axis=0, keepdims=True)
    p1 = jnp.where(is_e, e_prob, -1.0)
    v1 = jnp.max(p1, axis=0, keepdims=True)
    i1 = jnp.min(jnp.where(p1 == v1, rowf, big), axis=0, keepdims=True)
    p2 = jnp.where(rowf == i1, -1.0, p1)
    v2 = jnp.max(p2, axis=0, keepdims=True)
    i2 = jnp.min(jnp.where(p2 == v2, rowf, big), axis=0, keepdims=True)
    scale = g_val / (v1 + v2)
    r8 = lax.broadcasted_iota(jnp.int32, rt_ref.shape, 0)
    rt_ref[...] = (jnp.where(r8 == 0, i1, 0.0) + jnp.where(r8 == 1, i2, 0.0) +
                   jnp.where(r8 == 2, v1 * scale, 0.0) + jnp.where(r8 == 3, v2 * scale, 0.0))
    chosen = jnp.where(jnp.logical_or(rowf == i1, rowf == i2), 1.0, 0.0).astype(BF16)
    cnt_ref[...] = _dot(chosen, sel_ref[...])


def _route(geom, tr, logits_t, sel):
    n = geom.t // tr
    return pl.pallas_call(
        _route_kernel,
        grid=(n,),
        in_specs=[pl.BlockSpec((ROUTE_ROWS, tr), lambda i: (0, i)), pl.BlockSpec(sel.shape, lambda i: (0, 0))],
        out_specs=[pl.BlockSpec((SUBLANES, tr), lambda i: (0, i)),
                   pl.BlockSpec((None, ROUTE_ROWS, LANES), lambda i: (i, 0, 0))],
        out_shape=[jax.ShapeDtypeStruct((SUBLANES, geom.t), F32), jax.ShapeDtypeStruct((n, ROUTE_ROWS, LANES), F32)],
        compiler_params=_cparams(("parallel",)),
        name="moe_route",
    )(logits_t, sel)


def _slots_kernel(tm, rt_ref, st_ref, tri_ref, sl_ref):
    rt = rt_ref[...]
    rowf = lax.broadcasted_iota(jnp.int32, (ROUTE_ROWS, rt.shape[1]), 0).astype(F32)
    oh1 = rowf == rt[0:1, :]
    oh2 = rowf == rt[1:2, :]
    chosen = jnp.where(jnp.logical_or(oh1, oh2), 1.0, 0.0).astype(BF16)
    for j in range(rt.shape[1] // tm):
        cols = slice(j * tm, (j + 1) * tm)
        first = _dot(chosen[:, cols], tri_ref[...]) + st_ref[:, j:j + 1]
        s1 = jnp.sum(jnp.where(oh1[:, cols], first, 0.0), axis=0, keepdims=True)
        s2 = jnp.sum(jnp.where(oh2[:, cols], first, 0.0), axis=0, keepdims=True)
        sl_ref[j] = jnp.concatenate([s1, s2], axis=1).astype(jnp.int32)


def _slots(geom, tr, tm, rt_t, start3, tri):
    ns = tr // tm
    return pl.pallas_call(
        functools.partial(_slots_kernel, tm),
        grid=(geom.t // tr,),
        in_specs=[pl.BlockSpec((SUBLANES, tr), lambda i: (0, i)),
                  pl.BlockSpec((None, ROUTE_ROWS, LANES), lambda i: (i, 0, 0)),
                  pl.BlockSpec((tm, tm), lambda i: (0, 0))],
        out_specs=pl.BlockSpec((ns, 1, 2 * tm), lambda i: (i, 0, 0)),
        out_shape=jax.ShapeDtypeStruct((geom.t // tm, 1, 2 * tm), jnp.int32),
        compiler_params=_cparams(("parallel",)),
        name="moe_slots",
    )(rt_t, start3, tri)


def _plan(cnt3, ns, n_tiles_max):
    n_steps = cnt3.shape[0]
    cnt_t = cnt3[:, :, :ns].transpose(0, 2, 1).reshape(n_steps * ns, ROUTE_ROWS)
    incl = jnp.cumsum(cnt_t, axis=0)
    tiles_e = jnp.ceil(incl[-1] / TME)
    end_e = jnp.cumsum(tiles_e)
    start = incl - cnt_t + ((end_e - tiles_e) * TME)[None, :]
    start3 = jnp.pad(start.reshape(n_steps, ns, ROUTE_ROWS).transpose(0, 2, 1), ((0, 0), (0, 0), (0, LANES - ns)))
    n_used = end_e[-1].astype(jnp.int32)
    tile_idx = jnp.minimum(jnp.arange(n_tiles_max, dtype=jnp.int32), n_used - 1)
    tile_exp = jnp.sum(tile_idx[:, None] >= end_e[None, :N_EXPERTS].astype(jnp.int32), axis=1).astype(jnp.int32)
    return start3, tile_exp, tile_idx, n_used.reshape(1)


def _scatter_kernel(sl_ref, h_ref, xs_in, xs_out, sem):
    del xs_in
    tm = h_ref.shape[0] // SLAB

    def body(rr, carry):
        for j in range(ROW_UNROLL):
            r = rr * ROW_UNROLL + j
            pltpu.make_async_copy(_slab_row(h_ref, r), _slab_row(xs_out, sl_ref[0, r]), sem).start(priority=0)
            pltpu.make_async_copy(_slab_row(h_ref, r), _slab_row(xs_out, sl_ref[0, tm + r]), sem).start(priority=1)
        return carry

    lax.fori_loop(0, tm // ROW_UNROLL, body, 0)
    for _ in range(2):
        pltpu.make_async_copy(h_ref, xs_out.at[pl.ds(0, tm * SLAB)], sem).wait()


def _scatter(geom, tm, slots_smem, h2, xs):
    return pl.pallas_call(
        _scatter_kernel,
        grid=(geom.t // tm,),
        in_specs=[pl.BlockSpec((None, 1, 2 * tm), lambda i: (i, 0, 0), memory_space=pltpu.SMEM),
                  pl.BlockSpec((tm * SLAB, LANES), lambda i: (i, 0)),
                  pl.BlockSpec(memory_space=pl.ANY)],
        out_specs=pl.BlockSpec(memory_space=pl.ANY),
        out_shape=jax.ShapeDtypeStruct(xs.shape, xs.dtype),
        scratch_shapes=[pltpu.SemaphoreType.DMA(())],
        input_output_aliases={2: 0},
        compiler_params=_cparams(("arbitrary",)),
        name="moe_scatter",
    )(slots_smem, h2, xs)


def _experts_kernel(te_ref, ti_ref, nu_ref, xs_ref, wg_ref, wu_ref, wd_ref, ys_ref):
    del te_ref, ti_ref

    @pl.when(pl.program_id(0) < nu_ref[0])
    def _():
        x = _load_slabs(xs_ref, TME).astype(BF16)
        act = (_silu(_dot(x, wg_ref[...].astype(BF16))) * _dot(x, wu_ref[...].astype(BF16))).astype(BF16)
        _store_slabs(ys_ref, _dot(act, wd_ref[...].astype(BF16)))


def _experts(layer, xs, tile_exp, tile_idx, n_used, wg, wu, wd):
    n_tiles_max = xs.shape[0] // (TME * SLAB)
    slab = pl.BlockSpec((TME * SLAB, LANES), lambda n, te, ti, nu: (ti[n], 0))
    wspec = lambda a, b: pl.BlockSpec((None, None, a, b), lambda n, te, ti, nu: (layer, te[n], 0, 0))
    return pl.pallas_call(
        _experts_kernel,
        grid_spec=pltpu.PrefetchScalarGridSpec(
            num_scalar_prefetch=3,
            grid=(n_tiles_max,),
            in_specs=[slab, wspec(D_MODEL, D_EXPERT), wspec(D_MODEL, D_EXPERT), wspec(D_EXPERT, D_MODEL)],
            out_specs=slab),
        out_shape=jax.ShapeDtypeStruct(xs.shape, F32),
        compiler_params=_cparams(("arbitrary",)),
        name="moe_experts",
    )(tile_exp, tile_idx, n_used, xs, wg, wu, wd)


def _combine_kernel(geom, final, sl_cur, sl_nxt, rt_ref, x1_ref, gate_ref, ys_hbm, *refs):
    buf1, buf2, sem = refs[-3:]
    i = pl.program_id(0)
    tm = x1_ref.shape[0]

    def issue_row(sl_ref, slot, r):
        dst = slot * tm + r
        pltpu.make_async_copy(_slab_row(ys_hbm, sl_ref[0, r]), _slab_row(buf1, dst), sem.at[slot]).start(priority=0)
        pltpu.make_async_copy(_slab_row(ys_hbm, sl_ref[0, tm + r]), _slab_row(buf2, dst),
                              sem.at[slot]).start(priority=1)

    def wait_slot(slot):
        base = pl.multiple_of(slot * (tm * SLAB), tm * SLAB)
        for buf in (buf1, buf2):
            pltpu.make_async_copy(ys_hbm.at[pl.ds(0, tm * SLAB)], buf.at[pl.ds(base, tm * SLAB)],
                                  sem.at[slot]).wait()
        return base

    @pl.when(i == 0)
    def _():
        def body(rr, carry):
            for j in range(ROW_UNROLL):
                issue_row(sl_cur, 0, rr * ROW_UNROLL + j)
            return carry
        lax.fori_loop(0, tm // ROW_UNROLL, body, 0)

    slot = i % 2
    base = wait_slot(slot)
    for r in range(tm):
        issue_row(sl_nxt, 1 - slot, r)
    w = jnp.concatenate([rt_ref[...], jnp.zeros((LANES - SUBLANES, tm), F32)], axis=0).T
    y = w[:, 2:3] * _load_slabs(buf1, tm, base) + w[:, 3:4] * _load_slabs(buf2, tm, base)
    x_new = x1_ref[...] + gate_ref[...] * y
    if final:
        out_p, out_s = refs[:2]
        n_prompt = geom.tp // tm

        @pl.when(i < n_prompt)
        def _():
            out_p[...] = x_new

        @pl.when(i >= n_prompt)
        def _():
            out_s[...] = x_new
    else:
        sh_ref, sc_ref, g_ref, w_ref, o_ref = refs[:5]
        o_ref[...] = x_new
        _norm1_project(x_new, sh_ref, sc_ref, g_ref, w_ref, refs[5:-3])

    @pl.when(i == pl.num_programs(0) - 1)
    def _():
        wait_slot(1 - slot)


def _combine(geom, tm, slots_smem, rt, x1, mods, ys, nxt=None):
    n = geom.t // tm
    n_prompt = geom.tp // tm
    tok = lambda w: pl.BlockSpec((tm, w), lambda i: (i, 0))
    smem = lambda f: pl.BlockSpec((None, 1, 2 * tm), f, memory_space=pltpu.SMEM)
    args = [slots_smem, slots_smem, rt, x1, mods, ys]
    in_specs = [smem(lambda i: (i, 0, 0)), smem(lambda i: (jnp.minimum(i + 1, n - 1), 0, 0)),
                pl.BlockSpec((SUBLANES, tm), lambda i: (0, i)), tok(D_MODEL), _mod_spec(geom, tm, 5),
                pl.BlockSpec(memory_space=pl.ANY)]
    if nxt is None:
        out_specs = [pl.BlockSpec((tm, D_MODEL), lambda i: (jnp.minimum(i, n_prompt - 1), 0)),
                     pl.BlockSpec((tm, D_MODEL), lambda i: (jnp.maximum(i - n_prompt, 0), 0))]
        out_shape = [jax.ShapeDtypeStruct((geom.tp, D_MODEL), F32),
                     jax.ShapeDtypeStruct((geom.t - geom.tp, D_MODEL), F32)]
    else:
        mods_n, g1_n, w_in_n = nxt
        args += [mods_n, mods_n, g1_n, w_in_n]
        in_specs += [_mod_spec(geom, tm, 0), _mod_spec(geom, tm, 1), pl.BlockSpec((1, D_MODEL), lambda i: (0, 0)),
                     pl.BlockSpec(w_in_n.shape, lambda i: (0, 0))]
        out_specs = [tok(D_MODEL)] + [tok(w) for _, w in _IN_SEGS]
        out_shape = [jax.ShapeDtypeStruct((geom.t, D_MODEL), F32)] + \
            [jax.ShapeDtypeStruct((geom.t, w), BF16) for _, w in _IN_SEGS]
    return pl.pallas_call(
        functools.partial(_combine_kernel, geom, nxt is None),
        grid=(n,),
        in_specs=in_specs,
        out_specs=out_specs,
        out_shape=out_shape,
        scratch_shapes=[pltpu.VMEM((2 * tm * SLAB, LANES), F32), pltpu.VMEM((2 * tm * SLAB, LANES), F32),
                        pltpu.SemaphoreType.DMA((2,))],
        compiler_params=_cparams(("arbitrary",)),
        name="moe_combine",
    )(*args)


def _block_diag_ones(n, blk):
    idx = np.arange(n) // blk
    return jnp.asarray((idx[:, None] == idx[None, :]).astype(np.float32), dtype=BF16)


def _rope_tables(smax):
    inv_freq = 1.0 / (ROPE_BASE ** (jnp.arange(0, MLA_ROPE, 2, dtype=F32) / MLA_ROPE))
    ang = jnp.arange(smax, dtype=F32)[:, None] * inv_freq[None, :]
    cos, sin = jnp.cos(ang), jnp.sin(ang)
    zeros = lambda w: jnp.zeros((smax, w), F32)
    pad = LANES - MLA_QK
    c = jnp.concatenate([jnp.ones((smax, MLA_NOPE), F32), cos, cos, zeros(pad)], axis=1)
    s = jnp.concatenate([zeros(MLA_NOPE), -sin, sin, zeros(pad)], axis=1)
    return c, s


def _pad_cols(w, n):
    return jnp.pad(w, ((0, 0), (0, n - w.shape[1])))


def _layer_weights(l, w_in, mla_w_q_up, mla_w_kv_up, mla_q_gain, mla_k_gain, swa_q_gain, swa_k_gain, swa_sink,
                   moe_w_group, moe_b_group, moe_w_expert, moe_b_expert):
    wi = w_in[l]
    kr = jnp.pad(wi[:, A_IN + MLA_Q_RANK + MLA_KV_RANK:A_IN + B_IN], ((0, 0), (MLA_NOPE, LANES - MLA_QK)))
    w_in_p = jnp.concatenate([wi[:, :A_IN], wi[:, A_IN:A_IN + MLA_Q_RANK + MLA_KV_RANK], kr, wi[:, A_IN + B_IN:]],
                             axis=1).astype(BF16)
    wq = mla_w_q_up[l].reshape(MLA_Q_RANK, MLA_HEADS, MLA_QK)
    wq_p = jnp.pad(wq, ((0, 0), (0, 0), (0, LANES - MLA_QK)))
    wq2 = jnp.concatenate([wq_p, _swap_rope_halves(wq_p)], axis=2).reshape(MLA_Q_RANK, MLA_HEADS * 2 * LANES).astype(BF16)
    wvt = mla_w_kv_up[l].reshape(MLA_KV_RANK, MLA_HEADS, MLA_NOPE + MLA_V)[:, :, MLA_NOPE:].transpose(1, 2, 0).astype(BF16)
    gq_p = _pad_cols(mla_q_gain[l][None, :], LANES)
    gk_p = _pad_cols(mla_k_gain[l][None, :], LANES)
    gq_s, gk_s = _swap_rope_halves(gq_p), _swap_rope_halves(gk_p)
    gq_a = jnp.tile(swa_q_gain[l], SWA_HEADS)[None, :]
    gk_a = jnp.tile(swa_k_gain[l], SWA_KV_HEADS)[None, :]
    sink = jnp.repeat(swa_sink[l] * LOG2E, BLOCK).reshape(SWA_KV_HEADS, 1, SWA_GROUP * BLOCK)
    wr = _pad_cols(jnp.concatenate([moe_w_expert[l], moe_w_group[l]], axis=1), LANES)
    wr_hi = wr.astype(BF16)
    wr2 = jnp.concatenate([wr_hi, (wr - wr_hi.astype(F32)).astype(BF16)], axis=1)
    br = _pad_cols(jnp.concatenate([moe_b_expert[l], moe_b_group[l]])[None, :], LANES)
    return w_in_p, wq2, wvt, gq_p, gq_s, gk_p, gk_s, gq_a, gk_a, sink, wr2, br


def kernel(x_prompt, x_sample, c_prompt, c_sample, w_ada, b_ada, norm1_g, norm2_g, w_in, swa_q_gain, swa_k_gain, swa_sink, mla_q_norm_g, mla_w_q_up, mla_kv_norm_g, mla_w_kv_up, mla_q_gain, mla_k_gain, conv_dw_w, conv_dw_b, conv_gn_g, conv_gn_b, conv_w_pw2, conv_b_pw2, out_norm_a, out_norm_b, out_norm_c, w_out, moe_w_group, moe_b_group, moe_w_expert, moe_b_expert, moe_w_gate, moe_w_up, moe_w_down):
    bp, sp, d = x_prompt.shape
    bs, ss, _ = x_sample.shape
    assert d == D_MODEL
    geom = _Geom(bp, sp, bs, ss)
    depth = w_ada.shape[0]

    x = jnp.concatenate([x_prompt.reshape(bp * sp, d), x_sample.reshape(bs * ss, d)], axis=0)
    c = jnp.concatenate([c_prompt, c_sample], axis=0)
    rows = -(-geom.nb // 8) * 8
    c_pad = jnp.pad(c, ((0, rows - geom.nb), (0, 0)))
    mods_all = _modulation(c_pad, w_ada, b_ada)

    rope_c, rope_s = _rope_tables(max(sp, ss))
    rope_perm = _swap_rope_halves(jnp.eye(LANES, dtype=F32)).astype(BF16)
    ones2 = _block_diag_ones(2 * LANES, LANES)
    bdq = _block_diag_ones(A_Q, HEAD_DIM)
    bdk = _block_diag_ones(A_KV, HEAD_DIM)
    bdc = _block_diag_ones(CONV_CH, CONV_CH // CONV_GROUPS)
    swa_bias = _swa_bias()
    row = lambda v: v[None, :]
    tm_moe = geom.tile(256)
    n_tiles_max = 2 * geom.t // TME + N_EXPERTS
    tr_moe = geom.tile(2048)
    tri = jnp.asarray(np.triu(np.ones((tm_moe, tm_moe), np.float32), 1), dtype=BF16)
    sel = jnp.asarray(np.arange(tr_moe)[:, None] // tm_moe == np.arange(LANES)[None, :], dtype=BF16)
    xs = jnp.zeros((n_tiles_max * TME * SLAB, LANES), F32)

    weights = [_layer_weights(l, w_in, mla_w_q_up, mla_w_kv_up, mla_q_gain, mla_k_gain, swa_q_gain, swa_k_gain,
                              swa_sink, moe_w_group, moe_b_group, moe_w_expert, moe_b_expert) for l in range(depth)]
    mods_l = [mods_all[l, :geom.nb].reshape(geom.nb * N_MOD, 1, D_MODEL) for l in range(depth)]
    proj = _inproj(geom, x, mods_l[0], row(norm1_g[0]), weights[0][0])

    for l in range(depth):
        (w_in_p, wq2, wvt, gq_p, gq_s, gk_p, gk_s, gq_a, gk_a, sink, wr2, br) = weights[l]
        mods = mods_l[l]
        q_a, k_a, v_a, cq, ckv, kr, uc = proj
        out_a = _swa(geom, q_a, k_a, v_a, gq_a, gk_a, bdq, bdk, sink, swa_bias)
        tk = geom.tile(512)
        q_b, k_b, vt_b = _mla_prep(geom, tk, cq, ckv, kr, row(mla_q_norm_g[l]), row(mla_kv_norm_g[l]), wq2,
                                   mla_w_kv_up[l].astype(BF16), wvt, rope_perm, ones2, gq_p, gq_s, gk_p, gk_s,
                                   rope_c, rope_s)
        out_b = _mla_attn(geom, tk, q_b, k_b, vt_b)
        dw_w = jnp.pad(conv_dw_w[l].reshape(CONV_WIDTH, CONV_CH), ((0, 32 - CONV_WIDTH), (0, 0)))
        out_c = _conv(geom, uc, dw_w, row(conv_dw_b[l]), row(conv_gn_g[l]), row(conv_gn_b[l]), bdc,
                      conv_w_pw2[l].astype(BF16), row(conv_b_pw2[l]))
        x1, h2, logits = _outproj(geom, x, out_a, out_b, out_c, row(out_norm_a[l]), row(out_norm_b[l]),
                                  row(out_norm_c[l]), w_out[l].astype(BF16), mods, row(norm2_g[l]),
                                  wr2, br)
        rt, cnt3 = _route(geom, tr_moe, logits, sel)
        start3, tile_exp, tile_idx, n_used = _plan(cnt3, tr_moe // tm_moe, n_tiles_max)
        slots = _slots(geom, tr_moe, tm_moe, rt, start3, tri)
        xs = _scatter(geom, tm_moe, slots, h2, xs)
        ys = _experts(l, xs, tile_exp, tile_idx, n_used, moe_w_gate, moe_w_up, moe_w_down)
        if l + 1 < depth:
            x, *proj = _combine(geom, tm_moe, slots, rt, x1, mods, ys,
                                nxt=(mods_l[l + 1], row(norm1_g[l + 1]), weights[l + 1][0]))
        else:
            y_prompt, y_sample = _combine(geom, tm_moe, slots, rt, x1, mods, ys)

    return (y_prompt.reshape(bp, sp, d), y_sample.reshape(bs, ss, d))
```

```python
import functools

import numpy as np
import jax
import jax.numpy as jnp
from jax import lax
from jax.experimental import pallas as pl
from jax.experimental.pallas import tpu as pltpu

F32 = jnp.float32
BF16 = jnp.bfloat16

EPS = 1e-6
NEG_INF = -1e30
LOG2E = 1.4426950408889634

D_MODEL = 1024
HEAD_DIM = 64
SWA_HEADS = 6
SWA_KV_HEADS = 2
SWA_GROUP = SWA_HEADS // SWA_KV_HEADS
WINDOW = 128
BLOCK = WINDOW
MLA_HEADS = 6
MLA_Q_RANK = 256
MLA_KV_RANK = 128
MLA_NOPE = 64
MLA_ROPE = 32
MLA_QK = MLA_NOPE + MLA_ROPE
MLA_V = 64
MLA_VT_ROWS = MLA_V + 16
ROPE_BASE = 10000.0
CONV_CH = 256
CONV_GROUPS = 4
CONV_WIDTH = 31
CONV_PAD = (CONV_WIDTH - 1) // 2
A_Q = SWA_HEADS * HEAD_DIM
A_KV = SWA_KV_HEADS * HEAD_DIM
A_IN = A_Q + 2 * A_KV
B_IN = MLA_Q_RANK + MLA_KV_RANK + MLA_ROPE
N_GROUPS = 4
EXPERTS_PER_GROUP = 8
N_EXPERTS = N_GROUPS * EXPERTS_PER_GROUP
D_EXPERT = 256
N_MOD = 6

LANES = 128
SUBLANES = 8
SLAB = D_MODEL // (2 * 128)
U32 = jnp.uint32
TME = 512
ROW_UNROLL = 8
HALO = 16
VMEM_LIMIT = 48 * 1024 * 1024


class _Geom:
    def __init__(self, bp, sp, bs, ss):
        self.bp, self.sp, self.bs, self.ss = bp, sp, bs, ss
        self.tp = bp * sp
        self.t = bp * sp + bs * ss
        self.nb = bp + bs

    def tile(self, target):
        t = target
        while self.sp % t or self.ss % t:
            t //= 2
        return t

    def batch(self, i, tm):
        npt = self.tp // tm
        return jnp.where(i < npt, i // (self.sp // tm), self.bp + (i - npt) // (self.ss // tm))

    def pos(self, i, tm):
        npt = self.tp // tm
        return jnp.where(i < npt, i % (self.sp // tm), (i - npt) % (self.ss // tm))

    def is_last(self, i, tm):
        npt = self.tp // tm
        return jnp.where(i < npt, i % (self.sp // tm) == self.sp // tm - 1,
                         (i - npt) % (self.ss // tm) == self.ss // tm - 1)


def _cparams(sem):
    return pltpu.CompilerParams(dimension_semantics=sem, vmem_limit_bytes=VMEM_LIMIT)


def _silu(x):
    return x * jax.nn.sigmoid(x)


def _dot(a, b):
    return jnp.dot(a, b, preferred_element_type=F32)


def _dot_nt(a, b):
    return lax.dot_general(a, b, (((1,), (1,)), ((), ())), preferred_element_type=F32)


def _split_dot(x, w):
    hi = x.astype(BF16)
    lo = (x - hi.astype(F32)).astype(BF16)
    return _dot(hi, w) + _dot(lo, w)


def _mod_kernel(c_ref, w_ref, b_ref, o_ref):
    c = c_ref[...]
    o_ref[...] = _dot(_silu(c).astype(BF16), w_ref[...].astype(BF16)) + b_ref[...]


def _modulation(c_pad, w_ada, b_ada):
    L, d, n = w_ada.shape
    tn = 768
    rows = c_pad.shape[0]
    return pl.pallas_call(
        _mod_kernel,
        grid=(L, n // tn),
        in_specs=[pl.BlockSpec((rows, d), lambda l, j: (0, 0)),
                  pl.BlockSpec((None, d, tn), lambda l, j: (l, 0, j)),
                  pl.BlockSpec((None, 1, tn), lambda l, j: (l, 0, j))],
        out_specs=pl.BlockSpec((None, rows, tn), lambda l, j: (l, 0, j)),
        out_shape=jax.ShapeDtypeStruct((L, rows, n), F32),
        compiler_params=_cparams(("arbitrary", "arbitrary")),
        name="adaln_mod",
    )(c_pad, w_ada, b_ada.reshape(L, 1, n))


_IN_SEGS = (("q", A_Q), ("k", A_KV), ("v", A_KV), ("cq", MLA_Q_RANK), ("ckv", MLA_KV_RANK), ("kr", LANES),
            ("uc", 2 * CONV_CH))


def _norm1_project(x, sh_ref, sc_ref, g_ref, w_ref, out_refs):
    ms = jnp.mean(x * x, axis=-1, keepdims=True)
    h = x * lax.rsqrt(ms + EPS) * g_ref[...]
    h = (h * (1.0 + sc_ref[...]) + sh_ref[...]).astype(BF16)
    off = 0
    for (_, width), o_ref in zip(_IN_SEGS, out_refs):
        o_ref[...] = _dot(h, w_ref[:, off:off + width]).astype(o_ref.dtype)
        off += width


def _inproj_kernel(x_ref, sh_ref, sc_ref, g_ref, w_ref, *out_refs):
    _norm1_project(x_ref[...], sh_ref, sc_ref, g_ref, w_ref, out_refs)


def _mod_spec(geom, tm, k):
    return pl.BlockSpec((None, 1, D_MODEL), lambda i: (geom.batch(i, tm) * N_MOD + k, 0, 0))


def _inproj(geom, x, mods, g1, w_in_p):
    tm = geom.tile(512)
    nw = w_in_p.shape[1]
    return pl.pallas_call(
        _inproj_kernel,
        grid=(geom.t // tm,),
        in_specs=[pl.BlockSpec((tm, D_MODEL), lambda i: (i, 0)),
                  _mod_spec(geom, tm, 0), _mod_spec(geom, tm, 1),
                  pl.BlockSpec((1, D_MODEL), lambda i: (0, 0)),
                  pl.BlockSpec((D_MODEL, nw), lambda i: (0, 0))],
        out_specs=[pl.BlockSpec((tm, w), lambda i: (i, 0)) for _, w in _IN_SEGS],
        out_shape=[jax.ShapeDtypeStruct((geom.t, w), BF16) for _, w in _IN_SEGS],
        compiler_params=_cparams(("parallel",)),
        name="norm1_inproj",
    )(x, mods, mods, g1, w_in_p)


def _swa_kernel(geom, nb, q_ref, kp_ref, kc_ref, kn_ref, vp_ref, vc_ref, vn_ref, gq_ref, gk_ref,
                bdq_ref, bdk_ref, sink_ref, bias_ref, o_ref):
    i = pl.program_id(0)
    first = (geom.pos(i, nb * BLOCK) == 0).astype(jnp.int32)
    last = geom.is_last(i, nb * BLOCK).astype(jnp.int32)

    q = q_ref[...].astype(F32)
    msq = _dot((q * q).astype(BF16), bdq_ref[...]) * (1.0 / HEAD_DIM)
    q_t = (q * lax.rsqrt(msq + EPS) * gq_ref[...] * (HEAD_DIM ** -0.5 * LOG2E)).T.astype(BF16)

    k_all = jnp.concatenate([kp_ref[...], kc_ref[...], kn_ref[...]], axis=0).astype(F32)
    msk = _dot((k_all * k_all).astype(BF16), bdk_ref[...]) * (1.0 / HEAD_DIM)
    k_all = (k_all * lax.rsqrt(msk + EPS) * gk_ref[...]).astype(BF16)
    v_all = jnp.concatenate([vp_ref[...], vc_ref[...], vn_ref[...]], axis=0).astype(F32)
    ones_rows = 16
    v_t = jnp.concatenate([v_all.T, jnp.ones((ones_rows, v_all.shape[0]), F32)], axis=0).astype(BF16)
    zeros = jnp.zeros((HEAD_DIM, SWA_GROUP * BLOCK), BF16)

    pairs = [(b, g) for b in range(nb) for g in range(SWA_KV_HEADS)]
    scores, maxes = {}, {}
    for b, g in pairs:
        cls = (first if b == 0 else 0) + (2 * last if b == nb - 1 else 0)
        heads = range(g * SWA_GROUP, (g + 1) * SWA_GROUP)
        qg = jnp.concatenate([q_t[h * HEAD_DIM:(h + 1) * HEAD_DIM, b * BLOCK:(b + 1) * BLOCK] for h in heads],
                             axis=1)
        q_pad = jnp.concatenate([qg if j == g else zeros for j in range(SWA_KV_HEADS)], axis=0)
        s = _dot(k_all[b * BLOCK:(b + 3) * BLOCK, :], q_pad) + bias_ref[cls, g]
        scores[b, g] = s
        maxes[b, g] = jnp.maximum(jnp.max(s, axis=0, keepdims=True), sink_ref[g])
    probs = {bg: jnp.exp2(scores[bg] - maxes[bg]).astype(BF16) for bg in pairs}
    outs = {(b, g): _dot(v_t[:, b * BLOCK:(b + 3) * BLOCK], probs[b, g]) for b, g in pairs}
    for b in range(nb):
        rows = []
        for g in range(SWA_KV_HEADS):
            o = outs[b, g]
            denom = o[A_KV:A_KV + 1, :] + jnp.exp2(sink_ref[g] - maxes[b, g])
            og = o[g * HEAD_DIM:(g + 1) * HEAD_DIM, :] / denom
            rows += [og[:, j * BLOCK:(j + 1) * BLOCK] for j in range(SWA_GROUP)]
        o_ref[b * BLOCK:(b + 1) * BLOCK, :] = jnp.concatenate(rows, axis=0).T.astype(o_ref.dtype)


def _swa_bias():
    k = np.arange(3 * BLOCK)[:, None]
    q = np.arange(BLOCK)[None, :]
    rel = np.abs(k - BLOCK - q)
    out = np.zeros((4, SWA_KV_HEADS, 3 * BLOCK, SWA_GROUP * BLOCK), np.float32)
    for c in range(4):
        k_lo = BLOCK if c & 1 else 0
        k_hi = 2 * BLOCK if c & 2 else 3 * BLOCK
        valid = (rel <= WINDOW) & (k >= k_lo) & (k < k_hi)
        for g in range(SWA_KV_HEADS):
            for j in range(SWA_GROUP):
                slope = 2.0 ** (-8.0 * (g * SWA_GROUP + j + 1) / SWA_HEADS)
                out[c, g, :, j * BLOCK:(j + 1) * BLOCK] = np.where(valid, -slope * LOG2E * rel, NEG_INF)
    return jnp.asarray(out)


def _swa(geom, q, k, v, gq, gk, bdq, bdk, sink, bias):
    ts = geom.tile(512)
    nb = ts // BLOCK
    nblk = geom.t // BLOCK
    prev = lambda i: (jnp.maximum(i * nb - 1, 0), 0)
    cur = lambda i: (i, 0)
    nxt = lambda i: (jnp.minimum((i + 1) * nb, nblk - 1), 0)
    halo = lambda f: pl.BlockSpec((BLOCK, A_KV), f)
    const = lambda shape: pl.BlockSpec(shape, lambda i: (0,) * len(shape))
    return pl.pallas_call(
        functools.partial(_swa_kernel, geom, nb),
        grid=(geom.t // ts,),
        in_specs=[pl.BlockSpec((ts, A_Q), cur),
                  halo(prev), pl.BlockSpec((ts, A_KV), cur), halo(nxt),
                  halo(prev), pl.BlockSpec((ts, A_KV), cur), halo(nxt),
                  const((1, A_Q)), const((1, A_KV)), const((A_Q, A_Q)), const((A_KV, A_KV)),
                  const((SWA_KV_HEADS, 1, SWA_GROUP * BLOCK)), const(bias.shape)],
        out_specs=pl.BlockSpec((ts, A_Q), cur),
        out_shape=jax.ShapeDtypeStruct((geom.t, A_Q), BF16),
        compiler_params=_cparams(("parallel",)),
        name="swa_attention",
    )(q, k, k, k, v, v, v, gq, gk, bdq, bdk, sink, bias)


def _swap_rope_halves(a):
    lo, hi = MLA_NOPE, MLA_NOPE + MLA_ROPE // 2
    z = jnp.zeros_like(a)
    return jnp.concatenate([z[..., :lo], a[..., hi:MLA_QK], a[..., lo:hi], z[..., MLA_QK:]], axis=-1)


def _mla_prep_kernel(cq_ref, ckv_ref, kr_ref, gqn_ref, gkvn_ref, wq_ref, wkv_ref, wvt_ref, perm_ref, ones_ref,
                     gq_ref, gqs_ref, gk_ref, gks_ref, c_ref, s_ref, q_out, k_out, vt_out):
    cq = cq_ref[...].astype(F32)
    qn = (cq * lax.rsqrt(jnp.mean(cq * cq, axis=-1, keepdims=True) + EPS) * gqn_ref[...]).astype(BF16)
    ckv = ckv_ref[...].astype(F32)
    kvn = (ckv * lax.rsqrt(jnp.mean(ckv * ckv, axis=-1, keepdims=True) + EPS) * gkvn_ref[...]).astype(BF16)
    kr_b = kr_ref[...]
    kr = kr_b.astype(F32)
    lane = lax.broadcasted_iota(jnp.int32, (1, LANES), 1)
    ones = jnp.ones((MLA_VT_ROWS - MLA_V, cq.shape[0]), BF16)
    c = c_ref[...]
    sn = s_ref[...]
    q_c = c * (gq_ref[...] * (MLA_QK ** -0.5 * LOG2E))
    q_s = sn * (gqs_ref[...] * (MLA_QK ** -0.5 * LOG2E))
    k_c = c * gk_ref[...]
    k_rot = _dot(kr_b, perm_ref[...]) * (sn * gks_ref[...])
    inv_d = 1.0 / MLA_QK
    for hp in range(MLA_HEADS // 2):
        kv2 = _dot(kvn, wkv_ref[:, hp * 2 * LANES:(hp + 1) * 2 * LANES])
        for hh in range(2):
            h = 2 * hp + hh
            xq = _dot(qn, wq_ref[:, h * 2 * LANES:(h + 1) * 2 * LANES])
            x, xs = xq[:, :LANES], xq[:, LANES:]
            xk = jnp.where(lane < MLA_NOPE, kv2[:, hh * LANES:(hh + 1) * LANES], 0.0) + kr
            ss = _dot(jnp.concatenate([x * x, xk * xk], axis=1).astype(BF16), ones_ref[...])
            rq = lax.rsqrt(ss[:, :LANES] * inv_d + EPS)
            rk = lax.rsqrt(ss[:, LANES:] * inv_d + EPS)
            q_out[h] = (rq * (x * q_c + xs * q_s)).astype(BF16)
            k_out[h] = (rk * (xk * k_c + k_rot)).astype(BF16)
            vt_out[h, 0] = jnp.concatenate([_dot_nt(wvt_ref[h], kvn).astype(BF16), ones], axis=0)


def _mla_prep(geom, tk, cq, ckv, kr, gqn, gkvn, wq2, wkv, wvt, perm, ones2, gq_p, gq_s, gk_p, gk_s, rope_c, rope_s):
    tm = tk
    tok = lambda w: pl.BlockSpec((tm, w), lambda i: (i, 0))
    const = lambda shape: pl.BlockSpec(shape, lambda i: (0,) * len(shape))
    rope = pl.BlockSpec((tm, LANES), lambda i: (geom.pos(i, tm), 0))
    hm = pl.BlockSpec((MLA_HEADS, tm, LANES), lambda i: (0, i, 0))
    hm_t = pl.BlockSpec((MLA_HEADS, 1, MLA_VT_ROWS, tm), lambda i: (0, i, 0, 0))
    return pl.pallas_call(
        _mla_prep_kernel,
        grid=(geom.t // tm,),
        in_specs=[tok(MLA_Q_RANK), tok(MLA_KV_RANK), tok(LANES),
                  const((1, MLA_Q_RANK)), const((1, MLA_KV_RANK)),
                  const(wq2.shape), const(wkv.shape), const(wvt.shape), const(perm.shape), const(ones2.shape),
                  const((1, LANES)), const((1, LANES)), const((1, LANES)), const((1, LANES)), rope, rope],
        out_specs=[hm, hm, hm_t],
        out_shape=[jax.ShapeDtypeStruct((MLA_HEADS, geom.t, LANES), BF16),
                   jax.ShapeDtypeStruct((MLA_HEADS, geom.t, LANES), BF16),
                   jax.ShapeDtypeStruct((MLA_HEADS, geom.t // tm, MLA_VT_ROWS, tm), BF16)],
        compiler_params=_cparams(("parallel",)),
        name="mla_prep",
    )(cq, ckv, kr, gqn, gkvn, wq2, wkv, wvt, perm, ones2, gq_p, gq_s, gk_p, gk_s, rope_c, rope_s)


def _mla_attn_kernel(nk, q_ref, k_ref, vt_ref, o_ref, s_ref, mc_ref, p_ref, al_ref, acc_ref):
    tk = vt_ref.shape[3]
    tq = q_ref.shape[1]

    def stage_a(chunk, slot):
        off = chunk * tk
        if not isinstance(off, int):
            off = pl.multiple_of(off, tk)
        for hh in range(2):
            s = _dot_nt(k_ref[hh, pl.ds(off, tk), :], q_ref[hh])
            s_ref[slot, hh] = s
            mc_ref[slot, hh] = jnp.max(s, axis=0, keepdims=True)

    def stage_b(slot, m):
        m_out = []
        for hh in range(2):
            m_new = jnp.maximum(m[hh], mc_ref[slot, hh])
            al_ref[slot, hh] = jnp.exp2(m[hh] - m_new)
            p_ref[slot, hh] = jnp.exp2(s_ref[slot, hh] - m_new).astype(BF16)
            m_out.append(m_new)
        return tuple(m_out)

    def stage_c(chunk, slot):
        for hh in range(2):
            acc_ref[hh] = acc_ref[hh] * al_ref[slot, hh] + _dot(vt_ref[hh, chunk], p_ref[slot, hh])

    stage_a(0, 0)
    p_ref[1] = jnp.zeros(p_ref.shape[1:], BF16)
    al_ref[1] = jnp.ones(al_ref.shape[1:], F32)
    acc_ref[...] = jnp.zeros(acc_ref.shape, F32)
    m = (jnp.full((1, tq), NEG_INF, F32),) * 2

    def body(jj, m):
        j = 2 * jj
        stage_a(j + 1, 1)
        m = stage_b(0, m)
        stage_c(jnp.maximum(j - 1, 0), 1)
        stage_a(j + 2, 0)
        m = stage_b(1, m)
        stage_c(j, 0)
        return m

    m = lax.fori_loop(0, nk // 2 - 1, body, m)
    stage_a(nk - 1, 1)
    m = stage_b(0, m)
    stage_c(max(nk - 3, 0), 1)
    m = stage_b(1, m)
    stage_c(nk - 2, 0)
    stage_c(nk - 1, 1)
    o_t = jnp.concatenate([acc_ref[hh][:MLA_V, :] / acc_ref[hh][MLA_V:MLA_V + 1, :] for hh in range(2)], axis=0)
    o_ref[...] = o_t.T.astype(o_ref.dtype)


def _mla_attn_group(geom, tk, q, k, vt, out_prev, nseq, s, tok0):
    tq = min(1024, s)
    nq = s // tq
    nk = s // tk
    assert nk % 2 == 0
    qb0 = tok0 // tq
    sb0 = tok0 // s
    args = [q, k, vt]
    in_specs = [pl.BlockSpec((2, tq, LANES), lambda b, hp, i: (hp, qb0 + b * nq + i, 0)),
                pl.BlockSpec((2, s, LANES), lambda b, hp, i: (hp, sb0 + b, 0)),
                pl.BlockSpec((2, nk, MLA_VT_ROWS, tk), lambda b, hp, i: (hp, sb0 + b, 0, 0))]
    aliases = {}
    kern = functools.partial(_mla_attn_kernel, nk)
    if out_prev is not None:
        args.append(out_prev)
        in_specs.append(pl.BlockSpec(memory_space=pl.ANY))
        aliases = {3: 0}
        kern = lambda q_ref, k_ref, vt_ref, prev_ref, *rest, _k=kern: _k(q_ref, k_ref, vt_ref, *rest)
    return pl.pallas_call(
        kern,
        grid=(nseq, MLA_HEADS // 2, nq),
        in_specs=in_specs,
        out_specs=pl.BlockSpec((tq, LANES), lambda b, hp, i: (qb0 + b * nq + i, hp)),
        out_shape=jax.ShapeDtypeStruct((geom.t, MLA_HEADS * MLA_V), BF16),
        scratch_shapes=[pltpu.VMEM((2, 2, tk, tq), F32), pltpu.VMEM((2, 2, 1, tq), F32),
                        pltpu.VMEM((2, 2, tk, tq), BF16), pltpu.VMEM((2, 2, 1, tq), F32),
                        pltpu.VMEM((2, MLA_VT_ROWS, tq), F32)],
        input_output_aliases=aliases,
        compiler_params=_cparams(("parallel", "parallel", "arbitrary")),
        name="mla_attention",
    )(*args)


def _mla_attn(geom, tk, q, k, vt):
    assert geom.tp % geom.ss == 0
    out = _mla_attn_group(geom, tk, q, k, vt, None, geom.bp, geom.sp, 0)
    return _mla_attn_group(geom, tk, q, k, vt, out, geom.bs, geom.ss, geom.tp)


def _conv_kernel(geom, tc, up_ref, uc_ref, un_ref, w_ref, b_ref, gng_ref, gnb_ref, bd_ref, wpw_ref, bpw_ref,
                 o_ref, hs_ref):
    i = pl.program_id(0)
    first = geom.pos(i, tc) == 0
    last = geom.is_last(i, tc)

    def glu(u):
        u = u.astype(F32)
        return u[:, :CONV_CH] * jax.nn.sigmoid(u[:, CONV_CH:])

    hs_ref[0, 0:HALO, :] = jnp.where(first, 0.0, glu(up_ref[...]))
    hs_ref[0, HALO:HALO + tc, :] = glu(uc_ref[...])
    hs_ref[0, HALO + tc:HALO + tc + HALO, :] = jnp.where(last, 0.0, glu(un_ref[...]))
    span = tc + 2 * HALO - SUBLANES
    for sft in range(1, SUBLANES):
        hs_ref[sft, 0:span, :] = hs_ref[0, sft:sft + span, :]

    rows = 64 if tc % 64 == 0 else tc
    chunks = range(0, tc, rows)
    accs = {r0: jnp.zeros((rows, CONV_CH), F32) + b_ref[...] for r0 in chunks}
    for j in range(CONV_WIDTH):
        for r0 in chunks:
            start = HALO + r0 + j - CONV_PAD
            sft = start % SUBLANES
            accs[r0] = accs[r0] + hs_ref[sft, start - sft:start - sft + rows, :] * w_ref[j:j + 1, :]
    for r0 in chunks:
        acc = accs[r0]
        mu = _split_dot(acc, bd_ref[...]) * (CONV_GROUPS / CONV_CH)
        d = acc - mu
        var = _split_dot(d * d, bd_ref[...]) * (CONV_GROUPS / CONV_CH)
        hn = d * lax.rsqrt(var + EPS) * gng_ref[...] + gnb_ref[...]
        y = _dot(_silu(hn).astype(BF16), wpw_ref[...]) + bpw_ref[...]
        o_ref[r0:r0 + rows, :] = y.astype(o_ref.dtype)


def _conv(geom, uc, dw_w, dw_b, gn_g, gn_b, bd, w_pw2, b_pw2):
    tc = geom.tile(256)
    hb = tc // HALO
    nh = geom.t // HALO
    const = lambda shape: pl.BlockSpec(shape, lambda i: (0, 0))
    return pl.pallas_call(
        functools.partial(_conv_kernel, geom, tc),
        grid=(geom.t // tc,),
        in_specs=[pl.BlockSpec((HALO, 2 * CONV_CH), lambda i: (jnp.maximum(i * hb - 1, 0), 0)),
                  pl.BlockSpec((tc, 2 * CONV_CH), lambda i: (i, 0)),
                  pl.BlockSpec((HALO, 2 * CONV_CH), lambda i: (jnp.minimum((i + 1) * hb, nh - 1), 0)),
                  const((32, CONV_CH)), const((1, CONV_CH)), const((1, CONV_CH)), const((1, CONV_CH)),
                  const((CONV_CH, CONV_CH)), const((CONV_CH, CONV_CH)), const((1, CONV_CH))],
        out_specs=pl.BlockSpec((tc, CONV_CH), lambda i: (i, 0)),
        out_shape=jax.ShapeDtypeStruct((geom.t, CONV_CH), BF16),
        scratch_shapes=[pltpu.VMEM((SUBLANES, tc + 2 * HALO, CONV_CH), F32)],
        compiler_params=_cparams(("parallel",)),
        name="conformer_conv",
    )(uc, uc, uc, dw_w, dw_b, gn_g, gn_b, bd, w_pw2, b_pw2)


def _rms_rows(x, g):
    return x * lax.rsqrt(jnp.mean(x * x, axis=-1, keepdims=True) + EPS) * g


def _store_slabs(ref, x, base=0):
    rows, half = x.shape[0], x.shape[1] // 2
    hi = pltpu.bitcast(x[:, :half].astype(BF16).astype(F32), U32)
    lo = pltpu.bitcast(x[:, half:].astype(BF16).astype(F32), U32)
    words = jnp.bitwise_or(hi, jnp.right_shift(lo, jnp.uint32(16)))
    for c in range(SLAB):
        ref[pl.ds(base + c, rows, stride=SLAB), :] = words[:, c * LANES:(c + 1) * LANES]


def _load_slabs(ref, rows, base=0):
    words = jnp.concatenate([ref[pl.ds(base + c, rows, stride=SLAB), :] for c in range(SLAB)], axis=1)
    hi = pltpu.bitcast(jnp.bitwise_and(words, jnp.uint32(0xFFFF0000)), F32)
    lo = pltpu.bitcast(jnp.left_shift(words, jnp.uint32(16)), F32)
    return jnp.concatenate([hi, lo], axis=1)


def _slab_row(ref, r):
    return ref.at[pl.ds(pl.multiple_of(r * SLAB, SLAB), SLAB)]


def _outproj_kernel(x_ref, oa_ref, ob_ref, oc_ref, ga_ref, gb_ref, gc_ref, w_ref, gate_ref, sh_ref, sc_ref,
                    g2_ref, wr_hi_ref, br_ref, x1_ref, h2_ref, lg_ref):
    na = _rms_rows(oa_ref[...].astype(F32), ga_ref[...]).astype(BF16)
    nb = _rms_rows(ob_ref[...].astype(F32), gb_ref[...]).astype(BF16)
    nc = _rms_rows(oc_ref[...].astype(F32), gc_ref[...]).astype(BF16)
    wa = A_Q
    wb = wa + MLA_HEADS * MLA_V
    y = _dot(na, w_ref[0:wa, :]) + _dot(nb, w_ref[wa:wb, :]) + _dot(nc, w_ref[wb:, :])
    x1 = x_ref[...] + gate_ref[...] * y
    x1_ref[...] = x1
    h2 = _rms_rows(x1, g2_ref[...]) * (1.0 + sc_ref[...]) + sh_ref[...]
    hi = h2.astype(BF16)
    lo = (h2 - hi.astype(F32)).astype(BF16)
    _store_slabs(h2_ref, h2)
    lg2 = _dot(hi, wr_hi_ref[...])
    lg = lg2[:, :LANES] + lg2[:, LANES:] + _dot(lo, wr_hi_ref[:, :LANES]) + br_ref[...]
    lg_ref[...] = lg.T


def _outproj(geom, x, oa, ob, oc, ga, gb, gc, w_out, mods, g2, wr2, br):
    tm = geom.tile(256)
    tok = lambda w: pl.BlockSpec((tm, w), lambda i: (i, 0))
    const = lambda shape: pl.BlockSpec(shape, lambda i: (0, 0))
    wb = MLA_HEADS * MLA_V
    return pl.pallas_call(
        _outproj_kernel,
        grid=(geom.t // tm,),
        in_specs=[tok(D_MODEL), tok(A_Q), tok(wb), tok(CONV_CH),
                  const((1, A_Q)), const((1, wb)), const((1, CONV_CH)),
                  const((D_MODEL, D_MODEL)),
                  _mod_spec(geom, tm, 2), _mod_spec(geom, tm, 3), _mod_spec(geom, tm, 4),
                  const((1, D_MODEL)), const((D_MODEL, 2 * LANES)), const((1, LANES))],
        out_specs=[tok(D_MODEL), pl.BlockSpec((tm * SLAB, LANES), lambda i: (i, 0)),
                   pl.BlockSpec((LANES, tm), lambda i: (0, i))],
        out_shape=[jax.ShapeDtypeStruct((geom.t, D_MODEL), F32),
                   jax.ShapeDtypeStruct((geom.t * SLAB, LANES), U32),
                   jax.ShapeDtypeStruct((LANES, geom.t), F32)],
        compiler_params=_cparams(("parallel",)),
        name="merge_outproj_norm2",
    )(x, oa, ob, oc, ga, gb, gc, w_out, mods, mods, mods, g2, wr2, br)


ROUTE_ROWS = 40


def _route_kernel(lg_ref, sel_ref, rt_ref, cnt_ref):
    lg = lg_ref[...]
    row = lax.broadcasted_iota(jnp.int32, lg.shape, 0)
    rowf = row.astype(F32)
    big = float(2 * LANES)
    is_g = jnp.logical_and(row >= N_EXPERTS, row < N_EXPERTS + N_GROUPS)
    gl = jnp.where(is_g, lg, NEG_INF)
    gmax = jnp.max(gl, axis=0, keepdims=True)
    gsum = jnp.sum(jnp.exp(gl - gmax), axis=0, keepdims=True)
    g_val = 1.0 / gsum
    g_idx = jnp.min(jnp.where(jnp.logical_and(is_g, gl == gmax), rowf, big), axis=0, keepdims=True) - N_EXPERTS
    lo = g_idx * EXPERTS_PER_GROUP
    is_e = jnp.logical_and(rowf >= lo, rowf < lo + EXPERTS_PER_GROUP)
    el = jnp.where(is_e, lg, NEG_INF)
    emax = jnp.max(el, axis=0, keepdims=True)
    ee = jnp.exp(el - emax)
    e_prob = ee / jnp.sum(ee, axis=0, keepdims=True)
    p1 = jnp.where(is_e, e_prob, -1.0)
    v1 = jnp.max(p1, axis=0, keepdims=True)
    i1 = jnp.min(jnp.where(p1 == v1, rowf, big), axis=0, keepdims=True)
    p2 = jnp.where(rowf == i1, -1.0, p1)
    v2 = jnp.max(p2, axis=0, keepdims=True)
    i2 = jnp.min(jnp.where(p2 == v2, rowf, big), axis=0, keepdims=True)
    scale = g_val / (v1 + v2)
    r8 = lax.broadcasted_iota(jnp.int32, rt_ref.shape, 0)
    rt_ref[...] = (jnp.where(r8 == 0, i1, 0.0) + jnp.where(r8 == 1, i2, 0.0) +
                   jnp.where(r8 == 2, v1 * scale, 0.0) + jnp.where(r8 == 3, v2 * scale, 0.0))
    chosen = jnp.where(jnp.logical_or(rowf == i1, rowf == i2), 1.0, 0.0).astype(BF16)
    cnt_ref[...] = _dot(chosen, sel_ref[...])


def _route(geom, tr, logits_t, sel):
    n = geom.t // tr
    return pl.pallas_call(
        _route_kernel,
        grid=(n,),
        in_specs=[pl.BlockSpec((ROUTE_ROWS, tr), lambda i: (0, i)), pl.BlockSpec(sel.shape, lambda i: (0, 0))],
        out_specs=[pl.BlockSpec((SUBLANES, tr), lambda i: (0, i)),
                   pl.BlockSpec((None, ROUTE_ROWS, LANES), lambda i: (i, 0, 0))],
        out_shape=[jax.ShapeDtypeStruct((SUBLANES, geom.t), F32), jax.ShapeDtypeStruct((n, ROUTE_ROWS, LANES), F32)],
        compiler_params=_cparams(("parallel",)),
        name="moe_route",
    )(logits_t, sel)


def _slots_kernel(tm, rt_ref, st_ref, tri_ref, sl_ref):
    rt = rt_ref[...]
    rowf = lax.broadcasted_iota(jnp.int32, (ROUTE_ROWS, rt.shape[1]), 0).astype(F32)
    oh1 = rowf == rt[0:1, :]
    oh2 = rowf == rt[1:2, :]
    chosen = jnp.where(jnp.logical_or(oh1, oh2), 1.0, 0.0).astype(BF16)
    for j in range(rt.shape[1] // tm):
        cols = slice(j * tm, (j + 1) * tm)
        first = _dot(chosen[:, cols], tri_ref[...]) + st_ref[:, j:j + 1]
        s1 = jnp.sum(jnp.where(oh1[:, cols], first, 0.0), axis=0, keepdims=True)
        s2 = jnp.sum(jnp.where(oh2[:, cols], first, 0.0), axis=0, keepdims=True)
        sl_ref[j] = jnp.concatenate([s1, s2], axis=1).astype(jnp.int32)


def _slots(geom, tr, tm, rt_t, start3, tri):
    ns = tr // tm
    return pl.pallas_call(
        functools.partial(_slots_kernel, tm),
        grid=(geom.t // tr,),
        in_specs=[pl.BlockSpec((SUBLANES, tr), lambda i: (0, i)),
                  pl.BlockSpec((None, ROUTE_ROWS, LANES), lambda i: (i, 0, 0)),
                  pl.BlockSpec((tm, tm), lambda i: (0, 0))],
        out_specs=pl.BlockSpec((ns, 1, 2 * tm), lambda i: (i, 0, 0)),
        out_shape=jax.ShapeDtypeStruct((geom.t // tm, 1, 2 * tm), jnp.int32),
        compiler_params=_cparams(("parallel",)),
        name="moe_slots",
    )(rt_t, start3, tri)


def _plan(cnt3, ns, n_tiles_max):
    n_steps = cnt3.shape[0]
    cnt_t = cnt3[:, :, :ns].transpose(0, 2, 1).reshape(n_steps * ns, ROUTE_ROWS)
    incl = jnp.cumsum(cnt_t, axis=0)
    tiles_e = jnp.ceil(incl[-1] / TME)
    end_e = jnp.cumsum(tiles_e)
    start = incl - cnt_t + ((end_e - tiles_e) * TME)[None, :]
    start3 = jnp.pad(start.reshape(n_steps, ns, ROUTE_ROWS).transpose(0, 2, 1), ((0, 0), (0, 0), (0, LANES - ns)))
    n_used = end_e[-1].astype(jnp.int32)
    tile_idx = jnp.minimum(jnp.arange(n_tiles_max, dtype=jnp.int32), n_used - 1)
    tile_exp = jnp.sum(tile_idx[:, None] >= end_e[None, :N_EXPERTS].astype(jnp.int32), axis=1).astype(jnp.int32)
    return start3, tile_exp, tile_idx, n_used.reshape(1)


def _scatter_kernel(sl_ref, h_ref, xs_in, xs_out, sem):
    del xs_in
    tm = h_ref.shape[0] // SLAB

    def body(rr, carry):
        for j in range(ROW_UNROLL):
            r = rr * ROW_UNROLL + j
            pltpu.make_async_copy(_slab_row(h_ref, r), _slab_row(xs_out, sl_ref[0, r]), sem).start(priority=0)
            pltpu.make_async_copy(_slab_row(h_ref, r), _slab_row(xs_out, sl_ref[0, tm + r]), sem).start(priority=1)
        return carry

    lax.fori_loop(0, tm // ROW_UNROLL, body, 0)
    for _ in range(2):
        pltpu.make_async_copy(h_ref, xs_out.at[pl.ds(0, tm * SLAB)], sem).wait()


def _scatter(geom, tm, slots_smem, h2, xs):
    return pl.pallas_call(
        _scatter_kernel,
        grid=(geom.t // tm,),
        in_specs=[pl.BlockSpec((None, 1, 2 * tm), lambda i: (i, 0, 0), memory_space=pltpu.SMEM),
                  pl.BlockSpec((tm * SLAB, LANES), lambda i: (i, 0)),
                  pl.BlockSpec(memory_space=pl.ANY)],
        out_specs=pl.BlockSpec(memory_space=pl.ANY),
        out_shape=jax.ShapeDtypeStruct(xs.shape, xs.dtype),
        scratch_shapes=[pltpu.SemaphoreType.DMA(())],
        input_output_aliases={2: 0},
        compiler_params=_cparams(("arbitrary",)),
        name="moe_scatter",
    )(slots_smem, h2, xs)


def _experts_kernel(te_ref, ti_ref, nu_ref, xs_ref, wg_ref, wu_ref, wd_ref, ys_ref):
    del te_ref, ti_ref

    @pl.when(pl.program_id(0) < nu_ref[0])
    def _():
        x = _load_slabs(xs_ref, TME).astype(BF16)
        act = (_silu(_dot(x, wg_ref[...].astype(BF16))) * _dot(x, wu_ref[...].astype(BF16))).astype(BF16)
        _store_slabs(ys_ref, _dot(act, wd_ref[...].astype(BF16)))


def _experts(layer, xs, tile_exp, tile_idx, n_used, wg, wu, wd):
    n_tiles_max = xs.shape[0] // (TME * SLAB)
    slab = pl.BlockSpec((TME * SLAB, LANES), lambda n, te, ti, nu: (ti[n], 0))
    wspec = lambda a, b: pl.BlockSpec((None, None, a, b), lambda n, te, ti, nu: (layer, te[n], 0, 0))
    return pl.pallas_call(
        _experts_kernel,
        grid_spec=pltpu.PrefetchScalarGridSpec(
            num_scalar_prefetch=3,
            grid=(n_tiles_max,),
            in_specs=[slab, wspec(D_MODEL, D_EXPERT), wspec(D_MODEL, D_EXPERT), wspec(D_EXPERT, D_MODEL)],
            out_specs=slab),
        out_shape=jax.ShapeDtypeStruct(xs.shape, U32),
        compiler_params=_cparams(("arbitrary",)),
        name="moe_experts",
    )(tile_exp, tile_idx, n_used, xs, wg, wu, wd)


def _combine_kernel(geom, final, sl_cur, sl_nxt, rt_ref, x1_ref, gate_ref, ys_hbm, *refs):
    buf1, buf2, sem = refs[-3:]
    i = pl.program_id(0)
    tm = x1_ref.shape[0]

    def issue_row(sl_ref, slot, r):
        dst = slot * tm + r
        pltpu.make_async_copy(_slab_row(ys_hbm, sl_ref[0, r]), _slab_row(buf1, dst), sem.at[slot]).start(priority=0)
        pltpu.make_async_copy(_slab_row(ys_hbm, sl_ref[0, tm + r]), _slab_row(buf2, dst),
                              sem.at[slot]).start(priority=1)

    def wait_slot(slot):
        base = pl.multiple_of(slot * (tm * SLAB), tm * SLAB)
        for buf in (buf1, buf2):
            pltpu.make_async_copy(ys_hbm.at[pl.ds(0, tm * SLAB)], buf.at[pl.ds(base, tm * SLAB)],
                                  sem.at[slot]).wait()
        return base

    @pl.when(i == 0)
    def _():
        def body(rr, carry):
            for j in range(ROW_UNROLL):
                issue_row(sl_cur, 0, rr * ROW_UNROLL + j)
            return carry
        lax.fori_loop(0, tm // ROW_UNROLL, body, 0)

    slot = i % 2
    base = wait_slot(slot)
    for r in range(tm):
        issue_row(sl_nxt, 1 - slot, r)
    w = jnp.concatenate([rt_ref[...], jnp.zeros((LANES - SUBLANES, tm), F32)], axis=0).T
    y = w[:, 2:3] * _load_slabs(buf1, tm, base) + w[:, 3:4] * _load_slabs(buf2, tm, base)
    x_new = x1_ref[...] + gate_ref[...] * y
    if final:
        out_p, out_s = refs[:2]
        n_prompt = geom.tp // tm

        @pl.when(i < n_prompt)
        def _():
            out_p[...] = x_new

        @pl.when(i >= n_prompt)
        def _():
            out_s[...] = x_new
    else:
        sh_ref, sc_ref, g_ref, w_ref, o_ref = refs[:5]
        o_ref[...] = x_new
        _norm1_project(x_new, sh_ref, sc_ref, g_ref, w_ref, refs[5:-3])

    @pl.when(i == pl.num_programs(0) - 1)
    def _():
        wait_slot(1 - slot)


def _combine(geom, tm, slots_smem, rt, x1, mods, ys, nxt=None):
    n = geom.t // tm
    n_prompt = geom.tp // tm
    tok = lambda w: pl.BlockSpec((tm, w), lambda i: (i, 0))
    smem = lambda f: pl.BlockSpec((None, 1, 2 * tm), f, memory_space=pltpu.SMEM)
    args = [slots_smem, slots_smem, rt, x1, mods, ys]
    in_specs = [smem(lambda i: (i, 0, 0)), smem(lambda i: (jnp.minimum(i + 1, n - 1), 0, 0)),
                pl.BlockSpec((SUBLANES, tm), lambda i: (0, i)), tok(D_MODEL), _mod_spec(geom, tm, 5),
                pl.BlockSpec(memory_space=pl.ANY)]
    if nxt is None:
        out_specs = [pl.BlockSpec((tm, D_MODEL), lambda i: (jnp.minimum(i, n_prompt - 1), 0)),
                     pl.BlockSpec((tm, D_MODEL), lambda i: (jnp.maximum(i - n_prompt, 0), 0))]
        out_shape = [jax.ShapeDtypeStruct((geom.tp, D_MODEL), F32),
                     jax.ShapeDtypeStruct((geom.t - geom.tp, D_MODEL), F32)]
    else:
        mods_n, g1_n, w_in_n = nxt
        args += [mods_n, mods_n, g1_n, w_in_n]
        in_specs += [_mod_spec(geom, tm, 0), _mod_spec(geom, tm, 1), pl.BlockSpec((1, D_MODEL), lambda i: (0, 0)),
                     pl.BlockSpec(w_in_n.shape, lambda i: (0, 0))]
        out_specs = [tok(D_MODEL)] + [tok(w) for _, w in _IN_SEGS]
        out_shape = [jax.ShapeDtypeStruct((geom.t, D_MODEL), F32)] + \
            [jax.ShapeDtypeStruct((geom.t, w), BF16) for _, w in _IN_SEGS]
    return pl.pallas_call(
        functools.partial(_combine_kernel, geom, nxt is None),
        grid=(n,),
        in_specs=in_specs,
        out_specs=out_specs,
        out_shape=out_shape,
        scratch_shapes=[pltpu.VMEM((2 * tm * SLAB, LANES), U32), pltpu.VMEM((2 * tm * SLAB, LANES), U32),
                        pltpu.SemaphoreType.DMA((2,))],
        compiler_params=_cparams(("arbitrary",)),
        name="moe_combine",
    )(*args)


def _block_diag_ones(n, blk):
    idx = np.arange(n) // blk
    return jnp.asarray((idx[:, None] == idx[None, :]).astype(np.float32), dtype=BF16)


def _rope_tables(smax):
    inv_freq = 1.0 / (ROPE_BASE ** (jnp.arange(0, MLA_ROPE, 2, dtype=F32) / MLA_ROPE))
    ang = jnp.arange(smax, dtype=F32)[:, None] * inv_freq[None, :]
    cos, sin = jnp.cos(ang), jnp.sin(ang)
    zeros = lambda w: jnp.zeros((smax, w), F32)
    pad = LANES - MLA_QK
    c = jnp.concatenate([jnp.ones((smax, MLA_NOPE), F32), cos, cos, zeros(pad)], axis=1)
    s = jnp.concatenate([zeros(MLA_NOPE), -sin, sin, zeros(pad)], axis=1)
    return c, s


def _pad_cols(w, n):
    return jnp.pad(w, ((0, 0), (0, n - w.shape[1])))


def _layer_weights(l, w_in, mla_w_q_up, mla_w_kv_up, mla_q_gain, mla_k_gain, swa_q_gain, swa_k_gain, swa_sink,
                   moe_w_group, moe_b_group, moe_w_expert, moe_b_expert):
    wi = w_in[l]
    kr = jnp.pad(wi[:, A_IN + MLA_Q_RANK + MLA_KV_RANK:A_IN + B_IN], ((0, 0), (MLA_NOPE, LANES - MLA_QK)))
    w_in_p = jnp.concatenate([wi[:, :A_IN], wi[:, A_IN:A_IN + MLA_Q_RANK + MLA_KV_RANK], kr, wi[:, A_IN + B_IN:]],
                             axis=1).astype(BF16)
    wq = mla_w_q_up[l].reshape(MLA_Q_RANK, MLA_HEADS, MLA_QK)
    wq_p = jnp.pad(wq, ((0, 0), (0, 0), (0, LANES - MLA_QK)))
    wq2 = jnp.concatenate([wq_p, _swap_rope_halves(wq_p)], axis=2).reshape(MLA_Q_RANK, MLA_HEADS * 2 * LANES).astype(BF16)
    wvt = mla_w_kv_up[l].reshape(MLA_KV_RANK, MLA_HEADS, MLA_NOPE + MLA_V)[:, :, MLA_NOPE:].transpose(1, 2, 0).astype(BF16)
    gq_p = _pad_cols(mla_q_gain[l][None, :], LANES)
    gk_p = _pad_cols(mla_k_gain[l][None, :], LANES)
    gq_s, gk_s = _swap_rope_halves(gq_p), _swap_rope_halves(gk_p)
    gq_a = jnp.tile(swa_q_gain[l], SWA_HEADS)[None, :]
    gk_a = jnp.tile(swa_k_gain[l], SWA_KV_HEADS)[None, :]
    sink = jnp.repeat(swa_sink[l] * LOG2E, BLOCK).reshape(SWA_KV_HEADS, 1, SWA_GROUP * BLOCK)
    wr = _pad_cols(jnp.concatenate([moe_w_expert[l], moe_w_group[l]], axis=1), LANES)
    wr_hi = wr.astype(BF16)
    wr2 = jnp.concatenate([wr_hi, (wr - wr_hi.astype(F32)).astype(BF16)], axis=1)
    br = _pad_cols(jnp.concatenate([moe_b_expert[l], moe_b_group[l]])[None, :], LANES)
    return w_in_p, wq2, wvt, gq_p, gq_s, gk_p, gk_s, gq_a, gk_a, sink, wr2, br


def kernel(x_prompt, x_sample, c_prompt, c_sample, w_ada, b_ada, norm1_g, norm2_g, w_in, swa_q_gain, swa_k_gain, swa_sink, mla_q_norm_g, mla_w_q_up, mla_kv_norm_g, mla_w_kv_up, mla_q_gain, mla_k_gain, conv_dw_w, conv_dw_b, conv_gn_g, conv_gn_b, conv_w_pw2, conv_b_pw2, out_norm_a, out_norm_b, out_norm_c, w_out, moe_w_group, moe_b_group, moe_w_expert, moe_b_expert, moe_w_gate, moe_w_up, moe_w_down):
    bp, sp, d = x_prompt.shape
    bs, ss, _ = x_sample.shape
    assert d == D_MODEL
    geom = _Geom(bp, sp, bs, ss)
    depth = w_ada.shape[0]

    x = jnp.concatenate([x_prompt.reshape(bp * sp, d), x_sample.reshape(bs * ss, d)], axis=0)
    c = jnp.concatenate([c_prompt, c_sample], axis=0)
    rows = -(-geom.nb // 8) * 8
    c_pad = jnp.pad(c, ((0, rows - geom.nb), (0, 0)))
    mods_all = _modulation(c_pad, w_ada, b_ada)

    rope_c, rope_s = _rope_tables(max(sp, ss))
    rope_perm = _swap_rope_halves(jnp.eye(LANES, dtype=F32)).astype(BF16)
    ones2 = _block_diag_ones(2 * LANES, LANES)
    bdq = _block_diag_ones(A_Q, HEAD_DIM)
    bdk = _block_diag_ones(A_KV, HEAD_DIM)
    bdc = _block_diag_ones(CONV_CH, CONV_CH // CONV_GROUPS)
    swa_bias = _swa_bias()
    row = lambda v: v[None, :]
    tm_moe = geom.tile(256)
    n_tiles_max = 2 * geom.t // TME + N_EXPERTS
    tr_moe = geom.tile(2048)
    tri = jnp.asarray(np.triu(np.ones((tm_moe, tm_moe), np.float32), 1), dtype=BF16)
    sel = jnp.asarray(np.arange(tr_moe)[:, None] // tm_moe == np.arange(LANES)[None, :], dtype=BF16)
    xs = jnp.zeros((n_tiles_max * TME * SLAB, LANES), U32)

    weights = [_layer_weights(l, w_in, mla_w_q_up, mla_w_kv_up, mla_q_gain, mla_k_gain, swa_q_gain, swa_k_gain,
                              swa_sink, moe_w_group, moe_b_group, moe_w_expert, moe_b_expert) for l in range(depth)]
    mods_l = [mods_all[l, :geom.nb].reshape(geom.nb * N_MOD, 1, D_MODEL) for l in range(depth)]
    proj = _inproj(geom, x, mods_l[0], row(norm1_g[0]), weights[0][0])

    for l in range(depth):
        (w_in_p, wq2, wvt, gq_p, gq_s, gk_p, gk_s, gq_a, gk_a, sink, wr2, br) = weights[l]
        mods = mods_l[l]
        q_a, k_a, v_a, cq, ckv, kr, uc = proj
        out_a = _swa(geom, q_a, k_a, v_a, gq_a, gk_a, bdq, bdk, sink, swa_bias)
        tk = geom.tile(512)
        q_b, k_b, vt_b = _mla_prep(geom, tk, cq, ckv, kr, row(mla_q_norm_g[l]), row(mla_kv_norm_g[l]), wq2,
                                   mla_w_kv_up[l].astype(BF16), wvt, rope_perm, ones2, gq_p, gq_s, gk_p, gk_s,
                                   rope_c, rope_s)
        out_b = _mla_attn(geom, tk, q_b, k_b, vt_b)
        dw_w = jnp.pad(conv_dw_w[l].reshape(CONV_WIDTH, CONV_CH), ((0, 32 - CONV_WIDTH), (0, 0)))
        out_c = _conv(geom, uc, dw_w, row(conv_dw_b[l]), row(conv_gn_g[l]), row(conv_gn_b[l]), bdc,
                      conv_w_pw2[l].astype(BF16), row(conv_b_pw2[l]))
        x1, h2, logits = _outproj(geom, x, out_a, out_b, out_c, row(out_norm_a[l]), row(out_norm_b[l]),
                                  row(out_norm_c[l]), w_out[l].astype(BF16), mods, row(norm2_g[l]),
                                  wr2, br)
        rt, cnt3 = _route(geom, tr_moe, logits, sel)
        start3, tile_exp, tile_idx, n_used = _plan(cnt3, tr_moe // tm_moe, n_tiles_max)
        slots = _slots(geom, tr_moe, tm_moe, rt, start3, tri)
        xs = _scatter(geom, tm_moe, slots, h2, xs)
        ys = _experts(l, xs, tile_exp, tile_idx, n_used, moe_w_gate, moe_w_up, moe_w_down)
        if l + 1 < depth:
            x, *proj = _combine(geom, tm_moe, slots, rt, x1, mods, ys,
                                nxt=(mods_l[l + 1], row(norm1_g[l + 1]), weights[l + 1][0]))
        else:
            y_prompt, y_sample = _combine(geom, tm_moe, slots, rt, x1, mods, ys)

    return (y_prompt.reshape(bp, sp, d), y_sample.reshape(bs, ss, d))
```

```python
import functools

import numpy as np
import jax
import jax.numpy as jnp
from jax import lax
from jax.experimental import pallas as pl
from jax.experimental.pallas import tpu as pltpu

F32 = jnp.float32
BF16 = jnp.bfloat16

EPS = 1e-6
NEG_INF = -1e30
LOG2E = 1.4426950408889634

D_MODEL = 1024
HEAD_DIM = 64
SWA_HEADS = 6
SWA_KV_HEADS = 2
SWA_GROUP = SWA_HEADS // SWA_KV_HEADS
WINDOW = 128
BLOCK = WINDOW
MLA_HEADS = 6
MLA_Q_RANK = 256
MLA_KV_RANK = 128
MLA_NOPE = 64
MLA_ROPE = 32
MLA_QK = MLA_NOPE + MLA_ROPE
MLA_V = 64
MLA_VT_ROWS = MLA_V + 16
ROPE_BASE = 10000.0
CONV_CH = 256
CONV_GROUPS = 4
CONV_WIDTH = 31
CONV_PAD = (CONV_WIDTH - 1) // 2
A_Q = SWA_HEADS * HEAD_DIM
A_KV = SWA_KV_HEADS * HEAD_DIM
A_IN = A_Q + 2 * A_KV
B_IN = MLA_Q_RANK + MLA_KV_RANK + MLA_ROPE
N_GROUPS = 4
EXPERTS_PER_GROUP = 8
N_EXPERTS = N_GROUPS * EXPERTS_PER_GROUP
D_EXPERT = 256
N_MOD = 6

LANES = 128
SUBLANES = 8
SLAB = D_MODEL // (2 * 128)
U32 = jnp.uint32
TME = 512
ROW_UNROLL = 8
HALO = 16
VMEM_LIMIT = 48 * 1024 * 1024


class _Geom:
    def __init__(self, bp, sp, bs, ss):
        self.bp, self.sp, self.bs, self.ss = bp, sp, bs, ss
        self.tp = bp * sp
        self.t = bp * sp + bs * ss
        self.nb = bp + bs

    def tile(self, target):
        t = target
        while self.sp % t or self.ss % t:
            t //= 2
        return t

    def batch(self, i, tm):
        npt = self.tp // tm
        return jnp.where(i < npt, i // (self.sp // tm), self.bp + (i - npt) // (self.ss // tm))

    def pos(self, i, tm):
        npt = self.tp // tm
        return jnp.where(i < npt, i % (self.sp // tm), (i - npt) % (self.ss // tm))

    def is_last(self, i, tm):
        npt = self.tp // tm
        return jnp.where(i < npt, i % (self.sp // tm) == self.sp // tm - 1,
                         (i - npt) % (self.ss // tm) == self.ss // tm - 1)


def _cparams(sem):
    return pltpu.CompilerParams(dimension_semantics=sem, vmem_limit_bytes=VMEM_LIMIT)


def _silu(x):
    return x * jax.nn.sigmoid(x)


def _dot(a, b):
    return jnp.dot(a, b, preferred_element_type=F32)


def _dot_nt(a, b):
    return lax.dot_general(a, b, (((1,), (1,)), ((), ())), preferred_element_type=F32)


def _split_dot(x, w):
    hi = x.astype(BF16)
    lo = (x - hi.astype(F32)).astype(BF16)
    return _dot(hi, w) + _dot(lo, w)


def _mod_kernel(c_ref, w_ref, b_ref, o_ref):
    c = c_ref[...]
    o_ref[...] = _dot(_silu(c).astype(BF16), w_ref[...].astype(BF16)) + b_ref[...]


def _modulation(c_pad, w_ada, b_ada):
    L, d, n = w_ada.shape
    tn = 768
    rows = c_pad.shape[0]
    return pl.pallas_call(
        _mod_kernel,
        grid=(L, n // tn),
        in_specs=[pl.BlockSpec((rows, d), lambda l, j: (0, 0)),
                  pl.BlockSpec((None, d, tn), lambda l, j: (l, 0, j)),
                  pl.BlockSpec((None, 1, tn), lambda l, j: (l, 0, j))],
        out_specs=pl.BlockSpec((None, rows, tn), lambda l, j: (l, 0, j)),
        out_shape=jax.ShapeDtypeStruct((L, rows, n), F32),
        compiler_params=_cparams(("arbitrary", "arbitrary")),
        name="adaln_mod",
    )(c_pad, w_ada, b_ada.reshape(L, 1, n))


_IN_SEGS = (("q", A_Q), ("k", A_KV), ("v", A_KV), ("cq", MLA_Q_RANK), ("ckv", MLA_KV_RANK), ("kr", LANES),
            ("uc", 2 * CONV_CH))


def _norm1_project(x, sh_ref, sc_ref, g_ref, w_ref, out_refs):
    ms = jnp.mean(x * x, axis=-1, keepdims=True)
    h = x * lax.rsqrt(ms + EPS) * g_ref[...]
    h = (h * (1.0 + sc_ref[...]) + sh_ref[...]).astype(BF16)
    off = 0
    for (_, width), o_ref in zip(_IN_SEGS, out_refs):
        o_ref[...] = _dot(h, w_ref[:, off:off + width]).astype(o_ref.dtype)
        off += width


def _x_specs(geom, tm, x_parts):
    if len(x_parts) == 1:
        return [pl.BlockSpec((tm, D_MODEL), lambda i: (i, 0))]
    n_prompt = geom.tp // tm
    return [pl.BlockSpec((tm, D_MODEL), lambda i: (jnp.minimum(i, n_prompt - 1), 0)),
            pl.BlockSpec((tm, D_MODEL), lambda i: (jnp.maximum(i - n_prompt, 0), 0))]


def _x_tile(geom, x_refs):
    if len(x_refs) == 1:
        return x_refs[0][...]
    tm = x_refs[0].shape[0]
    return jnp.where(pl.program_id(0) < geom.tp // tm, x_refs[0][...], x_refs[1][...])


def _inproj_kernel(geom, n_x, *refs):
    sh_ref, sc_ref, g_ref, w_ref = refs[n_x:n_x + 4]
    _norm1_project(_x_tile(geom, refs[:n_x]), sh_ref, sc_ref, g_ref, w_ref, refs[n_x + 4:])


def _mod_spec(geom, tm, k):
    return pl.BlockSpec((None, 1, D_MODEL), lambda i: (geom.batch(i, tm) * N_MOD + k, 0, 0))


def _inproj(geom, x_parts, mods, g1, w_in_p):
    tm = geom.tile(512)
    nw = w_in_p.shape[1]
    return pl.pallas_call(
        functools.partial(_inproj_kernel, geom, len(x_parts)),
        grid=(geom.t // tm,),
        in_specs=_x_specs(geom, tm, x_parts) + [
                  _mod_spec(geom, tm, 0), _mod_spec(geom, tm, 1),
                  pl.BlockSpec((1, D_MODEL), lambda i: (0, 0)),
                  pl.BlockSpec((D_MODEL, nw), lambda i: (0, 0))],
        out_specs=[pl.BlockSpec((tm, w), lambda i: (i, 0)) for _, w in _IN_SEGS],
        out_shape=[jax.ShapeDtypeStruct((geom.t, w), BF16) for _, w in _IN_SEGS],
        compiler_params=_cparams(("parallel",)),
        name="norm1_inproj",
    )(*x_parts, mods, mods, g1, w_in_p)


def _swa_kernel(geom, nb, q_ref, kp_ref, kc_ref, kn_ref, vp_ref, vc_ref, vn_ref, gq_ref, gk_ref,
                bdq_ref, bdk_ref, sink_ref, bias_ref, o_ref):
    i = pl.program_id(0)
    first = (geom.pos(i, nb * BLOCK) == 0).astype(jnp.int32)
    last = geom.is_last(i, nb * BLOCK).astype(jnp.int32)

    q = q_ref[...].astype(F32)
    msq = _dot((q * q).astype(BF16), bdq_ref[...]) * (1.0 / HEAD_DIM)
    q_t = (q * lax.rsqrt(msq + EPS) * gq_ref[...] * (HEAD_DIM ** -0.5 * LOG2E)).T.astype(BF16)

    k_all = jnp.concatenate([kp_ref[...], kc_ref[...], kn_ref[...]], axis=0).astype(F32)
    msk = _dot((k_all * k_all).astype(BF16), bdk_ref[...]) * (1.0 / HEAD_DIM)
    k_all = (k_all * lax.rsqrt(msk + EPS) * gk_ref[...]).astype(BF16)
    v_all = jnp.concatenate([vp_ref[...], vc_ref[...], vn_ref[...]], axis=0).astype(F32)
    ones_rows = 16
    v_t = jnp.concatenate([v_all.T, jnp.ones((ones_rows, v_all.shape[0]), F32)], axis=0).astype(BF16)
    zeros = jnp.zeros((HEAD_DIM, SWA_GROUP * BLOCK), BF16)

    pairs = [(b, g) for b in range(nb) for g in range(SWA_KV_HEADS)]
    scores, maxes = {}, {}
    for b, g in pairs:
        cls = (first if b == 0 else 0) + (2 * last if b == nb - 1 else 0)
        heads = range(g * SWA_GROUP, (g + 1) * SWA_GROUP)
        qg = jnp.concatenate([q_t[h * HEAD_DIM:(h + 1) * HEAD_DIM, b * BLOCK:(b + 1) * BLOCK] for h in heads],
                             axis=1)
        q_pad = jnp.concatenate([qg if j == g else zeros for j in range(SWA_KV_HEADS)], axis=0)
        s = _dot(k_all[b * BLOCK:(b + 3) * BLOCK, :], q_pad) + bias_ref[cls, g]
        scores[b, g] = s
        maxes[b, g] = jnp.maximum(jnp.max(s, axis=0, keepdims=True), sink_ref[g])
    probs = {bg: jnp.exp2(scores[bg] - maxes[bg]).astype(BF16) for bg in pairs}
    outs = {(b, g): _dot(v_t[:, b * BLOCK:(b + 3) * BLOCK], probs[b, g]) for b, g in pairs}
    for b in range(nb):
        rows = []
        for g in range(SWA_KV_HEADS):
            o = outs[b, g]
            denom = o[A_KV:A_KV + 1, :] + jnp.exp2(sink_ref[g] - maxes[b, g])
            og = o[g * HEAD_DIM:(g + 1) * HEAD_DIM, :] / denom
            rows += [og[:, j * BLOCK:(j + 1) * BLOCK] for j in range(SWA_GROUP)]
        o_ref[b * BLOCK:(b + 1) * BLOCK, :] = jnp.concatenate(rows, axis=0).T.astype(o_ref.dtype)


def _swa_bias():
    k = np.arange(3 * BLOCK)[:, None]
    q = np.arange(BLOCK)[None, :]
    rel = np.abs(k - BLOCK - q)
    out = np.zeros((4, SWA_KV_HEADS, 3 * BLOCK, SWA_GROUP * BLOCK), np.float32)
    for c in range(4):
        k_lo = BLOCK if c & 1 else 0
        k_hi = 2 * BLOCK if c & 2 else 3 * BLOCK
        valid = (rel <= WINDOW) & (k >= k_lo) & (k < k_hi)
        for g in range(SWA_KV_HEADS):
            for j in range(SWA_GROUP):
                slope = 2.0 ** (-8.0 * (g * SWA_GROUP + j + 1) / SWA_HEADS)
                out[c, g, :, j * BLOCK:(j + 1) * BLOCK] = np.where(valid, -slope * LOG2E * rel, NEG_INF)
    return jnp.asarray(out)


def _swap_rope_halves(a):
    lo, hi = MLA_NOPE, MLA_NOPE + MLA_ROPE // 2
    z = jnp.zeros_like(a)
    return jnp.concatenate([z[..., :lo], a[..., hi:MLA_QK], a[..., lo:hi], z[..., MLA_QK:]], axis=-1)


def _mla_prep_kernel(cq_ref, ckv_ref, kr_ref, gqn_ref, gkvn_ref, wq_ref, wkv_ref, wvt_ref, perm_ref, ones_ref,
                     gq_ref, gqs_ref, gk_ref, gks_ref, c_ref, s_ref, q_out, k_out, vt_out):
    cq = cq_ref[...].astype(F32)
    qn = (cq * lax.rsqrt(jnp.mean(cq * cq, axis=-1, keepdims=True) + EPS) * gqn_ref[...]).astype(BF16)
    ckv = ckv_ref[...].astype(F32)
    kvn = (ckv * lax.rsqrt(jnp.mean(ckv * ckv, axis=-1, keepdims=True) + EPS) * gkvn_ref[...]).astype(BF16)
    kr_b = kr_ref[...]
    kr = kr_b.astype(F32)
    lane = lax.broadcasted_iota(jnp.int32, (1, LANES), 1)
    ones = jnp.ones((MLA_VT_ROWS - MLA_V, cq.shape[0]), BF16)
    c = c_ref[...]
    sn = s_ref[...]
    q_c = c * (gq_ref[...] * (MLA_QK ** -0.5 * LOG2E))
    q_s = sn * (gqs_ref[...] * (MLA_QK ** -0.5 * LOG2E))
    k_c = c * gk_ref[...]
    k_rot = _dot(kr_b, perm_ref[...]) * (sn * gks_ref[...])
    inv_d = 1.0 / MLA_QK
    for hp in range(MLA_HEADS // 2):
        kv2 = _dot(kvn, wkv_ref[:, hp * 2 * LANES:(hp + 1) * 2 * LANES])
        for hh in range(2):
            h = 2 * hp + hh
            xq = _dot(qn, wq_ref[:, h * 2 * LANES:(h + 1) * 2 * LANES])
            x, xs = xq[:, :LANES], xq[:, LANES:]
            xk = jnp.where(lane < MLA_NOPE, kv2[:, hh * LANES:(hh + 1) * LANES], 0.0) + kr
            ss = _dot(jnp.concatenate([x * x, xk * xk], axis=1).astype(BF16), ones_ref[...])
            rq = lax.rsqrt(ss[:, :LANES] * inv_d + EPS)
            rk = lax.rsqrt(ss[:, LANES:] * inv_d + EPS)
            q_out[h] = (rq * (x * q_c + xs * q_s)).astype(BF16)
            k_out[h] = (rk * (xk * k_c + k_rot)).astype(BF16)
            vt_out[h, 0] = jnp.concatenate([_dot_nt(wvt_ref[h], kvn).astype(BF16), ones], axis=0)


def _mla_prep(geom, tk, cq, ckv, kr, gqn, gkvn, wq2, wkv, wvt, perm, ones2, gq_p, gq_s, gk_p, gk_s, rope_c, rope_s):
    tm = tk
    tok = lambda w: pl.BlockSpec((tm, w), lambda i: (i, 0))
    const = lambda shape: pl.BlockSpec(shape, lambda i: (0,) * len(shape))
    rope = pl.BlockSpec((tm, LANES), lambda i: (geom.pos(i, tm), 0))
    hm = pl.BlockSpec((MLA_HEADS, tm, LANES), lambda i: (0, i, 0))
    hm_t = pl.BlockSpec((MLA_HEADS, 1, MLA_VT_ROWS, tm), lambda i: (0, i, 0, 0))
    return pl.pallas_call(
        _mla_prep_kernel,
        grid=(geom.t // tm,),
        in_specs=[tok(MLA_Q_RANK), tok(MLA_KV_RANK), tok(LANES),
                  const((1, MLA_Q_RANK)), const((1, MLA_KV_RANK)),
                  const(wq2.shape), const(wkv.shape), const(wvt.shape), const(perm.shape), const(ones2.shape),
                  const((1, LANES)), const((1, LANES)), const((1, LANES)), const((1, LANES)), rope, rope],
        out_specs=[hm, hm, hm_t],
        out_shape=[jax.ShapeDtypeStruct((MLA_HEADS, geom.t, LANES), BF16),
                   jax.ShapeDtypeStruct((MLA_HEADS, geom.t, LANES), BF16),
                   jax.ShapeDtypeStruct((MLA_HEADS, geom.t // tm, MLA_VT_ROWS, tm), BF16)],
        compiler_params=_cparams(("parallel",)),
        name="mla_prep",
    )(cq, ckv, kr, gqn, gkvn, wq2, wkv, wvt, perm, ones2, gq_p, gq_s, gk_p, gk_s, rope_c, rope_s)


def _mla_attn_kernel(nk, q_ref, k_ref, vt_ref, o_ref, s_ref, mc_ref, p_ref, al_ref, acc_ref):
    tk = vt_ref.shape[3]
    tq = q_ref.shape[1]

    def stage_a(chunk, slot):
        off = chunk * tk
        if not isinstance(off, int):
            off = pl.multiple_of(off, tk)
        for hh in range(2):
            s = _dot_nt(k_ref[hh, pl.ds(off, tk), :], q_ref[hh])
            s_ref[slot, hh] = s
            mc_ref[slot, hh] = jnp.max(s, axis=0, keepdims=True)

    def stage_b(slot, m):
        m_out = []
        for hh in range(2):
            m_new = jnp.maximum(m[hh], mc_ref[slot, hh])
            al_ref[slot, hh] = jnp.exp2(m[hh] - m_new)
            p_ref[slot, hh] = jnp.exp2(s_ref[slot, hh] - m_new).astype(BF16)
            m_out.append(m_new)
        return tuple(m_out)

    def stage_c(chunk, slot):
        for hh in range(2):
            acc_ref[hh] = acc_ref[hh] * al_ref[slot, hh] + _dot(vt_ref[hh, chunk], p_ref[slot, hh])

    stage_a(0, 0)
    p_ref[1] = jnp.zeros(p_ref.shape[1:], BF16)
    al_ref[1] = jnp.ones(al_ref.shape[1:], F32)
    acc_ref[...] = jnp.zeros(acc_ref.shape, F32)
    m = (jnp.full((1, tq), NEG_INF, F32),) * 2

    def body(jj, m):
        j = 2 * jj
        stage_a(j + 1, 1)
        m = stage_b(0, m)
        stage_c(jnp.maximum(j - 1, 0), 1)
        stage_a(j + 2, 0)
        m = stage_b(1, m)
        stage_c(j, 0)
        return m

    m = lax.fori_loop(0, nk // 2 - 1, body, m)
    stage_a(nk - 1, 1)
    m = stage_b(0, m)
    stage_c(max(nk - 3, 0), 1)
    m = stage_b(1, m)
    stage_c(nk - 2, 0)
    stage_c(nk - 1, 1)
    o_t = jnp.concatenate([acc_ref[hh][:MLA_V, :] / acc_ref[hh][MLA_V:MLA_V + 1, :] for hh in range(2)], axis=0)
    o_ref[...] = o_t.T.astype(o_ref.dtype)


def _mla_attn_group(geom, tk, q, k, vt, out_prev, nseq, s, tok0):
    tq = min(1024, s)
    nq = s // tq
    nk = s // tk
    assert nk % 2 == 0
    qb0 = tok0 // tq
    sb0 = tok0 // s
    args = [q, k, vt]
    in_specs = [pl.BlockSpec((2, tq, LANES), lambda b, hp, i: (hp, qb0 + b * nq + i, 0)),
                pl.BlockSpec((2, s, LANES), lambda b, hp, i: (hp, sb0 + b, 0)),
                pl.BlockSpec((2, nk, MLA_VT_ROWS, tk), lambda b, hp, i: (hp, sb0 + b, 0, 0))]
    aliases = {}
    kern = functools.partial(_mla_attn_kernel, nk)
    if out_prev is not None:
        args.append(out_prev)
        in_specs.append(pl.BlockSpec(memory_space=pl.ANY))
        aliases = {3: 0}
        kern = lambda q_ref, k_ref, vt_ref, prev_ref, *rest, _k=kern: _k(q_ref, k_ref, vt_ref, *rest)
    return pl.pallas_call(
        kern,
        grid=(nseq, MLA_HEADS // 2, nq),
        in_specs=in_specs,
        out_specs=pl.BlockSpec((tq, LANES), lambda b, hp, i: (qb0 + b * nq + i, hp)),
        out_shape=jax.ShapeDtypeStruct((geom.t, MLA_HEADS * MLA_V), BF16),
        scratch_shapes=[pltpu.VMEM((2, 2, tk, tq), F32), pltpu.VMEM((2, 2, 1, tq), F32),
                        pltpu.VMEM((2, 2, tk, tq), BF16), pltpu.VMEM((2, 2, 1, tq), F32),
                        pltpu.VMEM((2, MLA_VT_ROWS, tq), F32)],
        input_output_aliases=aliases,
        compiler_params=_cparams(("parallel", "parallel", "arbitrary")),
        name="mla_attention",
    )(*args)


def _mla_attn(geom, tk, q, k, vt):
    assert geom.tp % geom.ss == 0
    out = _mla_attn_group(geom, tk, q, k, vt, None, geom.bp, geom.sp, 0)
    return _mla_attn_group(geom, tk, q, k, vt, out, geom.bs, geom.ss, geom.tp)


def _conv_kernel(geom, tc, up_ref, uc_ref, un_ref, w_ref, b_ref, gng_ref, gnb_ref, bd_ref, wpw_ref, bpw_ref,
                 o_ref, hs_ref):
    i = pl.program_id(0)
    first = geom.pos(i, tc) == 0
    last = geom.is_last(i, tc)

    def glu(u):
        u = u.astype(F32)
        return u[:, :CONV_CH] * jax.nn.sigmoid(u[:, CONV_CH:])

    hs_ref[0, 0:HALO, :] = jnp.where(first, 0.0, glu(up_ref[...]))
    hs_ref[0, HALO:HALO + tc, :] = glu(uc_ref[...])
    hs_ref[0, HALO + tc:HALO + tc + HALO, :] = jnp.where(last, 0.0, glu(un_ref[...]))
    span = tc + 2 * HALO - SUBLANES
    for sft in range(1, SUBLANES):
        hs_ref[sft, 0:span, :] = hs_ref[0, sft:sft + span, :]

    rows = 64 if tc % 64 == 0 else tc
    chunks = list(range(0, tc, rows))
    accs = {}
    for g0 in range(0, len(chunks), 2):
        group = chunks[g0:g0 + 2]
        for r0 in group:
            accs[r0] = jnp.zeros((rows, CONV_CH), F32) + b_ref[...]
        for j in range(CONV_WIDTH):
            for r0 in group:
                start = HALO + r0 + j - CONV_PAD
                sft = start % SUBLANES
                accs[r0] = accs[r0] + hs_ref[sft, start - sft:start - sft + rows, :] * w_ref[j:j + 1, :]
    for r0 in chunks:
        acc = accs[r0]
        mu = _split_dot(acc, bd_ref[...]) * (CONV_GROUPS / CONV_CH)
        d = acc - mu
        var = _split_dot(d * d, bd_ref[...]) * (CONV_GROUPS / CONV_CH)
        hn = d * lax.rsqrt(var + EPS) * gng_ref[...] + gnb_ref[...]
        y = _dot(_silu(hn).astype(BF16), wpw_ref[...]) + bpw_ref[...]
        o_ref[r0:r0 + rows, :] = y.astype(o_ref.dtype)


def _swa_conv_kernel(geom, nb, tc, *refs):
    n_swa = 13
    swa_in, conv_in = refs[:n_swa], refs[n_swa:n_swa + 10]
    oa_ref, oc_ref, hs_ref = refs[n_swa + 10:]
    _conv_kernel(geom, tc, *conv_in, oc_ref, hs_ref)
    _swa_kernel(geom, nb, *swa_in, oa_ref)


def _swa_conv(geom, q, k, v, gq, gk, bdq, bdk, sink, bias, uc, dw_w, dw_b, gn_g, gn_b, bd, w_pw2, b_pw2):
    ts = geom.tile(512)
    nb = ts // BLOCK
    nblk = geom.t // BLOCK
    hb = ts // HALO
    nh = geom.t // HALO
    cur = lambda i: (i, 0)
    halo = lambda f: pl.BlockSpec((BLOCK, A_KV), f)
    prev = lambda i: (jnp.maximum(i * nb - 1, 0), 0)
    nxt = lambda i: (jnp.minimum((i + 1) * nb, nblk - 1), 0)
    const = lambda shape: pl.BlockSpec(shape, lambda i: (0,) * len(shape))
    return pl.pallas_call(
        functools.partial(_swa_conv_kernel, geom, nb, ts),
        grid=(geom.t // ts,),
        in_specs=[pl.BlockSpec((ts, A_Q), cur),
                  halo(prev), pl.BlockSpec((ts, A_KV), cur), halo(nxt),
                  halo(prev), pl.BlockSpec((ts, A_KV), cur), halo(nxt),
                  const((1, A_Q)), const((1, A_KV)), const((A_Q, A_Q)), const((A_KV, A_KV)),
                  const((SWA_KV_HEADS, 1, SWA_GROUP * BLOCK)), const(bias.shape),
                  pl.BlockSpec((HALO, 2 * CONV_CH), lambda i: (jnp.maximum(i * hb - 1, 0), 0)),
                  pl.BlockSpec((ts, 2 * CONV_CH), cur),
                  pl.BlockSpec((HALO, 2 * CONV_CH), lambda i: (jnp.minimum((i + 1) * hb, nh - 1), 0)),
                  const((32, CONV_CH)), const((1, CONV_CH)), const((1, CONV_CH)), const((1, CONV_CH)),
                  const((CONV_CH, CONV_CH)), const((CONV_CH, CONV_CH)), const((1, CONV_CH))],
        out_specs=[pl.BlockSpec((ts, A_Q), cur), pl.BlockSpec((ts, CONV_CH), cur)],
        out_shape=[jax.ShapeDtypeStruct((geom.t, A_Q), BF16), jax.ShapeDtypeStruct((geom.t, CONV_CH), BF16)],
        scratch_shapes=[pltpu.VMEM((SUBLANES, ts + 2 * HALO, CONV_CH), F32)],
        compiler_params=_cparams(("parallel",)),
        name="swa_conv",
    )(q, k, k, k, v, v, v, gq, gk, bdq, bdk, sink, bias, uc, uc, uc, dw_w, dw_b, gn_g, gn_b, bd, w_pw2, b_pw2)


def _rms_rows(x, g):
    return x * lax.rsqrt(jnp.mean(x * x, axis=-1, keepdims=True) + EPS) * g


def _store_slabs(ref, x, base=0):
    rows, half = x.shape[0], x.shape[1] // 2
    hi = pltpu.bitcast(x[:, :half].astype(BF16).astype(F32), U32)
    lo = pltpu.bitcast(x[:, half:].astype(BF16).astype(F32), U32)
    words = jnp.bitwise_or(hi, jnp.right_shift(lo, jnp.uint32(16)))
    for c in range(SLAB):
        ref[pl.ds(base + c, rows, stride=SLAB), :] = words[:, c * LANES:(c + 1) * LANES]


def _load_slabs(ref, rows, base=0):
    words = jnp.concatenate([ref[pl.ds(base + c, rows, stride=SLAB), :] for c in range(SLAB)], axis=1)
    hi = pltpu.bitcast(jnp.bitwise_and(words, jnp.uint32(0xFFFF0000)), F32)
    lo = pltpu.bitcast(jnp.left_shift(words, jnp.uint32(16)), F32)
    return jnp.concatenate([hi, lo], axis=1)


def _slab_row(ref, r):
    return ref.at[pl.ds(pl.multiple_of(r * SLAB, SLAB), SLAB)]


def _outproj_kernel(geom, n_x, *refs):
    x_refs = refs[:n_x]
    (oa_ref, ob_ref, oc_ref, ga_ref, gb_ref, gc_ref, w_ref, gate_ref, sh_ref, sc_ref,
     g2_ref, wr_hi_ref, br_ref, x1_ref, h2_ref, lg_ref) = refs[n_x:]
    na = _rms_rows(oa_ref[...].astype(F32), ga_ref[...]).astype(BF16)
    nb = _rms_rows(ob_ref[...].astype(F32), gb_ref[...]).astype(BF16)
    nc = _rms_rows(oc_ref[...].astype(F32), gc_ref[...]).astype(BF16)
    wa = A_Q
    wb = wa + MLA_HEADS * MLA_V
    y = _dot(na, w_ref[0:wa, :]) + _dot(nb, w_ref[wa:wb, :]) + _dot(nc, w_ref[wb:, :])
    x1 = _x_tile(geom, x_refs) + gate_ref[...] * y
    x1_ref[...] = x1
    h2 = _rms_rows(x1, g2_ref[...]) * (1.0 + sc_ref[...]) + sh_ref[...]
    hi = h2.astype(BF16)
    lo = (h2 - hi.astype(F32)).astype(BF16)
    _store_slabs(h2_ref, h2)
    lg2 = _dot(hi, wr_hi_ref[...])
    lg = lg2[:, :LANES] + lg2[:, LANES:] + _dot(lo, wr_hi_ref[:, :LANES]) + br_ref[...]
    lg_ref[...] = lg.T


def _outproj(geom, x_parts, oa, ob, oc, ga, gb, gc, w_out, mods, g2, wr2, br):
    tm = geom.tile(512)
    tok = lambda w: pl.BlockSpec((tm, w), lambda i: (i, 0))
    const = lambda shape: pl.BlockSpec(shape, lambda i: (0, 0))
    wb = MLA_HEADS * MLA_V
    return pl.pallas_call(
        functools.partial(_outproj_kernel, geom, len(x_parts)),
        grid=(geom.t // tm,),
        in_specs=_x_specs(geom, tm, x_parts) + [tok(A_Q), tok(wb), tok(CONV_CH),
                  const((1, A_Q)), const((1, wb)), const((1, CONV_CH)),
                  const((D_MODEL, D_MODEL)),
                  _mod_spec(geom, tm, 2), _mod_spec(geom, tm, 3), _mod_spec(geom, tm, 4),
                  const((1, D_MODEL)), const((D_MODEL, 2 * LANES)), const((1, LANES))],
        out_specs=[tok(D_MODEL), pl.BlockSpec((tm * SLAB, LANES), lambda i: (i, 0)),
                   pl.BlockSpec((LANES, tm), lambda i: (0, i))],
        out_shape=[jax.ShapeDtypeStruct((geom.t, D_MODEL), F32),
                   jax.ShapeDtypeStruct((geom.t * SLAB, LANES), U32),
                   jax.ShapeDtypeStruct((LANES, geom.t), F32)],
        compiler_params=_cparams(("parallel",)),
        name="merge_outproj_norm2",
    )(*x_parts, oa, ob, oc, ga, gb, gc, w_out, mods, mods, mods, g2, wr2, br)


ROUTE_ROWS = 40


def _route_kernel(lg_ref, sel_ref, rt_ref, cnt_ref):
    lg = lg_ref[...]
    row = lax.broadcasted_iota(jnp.int32, lg.shape, 0)
    rowf = row.astype(F32)
    big = float(2 * LANES)
    is_g = jnp.logical_and(row >= N_EXPERTS, row < N_EXPERTS + N_GROUPS)
    gl = jnp.where(is_g, lg, NEG_INF)
    gmax = jnp.max(gl, axis=0, keepdims=True)
    gsum = jnp.sum(jnp.exp(gl - gmax), axis=0, keepdims=True)
    g_val = 1.0 / gsum
    g_idx = jnp.min(jnp.where(jnp.logical_and(is_g, gl == gmax), rowf, big), axis=0, keepdims=True) - N_EXPERTS
    lo = g_idx * EXPERTS_PER_GROUP
    is_e = jnp.logical_and(rowf >= lo, rowf < lo + EXPERTS_PER_GROUP)
    el = jnp.where(is_e, lg, NEG_INF)
    emax = jnp.max(el, axis=0, keepdims=True)
    ee = jnp.exp(el - emax)
    e_prob = ee / jnp.sum(ee, axis=0, keepdims=True)
    p1 = jnp.where(is_e, e_prob, -1.0)
    v1 = jnp.max(p1, axis=0, keepdims=True)
    i1 = jnp.min(jnp.where(p1 == v1, rowf, big), axis=0, keepdims=True)
    p2 = jnp.where(rowf == i1, -1.0, p1)
    v2 = jnp.max(p2, axis=0, keepdims=True)
    i2 = jnp.min(jnp.where(p2 == v2, rowf, big), axis=0, keepdims=True)
    scale = g_val / (v1 + v2)
    r8 = lax.broadcasted_iota(jnp.int32, rt_ref.shape, 0)
    rt_ref[...] = (jnp.where(r8 == 0, i1, 0.0) + jnp.where(r8 == 1, i2, 0.0) +
                   jnp.where(r8 == 2, v1 * scale, 0.0) + jnp.where(r8 == 3, v2 * scale, 0.0))
    chosen = jnp.where(jnp.logical_or(rowf == i1, rowf == i2), 1.0, 0.0).astype(BF16)
    cnt_ref[...] = _dot(chosen, sel_ref[...])


def _route(geom, tr, logits_t, sel):
    n = geom.t // tr
    return pl.pallas_call(
        _route_kernel,
        grid=(n,),
        in_specs=[pl.BlockSpec((ROUTE_ROWS, tr), lambda i: (0, i)), pl.BlockSpec(sel.shape, lambda i: (0, 0))],
        out_specs=[pl.BlockSpec((SUBLANES, tr), lambda i: (0, i)),
                   pl.BlockSpec((None, ROUTE_ROWS, LANES), lambda i: (i, 0, 0))],
        out_shape=[jax.ShapeDtypeStruct((SUBLANES, geom.t), F32), jax.ShapeDtypeStruct((n, ROUTE_ROWS, LANES), F32)],
        compiler_params=_cparams(("parallel",)),
        name="moe_route",
    )(logits_t, sel)


def _slots_kernel(tm, rt_ref, st_ref, tri_ref, sl_ref):
    rt = rt_ref[...]
    rowf = lax.broadcasted_iota(jnp.int32, (ROUTE_ROWS, rt.shape[1]), 0).astype(F32)
    oh1 = rowf == rt[0:1, :]
    oh2 = rowf == rt[1:2, :]
    chosen = jnp.where(jnp.logical_or(oh1, oh2), 1.0, 0.0).astype(BF16)
    for j in range(rt.shape[1] // tm):
        cols = slice(j * tm, (j + 1) * tm)
        first = _dot(chosen[:, cols], tri_ref[...]) + st_ref[:, j:j + 1]
        s1 = jnp.sum(jnp.where(oh1[:, cols], first, 0.0), axis=0, keepdims=True)
        s2 = jnp.sum(jnp.where(oh2[:, cols], first, 0.0), axis=0, keepdims=True)
        sl_ref[j] = jnp.concatenate([s1, s2], axis=1).astype(jnp.int32)


def _slots(geom, tr, tm, rt_t, start3, tri):
    ns = tr // tm
    return pl.pallas_call(
        functools.partial(_slots_kernel, tm),
        grid=(geom.t // tr,),
        in_specs=[pl.BlockSpec((SUBLANES, tr), lambda i: (0, i)),
                  pl.BlockSpec((None, ROUTE_ROWS, LANES), lambda i: (i, 0, 0)),
                  pl.BlockSpec((tm, tm), lambda i: (0, 0))],
        out_specs=pl.BlockSpec((ns, 1, 2 * tm), lambda i: (i, 0, 0)),
        out_shape=jax.ShapeDtypeStruct((geom.t // tm, 1, 2 * tm), jnp.int32),
        compiler_params=_cparams(("parallel",)),
        name="moe_slots",
    )(rt_t, start3, tri)


def _plan(cnt3, ns, n_tiles_max):
    n_steps = cnt3.shape[0]
    cnt_t = cnt3[:, :, :ns].transpose(0, 2, 1).reshape(n_steps * ns, ROUTE_ROWS)
    incl = jnp.cumsum(cnt_t, axis=0)
    tiles_e = jnp.ceil(incl[-1] / TME)
    end_e = jnp.cumsum(tiles_e)
    start = incl - cnt_t + ((end_e - tiles_e) * TME)[None, :]
    start3 = jnp.pad(start.reshape(n_steps, ns, ROUTE_ROWS).transpose(0, 2, 1), ((0, 0), (0, 0), (0, LANES - ns)))
    n_used = end_e[-1].astype(jnp.int32)
    tile_idx = jnp.minimum(jnp.arange(n_tiles_max, dtype=jnp.int32), n_used - 1)
    tile_exp = jnp.sum(tile_idx[:, None] >= end_e[None, :N_EXPERTS].astype(jnp.int32), axis=1).astype(jnp.int32)
    return start3, tile_exp, tile_idx, n_used.reshape(1)


def _scatter_kernel(sl_ref, h_ref, xs_in, xs_out, sem):
    del xs_in
    tm = h_ref.shape[0] // SLAB

    def body(rr, carry):
        for j in range(ROW_UNROLL):
            r = rr * ROW_UNROLL + j
            pltpu.make_async_copy(_slab_row(h_ref, r), _slab_row(xs_out, sl_ref[0, r]), sem).start(priority=0)
            pltpu.make_async_copy(_slab_row(h_ref, r), _slab_row(xs_out, sl_ref[0, tm + r]), sem).start(priority=1)
        return carry

    lax.fori_loop(0, tm // ROW_UNROLL, body, 0)
    for _ in range(2):
        pltpu.make_async_copy(h_ref, xs_out.at[pl.ds(0, tm * SLAB)], sem).wait()


def _scatter(geom, tm, slots_smem, h2, xs):
    return pl.pallas_call(
        _scatter_kernel,
        grid=(geom.t // tm,),
        in_specs=[pl.BlockSpec((None, 1, 2 * tm), lambda i: (i, 0, 0), memory_space=pltpu.SMEM),
                  pl.BlockSpec((tm * SLAB, LANES), lambda i: (i, 0)),
                  pl.BlockSpec(memory_space=pl.ANY)],
        out_specs=pl.BlockSpec(memory_space=pl.ANY),
        out_shape=jax.ShapeDtypeStruct(xs.shape, xs.dtype),
        scratch_shapes=[pltpu.SemaphoreType.DMA(())],
        input_output_aliases={2: 0},
        compiler_params=_cparams(("arbitrary",)),
        name="moe_scatter",
    )(slots_smem, h2, xs)


def _experts_kernel(te_ref, ti_ref, nu_ref, xs_ref, wg_ref, wu_ref, wd_ref, ys_ref):
    del te_ref, ti_ref

    @pl.when(pl.program_id(0) < nu_ref[0])
    def _():
        x = _load_slabs(xs_ref, TME).astype(BF16)
        act = (_silu(_dot(x, wg_ref[...].astype(BF16))) * _dot(x, wu_ref[...].astype(BF16))).astype(BF16)
        _store_slabs(ys_ref, _dot(act, wd_ref[...].astype(BF16)))


def _experts(layer, xs, tile_exp, tile_idx, n_used, wg, wu, wd):
    n_tiles_max = xs.shape[0] // (TME * SLAB)
    slab = pl.BlockSpec((TME * SLAB, LANES), lambda n, te, ti, nu: (ti[n], 0))
    wspec = lambda a, b: pl.BlockSpec((None, None, a, b), lambda n, te, ti, nu: (layer, te[n], 0, 0))
    return pl.pallas_call(
        _experts_kernel,
        grid_spec=pltpu.PrefetchScalarGridSpec(
            num_scalar_prefetch=3,
            grid=(n_tiles_max,),
            in_specs=[slab, wspec(D_MODEL, D_EXPERT), wspec(D_MODEL, D_EXPERT), wspec(D_EXPERT, D_MODEL)],
            out_specs=slab),
        out_shape=jax.ShapeDtypeStruct(xs.shape, U32),
        compiler_params=_cparams(("arbitrary",)),
        name="moe_experts",
    )(tile_exp, tile_idx, n_used, xs, wg, wu, wd)


def _combine_kernel(geom, final, sl_cur, sl_nxt, rt_ref, x1_ref, gate_ref, ys_hbm, *refs):
    buf1, buf2, sem = refs[-3:]
    i = pl.program_id(0)
    tm = x1_ref.shape[0]

    def issue_row(sl_ref, slot, r):
        dst = slot * tm + r
        pltpu.make_async_copy(_slab_row(ys_hbm, sl_ref[0, r]), _slab_row(buf1, dst), sem.at[slot]).start(priority=0)
        pltpu.make_async_copy(_slab_row(ys_hbm, sl_ref[0, tm + r]), _slab_row(buf2, dst),
                              sem.at[slot]).start(priority=1)

    def wait_slot(slot):
        base = pl.multiple_of(slot * (tm * SLAB), tm * SLAB)
        for buf in (buf1, buf2):
            pltpu.make_async_copy(ys_hbm.at[pl.ds(0, tm * SLAB)], buf.at[pl.ds(base, tm * SLAB)],
                                  sem.at[slot]).wait()
        return base

    @pl.when(i == 0)
    def _():
        def body(rr, carry):
            for j in range(ROW_UNROLL):
                issue_row(sl_cur, 0, rr * ROW_UNROLL + j)
            return carry
        lax.fori_loop(0, tm // ROW_UNROLL, body, 0)

    slot = i % 2
    base = wait_slot(slot)
    for r in range(tm):
        issue_row(sl_nxt, 1 - slot, r)
    w = jnp.concatenate([rt_ref[...], jnp.zeros((LANES - SUBLANES, tm), F32)], axis=0).T
    y = w[:, 2:3] * _load_slabs(buf1, tm, base) + w[:, 3:4] * _load_slabs(buf2, tm, base)
    x_new = x1_ref[...] + gate_ref[...] * y
    if final:
        out_p, out_s = refs[:2]
        n_prompt = geom.tp // tm

        @pl.when(i < n_prompt)
        def _():
            out_p[...] = x_new

        @pl.when(i >= n_prompt)
        def _():
            out_s[...] = x_new
    else:
        sh_ref, sc_ref, g_ref, w_ref, o_ref = refs[:5]
        o_ref[...] = x_new
        _norm1_project(x_new, sh_ref, sc_ref, g_ref, w_ref, refs[5:-3])

    @pl.when(i == pl.num_programs(0) - 1)
    def _():
        wait_slot(1 - slot)


def _combine(geom, tm, slots_smem, rt, x1, mods, ys, nxt=None):
    n = geom.t // tm
    n_prompt = geom.tp // tm
    tok = lambda w: pl.BlockSpec((tm, w), lambda i: (i, 0))
    smem = lambda f: pl.BlockSpec((None, 1, 2 * tm), f, memory_space=pltpu.SMEM)
    args = [slots_smem, slots_smem, rt, x1, mods, ys]
    in_specs = [smem(lambda i: (i, 0, 0)), smem(lambda i: (jnp.minimum(i + 1, n - 1), 0, 0)),
                pl.BlockSpec((SUBLANES, tm), lambda i: (0, i)), tok(D_MODEL), _mod_spec(geom, tm, 5),
                pl.BlockSpec(memory_space=pl.ANY)]
    if nxt is None:
        out_specs = [pl.BlockSpec((tm, D_MODEL), lambda i: (jnp.minimum(i, n_prompt - 1), 0)),
                     pl.BlockSpec((tm, D_MODEL), lambda i: (jnp.maximum(i - n_prompt, 0), 0))]
        out_shape = [jax.ShapeDtypeStruct((geom.tp, D_MODEL), F32),
                     jax.ShapeDtypeStruct((geom.t - geom.tp, D_MODEL), F32)]
    else:
        mods_n, g1_n, w_in_n = nxt
        args += [mods_n, mods_n, g1_n, w_in_n]
        in_specs += [_mod_spec(geom, tm, 0), _mod_spec(geom, tm, 1), pl.BlockSpec((1, D_MODEL), lambda i: (0, 0)),
                     pl.BlockSpec(w_in_n.shape, lambda i: (0, 0))]
        out_specs = [tok(D_MODEL)] + [tok(w) for _, w in _IN_SEGS]
        out_shape = [jax.ShapeDtypeStruct((geom.t, D_MODEL), F32)] + \
            [jax.ShapeDtypeStruct((geom.t, w), BF16) for _, w in _IN_SEGS]
    return pl.pallas_call(
        functools.partial(_combine_kernel, geom, nxt is None),
        grid=(n,),
        in_specs=in_specs,
        out_specs=out_specs,
        out_shape=out_shape,
        scratch_shapes=[pltpu.VMEM((2 * tm * SLAB, LANES), U32), pltpu.VMEM((2 * tm * SLAB, LANES), U32),
                        pltpu.SemaphoreType.DMA((2,))],
        compiler_params=_cparams(("arbitrary",)),
        name="moe_combine",
    )(*args)


def _block_diag_ones(n, blk):
    idx = np.arange(n) // blk
    return jnp.asarray((idx[:, None] == idx[None, :]).astype(np.float32), dtype=BF16)


def _rope_tables(smax):
    inv_freq = 1.0 / (ROPE_BASE ** (jnp.arange(0, MLA_ROPE, 2, dtype=F32) / MLA_ROPE))
    ang = jnp.arange(smax, dtype=F32)[:, None] * inv_freq[None, :]
    cos, sin = jnp.cos(ang), jnp.sin(ang)
    zeros = lambda w: jnp.zeros((smax, w), F32)
    pad = LANES - MLA_QK
    c = jnp.concatenate([jnp.ones((smax, MLA_NOPE), F32), cos, cos, zeros(pad)], axis=1)
    s = jnp.concatenate([zeros(MLA_NOPE), -sin, sin, zeros(pad)], axis=1)
    return c, s


def _pad_cols(w, n):
    return jnp.pad(w, ((0, 0), (0, n - w.shape[1])))


def _layer_weights(l, w_in, mla_w_q_up, mla_w_kv_up, mla_q_gain, mla_k_gain, swa_q_gain, swa_k_gain, swa_sink,
                   moe_w_group, moe_b_group, moe_w_expert, moe_b_expert):
    wi = w_in[l]
    kr = jnp.pad(wi[:, A_IN + MLA_Q_RANK + MLA_KV_RANK:A_IN + B_IN], ((0, 0), (MLA_NOPE, LANES - MLA_QK)))
    w_in_p = jnp.concatenate([wi[:, :A_IN], wi[:, A_IN:A_IN + MLA_Q_RANK + MLA_KV_RANK], kr, wi[:, A_IN + B_IN:]],
                             axis=1).astype(BF16)
    wq = mla_w_q_up[l].reshape(MLA_Q_RANK, MLA_HEADS, MLA_QK)
    wq_p = jnp.pad(wq, ((0, 0), (0, 0), (0, LANES - MLA_QK)))
    wq2 = jnp.concatenate([wq_p, _swap_rope_halves(wq_p)], axis=2).reshape(MLA_Q_RANK, MLA_HEADS * 2 * LANES).astype(BF16)
    wvt = mla_w_kv_up[l].reshape(MLA_KV_RANK, MLA_HEADS, MLA_NOPE + MLA_V)[:, :, MLA_NOPE:].transpose(1, 2, 0).astype(BF16)
    gq_p = _pad_cols(mla_q_gain[l][None, :], LANES)
    gk_p = _pad_cols(mla_k_gain[l][None, :], LANES)
    gq_s, gk_s = _swap_rope_halves(gq_p), _swap_rope_halves(gk_p)
    gq_a = jnp.tile(swa_q_gain[l], SWA_HEADS)[None, :]
    gk_a = jnp.tile(swa_k_gain[l], SWA_KV_HEADS)[None, :]
    sink = jnp.repeat(swa_sink[l] * LOG2E, BLOCK).reshape(SWA_KV_HEADS, 1, SWA_GROUP * BLOCK)
    wr = _pad_cols(jnp.concatenate([moe_w_expert[l], moe_w_group[l]], axis=1), LANES)
    wr_hi = wr.astype(BF16)
    wr2 = jnp.concatenate([wr_hi, (wr - wr_hi.astype(F32)).astype(BF16)], axis=1)
    br = _pad_cols(jnp.concatenate([moe_b_expert[l], moe_b_group[l]])[None, :], LANES)
    return w_in_p, wq2, wvt, gq_p, gq_s, gk_p, gk_s, gq_a, gk_a, sink, wr2, br


def kernel(x_prompt, x_sample, c_prompt, c_sample, w_ada, b_ada, norm1_g, norm2_g, w_in, swa_q_gain, swa_k_gain, swa_sink, mla_q_norm_g, mla_w_q_up, mla_kv_norm_g, mla_w_kv_up, mla_q_gain, mla_k_gain, conv_dw_w, conv_dw_b, conv_gn_g, conv_gn_b, conv_w_pw2, conv_b_pw2, out_norm_a, out_norm_b, out_norm_c, w_out, moe_w_group, moe_b_group, moe_w_expert, moe_b_expert, moe_w_gate, moe_w_up, moe_w_down):
    bp, sp, d = x_prompt.shape
    bs, ss, _ = x_sample.shape
    assert d == D_MODEL
    geom = _Geom(bp, sp, bs, ss)
    depth = w_ada.shape[0]

    x_parts = (x_prompt.reshape(bp * sp, d), x_sample.reshape(bs * ss, d))
    c = jnp.concatenate([c_prompt, c_sample], axis=0)
    rows = -(-geom.nb // 8) * 8
    c_pad = jnp.pad(c, ((0, rows - geom.nb), (0, 0)))
    mods_all = _modulation(c_pad, w_ada, b_ada)

    rope_c, rope_s = _rope_tables(max(sp, ss))
    rope_perm = _swap_rope_halves(jnp.eye(LANES, dtype=F32)).astype(BF16)
    ones2 = _block_diag_ones(2 * LANES, LANES)
    bdq = _block_diag_ones(A_Q, HEAD_DIM)
    bdk = _block_diag_ones(A_KV, HEAD_DIM)
    bdc = _block_diag_ones(CONV_CH, CONV_CH // CONV_GROUPS)
    swa_bias = _swa_bias()
    row = lambda v: v[None, :]
    tm_moe = geom.tile(256)
    n_tiles_max = 2 * geom.t // TME + N_EXPERTS
    tr_moe = geom.tile(2048)
    tri = jnp.asarray(np.triu(np.ones((tm_moe, tm_moe), np.float32), 1), dtype=BF16)
    sel = jnp.asarray(np.arange(tr_moe)[:, None] // tm_moe == np.arange(LANES)[None, :], dtype=BF16)
    xs = jnp.zeros((n_tiles_max * TME * SLAB, LANES), U32)

    weights = [_layer_weights(l, w_in, mla_w_q_up, mla_w_kv_up, mla_q_gain, mla_k_gain, swa_q_gain, swa_k_gain,
                              swa_sink, moe_w_group, moe_b_group, moe_w_expert, moe_b_expert) for l in range(depth)]
    mods_l = [mods_all[l, :geom.nb].reshape(geom.nb * N_MOD, 1, D_MODEL) for l in range(depth)]
    proj = _inproj(geom, x_parts, mods_l[0], row(norm1_g[0]), weights[0][0])

    for l in range(depth):
        (w_in_p, wq2, wvt, gq_p, gq_s, gk_p, gk_s, gq_a, gk_a, sink, wr2, br) = weights[l]
        mods = mods_l[l]
        q_a, k_a, v_a, cq, ckv, kr, uc = proj
        tk = geom.tile(512)
        q_b, k_b, vt_b = _mla_prep(geom, tk, cq, ckv, kr, row(mla_q_norm_g[l]), row(mla_kv_norm_g[l]), wq2,
                                   mla_w_kv_up[l].astype(BF16), wvt, rope_perm, ones2, gq_p, gq_s, gk_p, gk_s,
                                   rope_c, rope_s)
        out_b = _mla_attn(geom, tk, q_b, k_b, vt_b)
        dw_w = jnp.pad(conv_dw_w[l].reshape(CONV_WIDTH, CONV_CH), ((0, 32 - CONV_WIDTH), (0, 0)))
        out_a, out_c = _swa_conv(geom, q_a, k_a, v_a, gq_a, gk_a, bdq, bdk, sink, swa_bias, uc, dw_w,
                                 row(conv_dw_b[l]), row(conv_gn_g[l]), row(conv_gn_b[l]), bdc,
                                 conv_w_pw2[l].astype(BF16), row(conv_b_pw2[l]))
        x1, h2, logits = _outproj(geom, x_parts, out_a, out_b, out_c, row(out_norm_a[l]), row(out_norm_b[l]),
                                  row(out_norm_c[l]), w_out[l].astype(BF16), mods, row(norm2_g[l]),
                                  wr2, br)
        rt, cnt3 = _route(geom, tr_moe, logits, sel)
        start3, tile_exp, tile_idx, n_used = _plan(cnt3, tr_moe // tm_moe, n_tiles_max)
        slots = _slots(geom, tr_moe, tm_moe, rt, start3, tri)
        xs = _scatter(geom, tm_moe, slots, h2, xs)
        ys = _experts(l, xs, tile_exp, tile_idx, n_used, moe_w_gate, moe_w_up, moe_w_down)
        if l + 1 < depth:
            x_new, *proj = _combine(geom, tm_moe, slots, rt, x1, mods, ys,
                                nxt=(mods_l[l + 1], row(norm1_g[l + 1]), weights[l + 1][0]))
            x_parts = (x_new,)
        else:
            y_prompt, y_sample = _combine(geom, tm_moe, slots, rt, x1, mods, ys)

    return (y_prompt.reshape(bp, sp, d), y_sample.reshape(bs, ss, d))
```

```python
import functools

import numpy as np
import jax
import jax.numpy as jnp
from jax import lax
from jax.experimental import pallas as pl
from jax.experimental.pallas import tpu as pltpu

F32 = jnp.float32
BF16 = jnp.bfloat16

EPS = 1e-6
NEG_INF = -1e30
LOG2E = 1.4426950408889634

D_MODEL = 1024
HEAD_DIM = 64
SWA_HEADS = 6
SWA_KV_HEADS = 2
SWA_GROUP = SWA_HEADS // SWA_KV_HEADS
WINDOW = 128
BLOCK = WINDOW
MLA_HEADS = 6
MLA_Q_RANK = 256
MLA_KV_RANK = 128
MLA_NOPE = 64
MLA_ROPE = 32
MLA_QK = MLA_NOPE + MLA_ROPE
MLA_V = 64
MLA_VT_ROWS = MLA_V + 16
ROPE_BASE = 10000.0
CONV_CH = 256
CONV_GROUPS = 4
CONV_WIDTH = 31
CONV_PAD = (CONV_WIDTH - 1) // 2
A_Q = SWA_HEADS * HEAD_DIM
A_KV = SWA_KV_HEADS * HEAD_DIM
A_IN = A_Q + 2 * A_KV
B_IN = MLA_Q_RANK + MLA_KV_RANK + MLA_ROPE
N_GROUPS = 4
EXPERTS_PER_GROUP = 8
N_EXPERTS = N_GROUPS * EXPERTS_PER_GROUP
D_EXPERT = 256
N_MOD = 6

LANES = 128
SUBLANES = 8
SLAB = D_MODEL // (2 * 128)
U32 = jnp.uint32
TME = 512
ROW_UNROLL = 8
HALO = 16
VMEM_LIMIT = 48 * 1024 * 1024


class _Geom:
    def __init__(self, bp, sp, bs, ss):
        self.bp, self.sp, self.bs, self.ss = bp, sp, bs, ss
        self.tp = bp * sp
        self.t = bp * sp + bs * ss
        self.nb = bp + bs

    def tile(self, target):
        t = target
        while self.sp % t or self.ss % t:
            t //= 2
        return t

    def batch(self, i, tm):
        npt = self.tp // tm
        return jnp.where(i < npt, i // (self.sp // tm), self.bp + (i - npt) // (self.ss // tm))

    def pos(self, i, tm):
        npt = self.tp // tm
        return jnp.where(i < npt, i % (self.sp // tm), (i - npt) % (self.ss // tm))

    def is_last(self, i, tm):
        npt = self.tp // tm
        return jnp.where(i < npt, i % (self.sp // tm) == self.sp // tm - 1,
                         (i - npt) % (self.ss // tm) == self.ss // tm - 1)


def _cparams(sem):
    return pltpu.CompilerParams(dimension_semantics=sem, vmem_limit_bytes=VMEM_LIMIT)


def _silu(x):
    return x * jax.nn.sigmoid(x)


def _dot(a, b):
    return jnp.dot(a, b, preferred_element_type=F32)


def _dot_nt(a, b):
    return lax.dot_general(a, b, (((1,), (1,)), ((), ())), preferred_element_type=F32)


def _split_dot(x, w):
    hi = x.astype(BF16)
    lo = (x - hi.astype(F32)).astype(BF16)
    return _dot(hi, w) + _dot(lo, w)


def _mod_kernel(c_ref, w_ref, b_ref, o_ref):
    c = c_ref[...]
    o_ref[...] = _dot(_silu(c).astype(BF16), w_ref[...].astype(BF16)) + b_ref[...]


def _modulation(c_pad, w_ada, b_ada):
    L, d, n = w_ada.shape
    tn = 768
    rows = c_pad.shape[0]
    return pl.pallas_call(
        _mod_kernel,
        grid=(L, n // tn),
        in_specs=[pl.BlockSpec((rows, d), lambda l, j: (0, 0)),
                  pl.BlockSpec((None, d, tn), lambda l, j: (l, 0, j)),
                  pl.BlockSpec((None, 1, tn), lambda l, j: (l, 0, j))],
        out_specs=pl.BlockSpec((None, rows, tn), lambda l, j: (l, 0, j)),
        out_shape=jax.ShapeDtypeStruct((L, rows, n), F32),
        compiler_params=_cparams(("arbitrary", "arbitrary")),
        name="adaln_mod",
    )(c_pad, w_ada, b_ada.reshape(L, 1, n))


_IN_SEGS = (("q", A_Q), ("k", A_KV), ("v", A_KV), ("cq", MLA_Q_RANK), ("ckv", MLA_KV_RANK), ("kr", LANES),
            ("uc", 2 * CONV_CH))


def _norm1_project(x, sh_ref, sc_ref, g_ref, w_ref, out_refs):
    ms = jnp.mean(x * x, axis=-1, keepdims=True)
    h = x * lax.rsqrt(ms + EPS) * g_ref[...]
    h = (h * (1.0 + sc_ref[...]) + sh_ref[...]).astype(BF16)
    off = 0
    for (_, width), o_ref in zip(_IN_SEGS, out_refs):
        o_ref[...] = _dot(h, w_ref[:, off:off + width]).astype(o_ref.dtype)
        off += width


def _x_specs(geom, tm, x_parts):
    if len(x_parts) == 1:
        return [pl.BlockSpec((tm, D_MODEL), lambda i: (i, 0))]
    n_prompt = geom.tp // tm
    return [pl.BlockSpec((tm, D_MODEL), lambda i: (jnp.minimum(i, n_prompt - 1), 0)),
            pl.BlockSpec((tm, D_MODEL), lambda i: (jnp.maximum(i - n_prompt, 0), 0))]


def _x_tile(geom, x_refs):
    if len(x_refs) == 1:
        return x_refs[0][...]
    tm = x_refs[0].shape[0]
    return jnp.where(pl.program_id(0) < geom.tp // tm, x_refs[0][...], x_refs[1][...])


def _inproj_kernel(geom, n_x, *refs):
    sh_ref, sc_ref, g_ref, w_ref = refs[n_x:n_x + 4]
    _norm1_project(_x_tile(geom, refs[:n_x]), sh_ref, sc_ref, g_ref, w_ref, refs[n_x + 4:])


def _mod_spec(geom, tm, k):
    return pl.BlockSpec((None, 1, D_MODEL), lambda i: (geom.batch(i, tm) * N_MOD + k, 0, 0))


def _inproj(geom, x_parts, mods, g1, w_in_p):
    tm = geom.tile(512)
    nw = w_in_p.shape[1]
    return pl.pallas_call(
        functools.partial(_inproj_kernel, geom, len(x_parts)),
        grid=(geom.t // tm,),
        in_specs=_x_specs(geom, tm, x_parts) + [
                  _mod_spec(geom, tm, 0), _mod_spec(geom, tm, 1),
                  pl.BlockSpec((1, D_MODEL), lambda i: (0, 0)),
                  pl.BlockSpec((D_MODEL, nw), lambda i: (0, 0))],
        out_specs=[pl.BlockSpec((tm, w), lambda i: (i, 0)) for _, w in _IN_SEGS],
        out_shape=[jax.ShapeDtypeStruct((geom.t, w), BF16) for _, w in _IN_SEGS],
        compiler_params=_cparams(("parallel",)),
        name="norm1_inproj",
    )(*x_parts, mods, mods, g1, w_in_p)


def _swa_kernel(geom, nb, q_ref, kp_ref, kc_ref, kn_ref, vp_ref, vc_ref, vn_ref, gq_ref, gk_ref,
                bdq_ref, bdk_ref, sink_ref, bias_ref, o_ref):
    i = pl.program_id(0)
    first = (geom.pos(i, nb * BLOCK) == 0).astype(jnp.int32)
    last = geom.is_last(i, nb * BLOCK).astype(jnp.int32)

    q = q_ref[...].astype(F32)
    msq = _dot((q * q).astype(BF16), bdq_ref[...]) * (1.0 / HEAD_DIM)
    q_t = (q * lax.rsqrt(msq + EPS) * gq_ref[...] * (HEAD_DIM ** -0.5 * LOG2E)).T.astype(BF16)

    k_all = jnp.concatenate([kp_ref[...], kc_ref[...], kn_ref[...]], axis=0).astype(F32)
    msk = _dot((k_all * k_all).astype(BF16), bdk_ref[...]) * (1.0 / HEAD_DIM)
    k_all = (k_all * lax.rsqrt(msk + EPS) * gk_ref[...]).astype(BF16)
    v_all = jnp.concatenate([vp_ref[...], vc_ref[...], vn_ref[...]], axis=0).astype(F32)
    ones_rows = 16
    v_t = jnp.concatenate([v_all.T, jnp.ones((ones_rows, v_all.shape[0]), F32)], axis=0).astype(BF16)
    zeros = jnp.zeros((HEAD_DIM, SWA_GROUP * BLOCK), BF16)

    pairs = [(b, g) for b in range(nb) for g in range(SWA_KV_HEADS)]
    scores, maxes = {}, {}
    for b, g in pairs:
        cls = (first if b == 0 else 0) + (2 * last if b == nb - 1 else 0)
        heads = range(g * SWA_GROUP, (g + 1) * SWA_GROUP)
        qg = jnp.concatenate([q_t[h * HEAD_DIM:(h + 1) * HEAD_DIM, b * BLOCK:(b + 1) * BLOCK] for h in heads],
                             axis=1)
        q_pad = jnp.concatenate([qg if j == g else zeros for j in range(SWA_KV_HEADS)], axis=0)
        s = _dot(k_all[b * BLOCK:(b + 3) * BLOCK, :], q_pad) + bias_ref[cls, g]
        scores[b, g] = s
        maxes[b, g] = jnp.maximum(jnp.max(s, axis=0, keepdims=True), sink_ref[g])
    probs = {bg: jnp.exp2(scores[bg] - maxes[bg]).astype(BF16) for bg in pairs}
    outs = {(b, g): _dot(v_t[:, b * BLOCK:(b + 3) * BLOCK], probs[b, g]) for b, g in pairs}
    for b in range(nb):
        rows = []
        for g in range(SWA_KV_HEADS):
            o = outs[b, g]
            denom = o[A_KV:A_KV + 1, :] + jnp.exp2(sink_ref[g] - maxes[b, g])
            og = o[g * HEAD_DIM:(g + 1) * HEAD_DIM, :] / denom
            rows += [og[:, j * BLOCK:(j + 1) * BLOCK] for j in range(SWA_GROUP)]
        o_ref[b * BLOCK:(b + 1) * BLOCK, :] = jnp.concatenate(rows, axis=0).T.astype(o_ref.dtype)


def _swa_bias():
    k = np.arange(3 * BLOCK)[:, None]
    q = np.arange(BLOCK)[None, :]
    rel = np.abs(k - BLOCK - q)
    out = np.zeros((4, SWA_KV_HEADS, 3 * BLOCK, SWA_GROUP * BLOCK), np.float32)
    for c in range(4):
        k_lo = BLOCK if c & 1 else 0
        k_hi = 2 * BLOCK if c & 2 else 3 * BLOCK
        valid = (rel <= WINDOW) & (k >= k_lo) & (k < k_hi)
        for g in range(SWA_KV_HEADS):
            for j in range(SWA_GROUP):
                slope = 2.0 ** (-8.0 * (g * SWA_GROUP + j + 1) / SWA_HEADS)
                out[c, g, :, j * BLOCK:(j + 1) * BLOCK] = np.where(valid, -slope * LOG2E * rel, NEG_INF)
    return jnp.asarray(out)


def _swap_rope_halves(a):
    lo, hi = MLA_NOPE, MLA_NOPE + MLA_ROPE // 2
    z = jnp.zeros_like(a)
    return jnp.concatenate([z[..., :lo], a[..., hi:MLA_QK], a[..., lo:hi], z[..., MLA_QK:]], axis=-1)


def _mla_prep_kernel(cq_ref, ckv_ref, kr_ref, gqn_ref, gkvn_ref, wq_ref, wkv_ref, wvt_ref, perm_ref, ones_ref,
                     gq_ref, gqs_ref, gk_ref, gks_ref, c_ref, s_ref, q_out, k_out, vt_out):
    cq = cq_ref[...].astype(F32)
    qn = (cq * lax.rsqrt(jnp.mean(cq * cq, axis=-1, keepdims=True) + EPS) * gqn_ref[...]).astype(BF16)
    ckv = ckv_ref[...].astype(F32)
    kvn = (ckv * lax.rsqrt(jnp.mean(ckv * ckv, axis=-1, keepdims=True) + EPS) * gkvn_ref[...]).astype(BF16)
    kr_b = kr_ref[...]
    kr = kr_b.astype(F32)
    lane = lax.broadcasted_iota(jnp.int32, (1, LANES), 1)
    ones = jnp.ones((MLA_VT_ROWS - MLA_V, cq.shape[0]), BF16)
    c = c_ref[...]
    sn = s_ref[...]
    q_c = c * (gq_ref[...] * (MLA_QK ** -0.5 * LOG2E))
    q_s = sn * (gqs_ref[...] * (MLA_QK ** -0.5 * LOG2E))
    k_c = c * gk_ref[...]
    k_rot = _dot(kr_b, perm_ref[...]) * (sn * gks_ref[...])
    inv_d = 1.0 / MLA_QK
    for hp in range(MLA_HEADS // 2):
        kv2 = _dot(kvn, wkv_ref[:, hp * 2 * LANES:(hp + 1) * 2 * LANES])
        for hh in range(2):
            h = 2 * hp + hh
            xq = _dot(qn, wq_ref[:, h * 2 * LANES:(h + 1) * 2 * LANES])
            x, xs = xq[:, :LANES], xq[:, LANES:]
            xk = jnp.where(lane < MLA_NOPE, kv2[:, hh * LANES:(hh + 1) * LANES], 0.0) + kr
            ss = _dot(jnp.concatenate([x * x, xk * xk], axis=1).astype(BF16), ones_ref[...])
            rq = lax.rsqrt(ss[:, :LANES] * inv_d + EPS)
            rk = lax.rsqrt(ss[:, LANES:] * inv_d + EPS)
            q_out[h] = (rq * (x * q_c + xs * q_s)).astype(BF16)
            k_out[h] = (rk * (xk * k_c + k_rot)).astype(BF16)
            vt_out[h, 0] = jnp.concatenate([_dot_nt(wvt_ref[h], kvn).astype(BF16), ones], axis=0)


def _mla_prep(geom, tk, cq, ckv, kr, gqn, gkvn, wq2, wkv, wvt, perm, ones2, gq_p, gq_s, gk_p, gk_s, rope_c, rope_s):
    tm = tk
    tok = lambda w: pl.BlockSpec((tm, w), lambda i: (i, 0))
    const = lambda shape: pl.BlockSpec(shape, lambda i: (0,) * len(shape))
    rope = pl.BlockSpec((tm, LANES), lambda i: (geom.pos(i, tm), 0))
    hm = pl.BlockSpec((MLA_HEADS, tm, LANES), lambda i: (0, i, 0))
    hm_t = pl.BlockSpec((MLA_HEADS, 1, MLA_VT_ROWS, tm), lambda i: (0, i, 0, 0))
    return pl.pallas_call(
        _mla_prep_kernel,
        grid=(geom.t // tm,),
        in_specs=[tok(MLA_Q_RANK), tok(MLA_KV_RANK), tok(LANES),
                  const((1, MLA_Q_RANK)), const((1, MLA_KV_RANK)),
                  const(wq2.shape), const(wkv.shape), const(wvt.shape), const(perm.shape), const(ones2.shape),
                  const((1, LANES)), const((1, LANES)), const((1, LANES)), const((1, LANES)), rope, rope],
        out_specs=[hm, hm, hm_t],
        out_shape=[jax.ShapeDtypeStruct((MLA_HEADS, geom.t, LANES), BF16),
                   jax.ShapeDtypeStruct((MLA_HEADS, geom.t, LANES), BF16),
                   jax.ShapeDtypeStruct((MLA_HEADS, geom.t // tm, MLA_VT_ROWS, tm), BF16)],
        compiler_params=_cparams(("parallel",)),
        name="mla_prep",
    )(cq, ckv, kr, gqn, gkvn, wq2, wkv, wvt, perm, ones2, gq_p, gq_s, gk_p, gk_s, rope_c, rope_s)


def _mla_attn_kernel(nk, q_ref, k_ref, vt_ref, o_ref, s_ref, mc_ref, p_ref, al_ref, acc_ref):
    tk = vt_ref.shape[3]
    tq = q_ref.shape[1]

    def stage_a(chunk, slot):
        off = chunk * tk
        if not isinstance(off, int):
            off = pl.multiple_of(off, tk)
        for hh in range(2):
            s = _dot_nt(k_ref[hh, pl.ds(off, tk), :], q_ref[hh])
            s_ref[slot, hh] = s
            mc_ref[slot, hh] = jnp.max(s, axis=0, keepdims=True)

    def stage_b(slot, m):
        m_out = []
        for hh in range(2):
            m_new = jnp.maximum(m[hh], mc_ref[slot, hh])
            al_ref[slot, hh] = jnp.exp2(m[hh] - m_new)
            p_ref[slot, hh] = jnp.exp2(s_ref[slot, hh] - m_new).astype(BF16)
            m_out.append(m_new)
        return tuple(m_out)

    def stage_c(chunk, slot):
        for hh in range(2):
            acc_ref[hh] = acc_ref[hh] * al_ref[slot, hh] + _dot(vt_ref[hh, chunk], p_ref[slot, hh])

    stage_a(0, 0)
    p_ref[1] = jnp.zeros(p_ref.shape[1:], BF16)
    al_ref[1] = jnp.ones(al_ref.shape[1:], F32)
    acc_ref[...] = jnp.zeros(acc_ref.shape, F32)
    m = (jnp.full((1, tq), NEG_INF, F32),) * 2

    def body(jj, m):
        j = 2 * jj
        stage_a(j + 1, 1)
        m = stage_b(0, m)
        stage_c(jnp.maximum(j - 1, 0), 1)
        stage_a(j + 2, 0)
        m = stage_b(1, m)
        stage_c(j, 0)
        return m

    m = lax.fori_loop(0, nk // 2 - 1, body, m)
    stage_a(nk - 1, 1)
    m = stage_b(0, m)
    stage_c(max(nk - 3, 0), 1)
    m = stage_b(1, m)
    stage_c(nk - 2, 0)
    stage_c(nk - 1, 1)
    o_t = jnp.concatenate([acc_ref[hh][:MLA_V, :] / acc_ref[hh][MLA_V:MLA_V + 1, :] for hh in range(2)], axis=0)
    o_ref[...] = o_t.T.astype(o_ref.dtype)


def _mla_attn_group(geom, tk, q, k, vt, out_prev, nseq, s, tok0):
    tq = min(1024, s)
    nq = s // tq
    nk = s // tk
    assert nk % 2 == 0
    qb0 = tok0 // tq
    sb0 = tok0 // s
    args = [q, k, vt]
    in_specs = [pl.BlockSpec((2, tq, LANES), lambda b, hp, i: (hp, qb0 + b * nq + i, 0)),
                pl.BlockSpec((2, s, LANES), lambda b, hp, i: (hp, sb0 + b, 0)),
                pl.BlockSpec((2, nk, MLA_VT_ROWS, tk), lambda b, hp, i: (hp, sb0 + b, 0, 0))]
    aliases = {}
    kern = functools.partial(_mla_attn_kernel, nk)
    if out_prev is not None:
        args.append(out_prev)
        in_specs.append(pl.BlockSpec(memory_space=pl.ANY))
        aliases = {3: 0}
        kern = lambda q_ref, k_ref, vt_ref, prev_ref, *rest, _k=kern: _k(q_ref, k_ref, vt_ref, *rest)
    return pl.pallas_call(
        kern,
        grid=(nseq, MLA_HEADS // 2, nq),
        in_specs=in_specs,
        out_specs=pl.BlockSpec((tq, LANES), lambda b, hp, i: (qb0 + b * nq + i, hp)),
        out_shape=jax.ShapeDtypeStruct((geom.t, MLA_HEADS * MLA_V), BF16),
        scratch_shapes=[pltpu.VMEM((2, 2, tk, tq), F32), pltpu.VMEM((2, 2, 1, tq), F32),
                        pltpu.VMEM((2, 2, tk, tq), BF16), pltpu.VMEM((2, 2, 1, tq), F32),
                        pltpu.VMEM((2, MLA_VT_ROWS, tq), F32)],
        input_output_aliases=aliases,
        compiler_params=_cparams(("parallel", "parallel", "arbitrary")),
        name="mla_attention",
    )(*args)


def _mla_attn(geom, tk, q, k, vt):
    assert geom.tp % geom.ss == 0
    out = _mla_attn_group(geom, tk, q, k, vt, None, geom.bp, geom.sp, 0)
    return _mla_attn_group(geom, tk, q, k, vt, out, geom.bs, geom.ss, geom.tp)


def _conv_kernel(geom, tc, up_ref, uc_ref, un_ref, w_ref, b_ref, gng_ref, gnb_ref, bd_ref, wpw_ref, bpw_ref,
                 o_ref, hs_ref):
    i = pl.program_id(0)
    first = geom.pos(i, tc) == 0
    last = geom.is_last(i, tc)

    def glu(u):
        u = u.astype(F32)
        return u[:, :CONV_CH] * jax.nn.sigmoid(u[:, CONV_CH:])

    hs_ref[0, 0:HALO, :] = jnp.where(first, 0.0, glu(up_ref[...]))
    hs_ref[0, HALO:HALO + tc, :] = glu(uc_ref[...])
    hs_ref[0, HALO + tc:HALO + tc + HALO, :] = jnp.where(last, 0.0, glu(un_ref[...]))
    span = tc + 2 * HALO - SUBLANES
    for sft in range(1, SUBLANES):
        hs_ref[sft, 0:span, :] = hs_ref[0, sft:sft + span, :]

    rows = 64 if tc % 64 == 0 else tc
    chunks = list(range(0, tc, rows))
    accs = {}
    for g0 in range(0, len(chunks), 2):
        group = chunks[g0:g0 + 2]
        for r0 in group:
            accs[r0] = jnp.zeros((rows, CONV_CH), F32) + b_ref[...]
        for j in range(CONV_WIDTH):
            for r0 in group:
                start = HALO + r0 + j - CONV_PAD
                sft = start % SUBLANES
                accs[r0] = accs[r0] + hs_ref[sft, start - sft:start - sft + rows, :] * w_ref[j:j + 1, :]
    for r0 in chunks:
        acc = accs[r0]
        mu = _split_dot(acc, bd_ref[...]) * (CONV_GROUPS / CONV_CH)
        d = acc - mu
        var = _split_dot(d * d, bd_ref[...]) * (CONV_GROUPS / CONV_CH)
        hn = d * lax.rsqrt(var + EPS) * gng_ref[...] + gnb_ref[...]
        y = _dot(_silu(hn).astype(BF16), wpw_ref[...]) + bpw_ref[...]
        o_ref[r0:r0 + rows, :] = y.astype(o_ref.dtype)


def _swa_conv_kernel(geom, nb, tc, *refs):
    n_swa = 13
    swa_in, conv_in = refs[:n_swa], refs[n_swa:n_swa + 10]
    oa_ref, oc_ref, hs_ref = refs[n_swa + 10:]
    _conv_kernel(geom, tc, *conv_in, oc_ref, hs_ref)
    _swa_kernel(geom, nb, *swa_in, oa_ref)


def _swa_conv(geom, q, k, v, gq, gk, bdq, bdk, sink, bias, uc, dw_w, dw_b, gn_g, gn_b, bd, w_pw2, b_pw2):
    ts = geom.tile(1024)
    nb = ts // BLOCK
    nblk = geom.t // BLOCK
    hb = ts // HALO
    nh = geom.t // HALO
    cur = lambda i: (i, 0)
    halo = lambda f: pl.BlockSpec((BLOCK, A_KV), f)
    prev = lambda i: (jnp.maximum(i * nb - 1, 0), 0)
    nxt = lambda i: (jnp.minimum((i + 1) * nb, nblk - 1), 0)
    const = lambda shape: pl.BlockSpec(shape, lambda i: (0,) * len(shape))
    return pl.pallas_call(
        functools.partial(_swa_conv_kernel, geom, nb, ts),
        grid=(geom.t // ts,),
        in_specs=[pl.BlockSpec((ts, A_Q), cur),
                  halo(prev), pl.BlockSpec((ts, A_KV), cur), halo(nxt),
                  halo(prev), pl.BlockSpec((ts, A_KV), cur), halo(nxt),
                  const((1, A_Q)), const((1, A_KV)), const((A_Q, A_Q)), const((A_KV, A_KV)),
                  const((SWA_KV_HEADS, 1, SWA_GROUP * BLOCK)), const(bias.shape),
                  pl.BlockSpec((HALO, 2 * CONV_CH), lambda i: (jnp.maximum(i * hb - 1, 0), 0)),
                  pl.BlockSpec((ts, 2 * CONV_CH), cur),
                  pl.BlockSpec((HALO, 2 * CONV_CH), lambda i: (jnp.minimum((i + 1) * hb, nh - 1), 0)),
                  const((32, CONV_CH)), const((1, CONV_CH)), const((1, CONV_CH)), const((1, CONV_CH)),
                  const((CONV_CH, CONV_CH)), const((CONV_CH, CONV_CH)), const((1, CONV_CH))],
        out_specs=[pl.BlockSpec((ts, A_Q), cur), pl.BlockSpec((ts, CONV_CH), cur)],
        out_shape=[jax.ShapeDtypeStruct((geom.t, A_Q), BF16), jax.ShapeDtypeStruct((geom.t, CONV_CH), BF16)],
        scratch_shapes=[pltpu.VMEM((SUBLANES, ts + 2 * HALO, CONV_CH), F32)],
        compiler_params=_cparams(("parallel",)),
        name="swa_conv",
    )(q, k, k, k, v, v, v, gq, gk, bdq, bdk, sink, bias, uc, uc, uc, dw_w, dw_b, gn_g, gn_b, bd, w_pw2, b_pw2)


def _rms_rows(x, g):
    return x * lax.rsqrt(jnp.mean(x * x, axis=-1, keepdims=True) + EPS) * g


def _store_slabs(ref, x, base=0):
    rows, half = x.shape[0], x.shape[1] // 2
    hi = pltpu.bitcast(x[:, :half].astype(BF16).astype(F32), U32)
    lo = pltpu.bitcast(x[:, half:].astype(BF16).astype(F32), U32)
    words = jnp.bitwise_or(hi, jnp.right_shift(lo, jnp.uint32(16)))
    for c in range(SLAB):
        ref[pl.ds(base + c, rows, stride=SLAB), :] = words[:, c * LANES:(c + 1) * LANES]


def _load_slabs(ref, rows, base=0):
    words = jnp.concatenate([ref[pl.ds(base + c, rows, stride=SLAB), :] for c in range(SLAB)], axis=1)
    hi = pltpu.bitcast(jnp.bitwise_and(words, jnp.uint32(0xFFFF0000)), F32)
    lo = pltpu.bitcast(jnp.left_shift(words, jnp.uint32(16)), F32)
    return jnp.concatenate([hi, lo], axis=1)


def _slab_row(ref, r):
    return ref.at[pl.ds(pl.multiple_of(r * SLAB, SLAB), SLAB)]


def _outproj_kernel(geom, n_x, *refs):
    x_refs = refs[:n_x]
    (oa_ref, ob_ref, oc_ref, ga_ref, gb_ref, gc_ref, w_ref, gate_ref, sh_ref, sc_ref,
     g2_ref, wr_hi_ref, br_ref, x1_ref, h2_ref, lg_ref) = refs[n_x:]
    na = _rms_rows(oa_ref[...].astype(F32), ga_ref[...]).astype(BF16)
    nb = _rms_rows(ob_ref[...].astype(F32), gb_ref[...]).astype(BF16)
    nc = _rms_rows(oc_ref[...].astype(F32), gc_ref[...]).astype(BF16)
    wa = A_Q
    wb = wa + MLA_HEADS * MLA_V
    y = _dot(na, w_ref[0:wa, :]) + _dot(nb, w_ref[wa:wb, :]) + _dot(nc, w_ref[wb:, :])
    x1 = _x_tile(geom, x_refs) + gate_ref[...] * y
    x1_ref[...] = x1
    h2 = _rms_rows(x1, g2_ref[...]) * (1.0 + sc_ref[...]) + sh_ref[...]
    hi = h2.astype(BF16)
    lo = (h2 - hi.astype(F32)).astype(BF16)
    _store_slabs(h2_ref, h2)
    lg2 = _dot(hi, wr_hi_ref[...])
    lg = lg2[:, :LANES] + lg2[:, LANES:] + _dot(lo, wr_hi_ref[:, :LANES]) + br_ref[...]
    lg_ref[...] = lg.T


def _outproj(geom, x_parts, oa, ob, oc, ga, gb, gc, w_out, mods, g2, wr2, br):
    tm = geom.tile(512)
    tok = lambda w: pl.BlockSpec((tm, w), lambda i: (i, 0))
    const = lambda shape: pl.BlockSpec(shape, lambda i: (0, 0))
    wb = MLA_HEADS * MLA_V
    return pl.pallas_call(
        functools.partial(_outproj_kernel, geom, len(x_parts)),
        grid=(geom.t // tm,),
        in_specs=_x_specs(geom, tm, x_parts) + [tok(A_Q), tok(wb), tok(CONV_CH),
                  const((1, A_Q)), const((1, wb)), const((1, CONV_CH)),
                  const((D_MODEL, D_MODEL)),
                  _mod_spec(geom, tm, 2), _mod_spec(geom, tm, 3), _mod_spec(geom, tm, 4),
                  const((1, D_MODEL)), const((D_MODEL, 2 * LANES)), const((1, LANES))],
        out_specs=[tok(D_MODEL), pl.BlockSpec((tm * SLAB, LANES), lambda i: (i, 0)),
                   pl.BlockSpec((LANES, tm), lambda i: (0, i))],
        out_shape=[jax.ShapeDtypeStruct((geom.t, D_MODEL), F32),
                   jax.ShapeDtypeStruct((geom.t * SLAB, LANES), U32),
                   jax.ShapeDtypeStruct((LANES, geom.t), F32)],
        compiler_params=_cparams(("parallel",)),
        name="merge_outproj_norm2",
    )(*x_parts, oa, ob, oc, ga, gb, gc, w_out, mods, mods, mods, g2, wr2, br)


ROUTE_ROWS = 40


def _route_kernel(lg_ref, sel_ref, rt_ref, cnt_ref):
    lg = lg_ref[...]
    row = lax.broadcasted_iota(jnp.int32, lg.shape, 0)
    rowf = row.astype(F32)
    big = float(2 * LANES)
    is_g = jnp.logical_and(row >= N_EXPERTS, row < N_EXPERTS + N_GROUPS)
    gl = jnp.where(is_g, lg, NEG_INF)
    gmax = jnp.max(gl, axis=0, keepdims=True)
    gsum = jnp.sum(jnp.exp(gl - gmax), axis=0, keepdims=True)
    g_val = 1.0 / gsum
    g_idx = jnp.min(jnp.where(jnp.logical_and(is_g, gl == gmax), rowf, big), axis=0, keepdims=True) - N_EXPERTS
    lo = g_idx * EXPERTS_PER_GROUP
    is_e = jnp.logical_and(rowf >= lo, rowf < lo + EXPERTS_PER_GROUP)
    el = jnp.where(is_e, lg, NEG_INF)
    emax = jnp.max(el, axis=0, keepdims=True)
    ee = jnp.exp(el - emax)
    e_prob = ee / jnp.sum(ee, axis=0, keepdims=True)
    p1 = jnp.where(is_e, e_prob, -1.0)
    v1 = jnp.max(p1, axis=0, keepdims=True)
    i1 = jnp.min(jnp.where(p1 == v1, rowf, big), axis=0, keepdims=True)
    p2 = jnp.where(rowf == i1, -1.0, p1)
    v2 = jnp.max(p2, axis=0, keepdims=True)
    i2 = jnp.min(jnp.where(p2 == v2, rowf, big), axis=0, keepdims=True)
    scale = g_val / (v1 + v2)
    r8 = lax.broadcasted_iota(jnp.int32, rt_ref.shape, 0)
    rt_ref[...] = (jnp.where(r8 == 0, i1, 0.0) + jnp.where(r8 == 1, i2, 0.0) +
                   jnp.where(r8 == 2, v1 * scale, 0.0) + jnp.where(r8 == 3, v2 * scale, 0.0))
    chosen = jnp.where(jnp.logical_or(rowf == i1, rowf == i2), 1.0, 0.0).astype(BF16)
    cnt_ref[...] = _dot(chosen, sel_ref[...])


def _route(geom, tr, logits_t, sel):
    n = geom.t // tr
    return pl.pallas_call(
        _route_kernel,
        grid=(n,),
        in_specs=[pl.BlockSpec((ROUTE_ROWS, tr), lambda i: (0, i)), pl.BlockSpec(sel.shape, lambda i: (0, 0))],
        out_specs=[pl.BlockSpec((SUBLANES, tr), lambda i: (0, i)),
                   pl.BlockSpec((None, ROUTE_ROWS, LANES), lambda i: (i, 0, 0))],
        out_shape=[jax.ShapeDtypeStruct((SUBLANES, geom.t), F32), jax.ShapeDtypeStruct((n, ROUTE_ROWS, LANES), F32)],
        compiler_params=_cparams(("parallel",)),
        name="moe_route",
    )(logits_t, sel)


def _slots_kernel(tm, rt_ref, st_ref, tri_ref, sl_ref):
    rt = rt_ref[...]
    rowf = lax.broadcasted_iota(jnp.int32, (ROUTE_ROWS, rt.shape[1]), 0).astype(F32)
    oh1 = rowf == rt[0:1, :]
    oh2 = rowf == rt[1:2, :]
    chosen = jnp.where(jnp.logical_or(oh1, oh2), 1.0, 0.0).astype(BF16)
    for j in range(rt.shape[1] // tm):
        cols = slice(j * tm, (j + 1) * tm)
        first = _dot(chosen[:, cols], tri_ref[...]) + st_ref[:, j:j + 1]
        s1 = jnp.sum(jnp.where(oh1[:, cols], first, 0.0), axis=0, keepdims=True)
        s2 = jnp.sum(jnp.where(oh2[:, cols], first, 0.0), axis=0, keepdims=True)
        sl_ref[j] = jnp.concatenate([s1, s2], axis=1).astype(jnp.int32)


def _slots(geom, tr, tm, rt_t, start3, tri):
    ns = tr // tm
    return pl.pallas_call(
        functools.partial(_slots_kernel, tm),
        grid=(geom.t // tr,),
        in_specs=[pl.BlockSpec((SUBLANES, tr), lambda i: (0, i)),
                  pl.BlockSpec((None, ROUTE_ROWS, LANES), lambda i: (i, 0, 0)),
                  pl.BlockSpec((tm, tm), lambda i: (0, 0))],
        out_specs=pl.BlockSpec((ns, 1, 2 * tm), lambda i: (i, 0, 0)),
        out_shape=jax.ShapeDtypeStruct((geom.t // tm, 1, 2 * tm), jnp.int32),
        compiler_params=_cparams(("parallel",)),
        name="moe_slots",
    )(rt_t, start3, tri)


def _plan(cnt3, ns, n_tiles_max):
    n_steps = cnt3.shape[0]
    cnt_t = cnt3[:, :, :ns].transpose(0, 2, 1).reshape(n_steps * ns, ROUTE_ROWS)
    incl = jnp.cumsum(cnt_t, axis=0)
    tiles_e = jnp.ceil(incl[-1] / TME)
    end_e = jnp.cumsum(tiles_e)
    start = incl - cnt_t + ((end_e - tiles_e) * TME)[None, :]
    start3 = jnp.pad(start.reshape(n_steps, ns, ROUTE_ROWS).transpose(0, 2, 1), ((0, 0), (0, 0), (0, LANES - ns)))
    n_used = end_e[-1].astype(jnp.int32)
    tile_idx = jnp.minimum(jnp.arange(n_tiles_max, dtype=jnp.int32), n_used - 1)
    tile_exp = jnp.sum(tile_idx[:, None] >= end_e[None, :N_EXPERTS].astype(jnp.int32), axis=1).astype(jnp.int32)
    return start3, tile_exp, tile_idx, n_used.reshape(1)


def _scatter_kernel(sl_ref, h_ref, xs_in, xs_out, sem):
    del xs_in
    tm = h_ref.shape[0] // SLAB

    def body(rr, carry):
        for j in range(ROW_UNROLL):
            r = rr * ROW_UNROLL + j
            pltpu.make_async_copy(_slab_row(h_ref, r), _slab_row(xs_out, sl_ref[0, r]), sem).start(priority=0)
            pltpu.make_async_copy(_slab_row(h_ref, r), _slab_row(xs_out, sl_ref[0, tm + r]), sem).start(priority=1)
        return carry

    lax.fori_loop(0, tm // ROW_UNROLL, body, 0)
    for _ in range(2):
        pltpu.make_async_copy(h_ref, xs_out.at[pl.ds(0, tm * SLAB)], sem).wait()


def _scatter(geom, tm, slots_smem, h2, xs):
    return pl.pallas_call(
        _scatter_kernel,
        grid=(geom.t // tm,),
        in_specs=[pl.BlockSpec((None, 1, 2 * tm), lambda i: (i, 0, 0), memory_space=pltpu.SMEM),
                  pl.BlockSpec((tm * SLAB, LANES), lambda i: (i, 0)),
                  pl.BlockSpec(memory_space=pl.ANY)],
        out_specs=pl.BlockSpec(memory_space=pl.ANY),
        out_shape=jax.ShapeDtypeStruct(xs.shape, xs.dtype),
        scratch_shapes=[pltpu.SemaphoreType.DMA(())],
        input_output_aliases={2: 0},
        compiler_params=_cparams(("arbitrary",)),
        name="moe_scatter",
    )(slots_smem, h2, xs)


def _experts_kernel(te_ref, ti_ref, nu_ref, xs_ref, wg_ref, wu_ref, wd_ref, ys_ref):
    del te_ref, ti_ref

    @pl.when(pl.program_id(0) < nu_ref[0])
    def _():
        x = _load_slabs(xs_ref, TME).astype(BF16)
        act = (_silu(_dot(x, wg_ref[...].astype(BF16))) * _dot(x, wu_ref[...].astype(BF16))).astype(BF16)
        _store_slabs(ys_ref, _dot(act, wd_ref[...].astype(BF16)))


def _experts(layer, xs, tile_exp, tile_idx, n_used, wg, wu, wd):
    n_tiles_max = xs.shape[0] // (TME * SLAB)
    slab = pl.BlockSpec((TME * SLAB, LANES), lambda n, te, ti, nu: (ti[n], 0))
    wspec = lambda a, b: pl.BlockSpec((None, None, a, b), lambda n, te, ti, nu: (layer, te[n], 0, 0))
    return pl.pallas_call(
        _experts_kernel,
        grid_spec=pltpu.PrefetchScalarGridSpec(
            num_scalar_prefetch=3,
            grid=(n_tiles_max,),
            in_specs=[slab, wspec(D_MODEL, D_EXPERT), wspec(D_MODEL, D_EXPERT), wspec(D_EXPERT, D_MODEL)],
            out_specs=slab),
        out_shape=jax.ShapeDtypeStruct(xs.shape, U32),
        compiler_params=_cparams(("arbitrary",)),
        name="moe_experts",
    )(tile_exp, tile_idx, n_used, xs, wg, wu, wd)


def _combine_kernel(geom, final, sl_cur, sl_nxt, rt_ref, x1_ref, gate_ref, ys_hbm, *refs):
    buf1, buf2, sem = refs[-3:]
    i = pl.program_id(0)
    tm = x1_ref.shape[0]

    def issue_row(sl_ref, slot, r):
        dst = slot * tm + r
        pltpu.make_async_copy(_slab_row(ys_hbm, sl_ref[0, r]), _slab_row(buf1, dst), sem.at[slot]).start(priority=0)
        pltpu.make_async_copy(_slab_row(ys_hbm, sl_ref[0, tm + r]), _slab_row(buf2, dst),
                              sem.at[slot]).start(priority=1)

    def wait_slot(slot):
        base = pl.multiple_of(slot * (tm * SLAB), tm * SLAB)
        for buf in (buf1, buf2):
            pltpu.make_async_copy(ys_hbm.at[pl.ds(0, tm * SLAB)], buf.at[pl.ds(base, tm * SLAB)],
                                  sem.at[slot]).wait()
        return base

    @pl.when(i == 0)
    def _():
        def body(rr, carry):
            for j in range(ROW_UNROLL):
                issue_row(sl_cur, 0, rr * ROW_UNROLL + j)
            return carry
        lax.fori_loop(0, tm // ROW_UNROLL, body, 0)

    slot = i % 2
    base = wait_slot(slot)
    for r in range(tm):
        issue_row(sl_nxt, 1 - slot, r)
    w = jnp.concatenate([rt_ref[...], jnp.zeros((LANES - SUBLANES, tm), F32)], axis=0).T
    y = w[:, 2:3] * _load_slabs(buf1, tm, base) + w[:, 3:4] * _load_slabs(buf2, tm, base)
    x_new = x1_ref[...] + gate_ref[...] * y
    if final:
        out_p, out_s = refs[:2]
        n_prompt = geom.tp // tm

        @pl.when(i < n_prompt)
        def _():
            out_p[...] = x_new

        @pl.when(i >= n_prompt)
        def _():
            out_s[...] = x_new
    else:
        sh_ref, sc_ref, g_ref, w_ref, o_ref = refs[:5]
        o_ref[...] = x_new
        _norm1_project(x_new, sh_ref, sc_ref, g_ref, w_ref, refs[5:-3])

    @pl.when(i == pl.num_programs(0) - 1)
    def _():
        wait_slot(1 - slot)


def _combine(geom, tm, slots_smem, rt, x1, mods, ys, nxt=None):
    n = geom.t // tm
    n_prompt = geom.tp // tm
    tok = lambda w: pl.BlockSpec((tm, w), lambda i: (i, 0))
    smem = lambda f: pl.BlockSpec((None, 1, 2 * tm), f, memory_space=pltpu.SMEM)
    args = [slots_smem, slots_smem, rt, x1, mods, ys]
    in_specs = [smem(lambda i: (i, 0, 0)), smem(lambda i: (jnp.minimum(i + 1, n - 1), 0, 0)),
                pl.BlockSpec((SUBLANES, tm), lambda i: (0, i)), tok(D_MODEL), _mod_spec(geom, tm, 5),
                pl.BlockSpec(memory_space=pl.ANY)]
    if nxt is None:
        out_specs = [pl.BlockSpec((tm, D_MODEL), lambda i: (jnp.minimum(i, n_prompt - 1), 0)),
                     pl.BlockSpec((tm, D_MODEL), lambda i: (jnp.maximum(i - n_prompt, 0), 0))]
        out_shape = [jax.ShapeDtypeStruct((geom.tp, D_MODEL), F32),
                     jax.ShapeDtypeStruct((geom.t - geom.tp, D_MODEL), F32)]
    else:
        mods_n, g1_n, w_in_n = nxt
        args += [mods_n, mods_n, g1_n, w_in_n]
        in_specs += [_mod_spec(geom, tm, 0), _mod_spec(geom, tm, 1), pl.BlockSpec((1, D_MODEL), lambda i: (0, 0)),
                     pl.BlockSpec(w_in_n.shape, lambda i: (0, 0))]
        out_specs = [tok(D_MODEL)] + [tok(w) for _, w in _IN_SEGS]
        out_shape = [jax.ShapeDtypeStruct((geom.t, D_MODEL), F32)] + \
            [jax.ShapeDtypeStruct((geom.t, w), BF16) for _, w in _IN_SEGS]
    return pl.pallas_call(
        functools.partial(_combine_kernel, geom, nxt is None),
        grid=(n,),
        in_specs=in_specs,
        out_specs=out_specs,
        out_shape=out_shape,
        scratch_shapes=[pltpu.VMEM((2 * tm * SLAB, LANES), U32), pltpu.VMEM((2 * tm * SLAB, LANES), U32),
                        pltpu.SemaphoreType.DMA((2,))],
        compiler_params=_cparams(("arbitrary",)),
        name="moe_combine",
    )(*args)


def _block_diag_ones(n, blk):
    idx = np.arange(n) // blk
    return jnp.asarray((idx[:, None] == idx[None, :]).astype(np.float32), dtype=BF16)


def _rope_tables(smax):
    inv_freq = 1.0 / (ROPE_BASE ** (jnp.arange(0, MLA_ROPE, 2, dtype=F32) / MLA_ROPE))
    ang = jnp.arange(smax, dtype=F32)[:, None] * inv_freq[None, :]
    cos, sin = jnp.cos(ang), jnp.sin(ang)
    zeros = lambda w: jnp.zeros((smax, w), F32)
    pad = LANES - MLA_QK
    c = jnp.concatenate([jnp.ones((smax, MLA_NOPE), F32), cos, cos, zeros(pad)], axis=1)
    s = jnp.concatenate([zeros(MLA_NOPE), -sin, sin, zeros(pad)], axis=1)
    return c, s


def _pad_cols(w, n):
    return jnp.pad(w, ((0, 0), (0, n - w.shape[1])))


def _layer_weights(l, w_in, mla_w_q_up, mla_w_kv_up, mla_q_gain, mla_k_gain, swa_q_gain, swa_k_gain, swa_sink,
                   moe_w_group, moe_b_group, moe_w_expert, moe_b_expert):
    wi = w_in[l]
    kr = jnp.pad(wi[:, A_IN + MLA_Q_RANK + MLA_KV_RANK:A_IN + B_IN], ((0, 0), (MLA_NOPE, LANES - MLA_QK)))
    w_in_p = jnp.concatenate([wi[:, :A_IN], wi[:, A_IN:A_IN + MLA_Q_RANK + MLA_KV_RANK], kr, wi[:, A_IN + B_IN:]],
                             axis=1).astype(BF16)
    wq = mla_w_q_up[l].reshape(MLA_Q_RANK, MLA_HEADS, MLA_QK)
    wq_p = jnp.pad(wq, ((0, 0), (0, 0), (0, LANES - MLA_QK)))
    wq2 = jnp.concatenate([wq_p, _swap_rope_halves(wq_p)], axis=2).reshape(MLA_Q_RANK, MLA_HEADS * 2 * LANES).astype(BF16)
    wvt = mla_w_kv_up[l].reshape(MLA_KV_RANK, MLA_HEADS, MLA_NOPE + MLA_V)[:, :, MLA_NOPE:].transpose(1, 2, 0).astype(BF16)
    gq_p = _pad_cols(mla_q_gain[l][None, :], LANES)
    gk_p = _pad_cols(mla_k_gain[l][None, :], LANES)
    gq_s, gk_s = _swap_rope_halves(gq_p), _swap_rope_halves(gk_p)
    gq_a = jnp.tile(swa_q_gain[l], SWA_HEADS)[None, :]
    gk_a = jnp.tile(swa_k_gain[l], SWA_KV_HEADS)[None, :]
    sink = jnp.repeat(swa_sink[l] * LOG2E, BLOCK).reshape(SWA_KV_HEADS, 1, SWA_GROUP * BLOCK)
    wr = _pad_cols(jnp.concatenate([moe_w_expert[l], moe_w_group[l]], axis=1), LANES)
    wr_hi = wr.astype(BF16)
    wr2 = jnp.concatenate([wr_hi, (wr - wr_hi.astype(F32)).astype(BF16)], axis=1)
    br = _pad_cols(jnp.concatenate([moe_b_expert[l], moe_b_group[l]])[None, :], LANES)
    return w_in_p, wq2, wvt, gq_p, gq_s, gk_p, gk_s, gq_a, gk_a, sink, wr2, br


def kernel(x_prompt, x_sample, c_prompt, c_sample, w_ada, b_ada, norm1_g, norm2_g, w_in, swa_q_gain, swa_k_gain, swa_sink, mla_q_norm_g, mla_w_q_up, mla_kv_norm_g, mla_w_kv_up, mla_q_gain, mla_k_gain, conv_dw_w, conv_dw_b, conv_gn_g, conv_gn_b, conv_w_pw2, conv_b_pw2, out_norm_a, out_norm_b, out_norm_c, w_out, moe_w_group, moe_b_group, moe_w_expert, moe_b_expert, moe_w_gate, moe_w_up, moe_w_down):
    bp, sp, d = x_prompt.shape
    bs, ss, _ = x_sample.shape
    assert d == D_MODEL
    geom = _Geom(bp, sp, bs, ss)
    depth = w_ada.shape[0]

    x_parts = (x_prompt.reshape(bp * sp, d), x_sample.reshape(bs * ss, d))
    c = jnp.concatenate([c_prompt, c_sample], axis=0)
    rows = -(-geom.nb // 8) * 8
    c_pad = jnp.pad(c, ((0, rows - geom.nb), (0, 0)))
    mods_all = _modulation(c_pad, w_ada, b_ada)

    rope_c, rope_s = _rope_tables(max(sp, ss))
    rope_perm = _swap_rope_halves(jnp.eye(LANES, dtype=F32)).astype(BF16)
    ones2 = _block_diag_ones(2 * LANES, LANES)
    bdq = _block_diag_ones(A_Q, HEAD_DIM)
    bdk = _block_diag_ones(A_KV, HEAD_DIM)
    bdc = _block_diag_ones(CONV_CH, CONV_CH // CONV_GROUPS)
    swa_bias = _swa_bias()
    row = lambda v: v[None, :]
    tm_moe = geom.tile(256)
    n_tiles_max = 2 * geom.t // TME + N_EXPERTS
    tr_moe = geom.tile(2048)
    tri = jnp.asarray(np.triu(np.ones((tm_moe, tm_moe), np.float32), 1), dtype=BF16)
    sel = jnp.asarray(np.arange(tr_moe)[:, None] // tm_moe == np.arange(LANES)[None, :], dtype=BF16)
    xs = jnp.zeros((n_tiles_max * TME * SLAB, LANES), U32)

    weights = [_layer_weights(l, w_in, mla_w_q_up, mla_w_kv_up, mla_q_gain, mla_k_gain, swa_q_gain, swa_k_gain,
                              swa_sink, moe_w_group, moe_b_group, moe_w_expert, moe_b_expert) for l in range(depth)]
    mods_l = [mods_all[l, :geom.nb].reshape(geom.nb * N_MOD, 1, D_MODEL) for l in range(depth)]
    proj = _inproj(geom, x_parts, mods_l[0], row(norm1_g[0]), weights[0][0])

    for l in range(depth):
        (w_in_p, wq2, wvt, gq_p, gq_s, gk_p, gk_s, gq_a, gk_a, sink, wr2, br) = weights[l]
        mods = mods_l[l]
        q_a, k_a, v_a, cq, ckv, kr, uc = proj
        tk = geom.tile(512)
        q_b, k_b, vt_b = _mla_prep(geom, tk, cq, ckv, kr, row(mla_q_norm_g[l]), row(mla_kv_norm_g[l]), wq2,
                                   mla_w_kv_up[l].astype(BF16), wvt, rope_perm, ones2, gq_p, gq_s, gk_p, gk_s,
                                   rope_c, rope_s)
        out_b = _mla_attn(geom, tk, q_b, k_b, vt_b)
        dw_w = jnp.pad(conv_dw_w[l].reshape(CONV_WIDTH, CONV_CH), ((0, 32 - CONV_WIDTH), (0, 0)))
        out_a, out_c = _swa_conv(geom, q_a, k_a, v_a, gq_a, gk_a, bdq, bdk, sink, swa_bias, uc, dw_w,
                                 row(conv_dw_b[l]), row(conv_gn_g[l]), row(conv_gn_b[l]), bdc,
                                 conv_w_pw2[l].astype(BF16), row(conv_b_pw2[l]))
        x1, h2, logits = _outproj(geom, x_parts, out_a, out_b, out_c, row(out_norm_a[l]), row(out_norm_b[l]),
                                  row(out_norm_c[l]), w_out[l].astype(BF16), mods, row(norm2_g[l]),
                                  wr2, br)
        rt, cnt3 = _route(geom, tr_moe, logits, sel)
        start3, tile_exp, tile_idx, n_used = _plan(cnt3, tr_moe // tm_moe, n_tiles_max)
        slots = _slots(geom, tr_moe, tm_moe, rt, start3, tri)
        xs = _scatter(geom, tm_moe, slots, h2, xs)
        ys = _experts(l, xs, tile_exp, tile_idx, n_used, moe_w_gate, moe_w_up, moe_w_down)
        if l + 1 < depth:
            x_new, *proj = _combine(geom, tm_moe, slots, rt, x1, mods, ys,
                                nxt=(mods_l[l + 1], row(norm1_g[l + 1]), weights[l + 1][0]))
            x_parts = (x_new,)
        else:
            y_prompt, y_sample = _combine(geom, tm_moe, slots, rt, x1, mods, ys)

    return (y_prompt.reshape(bp, sp, d), y_sample.reshape(bs, ss, d))
```

```python
import functools

import numpy as np
import jax
import jax.numpy as jnp
from jax import lax
from jax.experimental import pallas as pl
from jax.experimental.pallas import tpu as pltpu

F32 = jnp.float32
BF16 = jnp.bfloat16

EPS = 1e-6
NEG_INF = -1e30
LOG2E = 1.4426950408889634

D_MODEL = 1024
HEAD_DIM = 64
SWA_HEADS = 6
SWA_KV_HEADS = 2
SWA_GROUP = SWA_HEADS // SWA_KV_HEADS
WINDOW = 128
BLOCK = WINDOW
MLA_HEADS = 6
MLA_Q_RANK = 256
MLA_KV_RANK = 128
MLA_NOPE = 64
MLA_ROPE = 32
MLA_QK = MLA_NOPE + MLA_ROPE
MLA_V = 64
MLA_VT_ROWS = MLA_V + 16
ROPE_BASE = 10000.0
CONV_CH = 256
CONV_GROUPS = 4
CONV_WIDTH = 31
CONV_PAD = (CONV_WIDTH - 1) // 2
A_Q = SWA_HEADS * HEAD_DIM
A_KV = SWA_KV_HEADS * HEAD_DIM
A_IN = A_Q + 2 * A_KV
B_IN = MLA_Q_RANK + MLA_KV_RANK + MLA_ROPE
N_GROUPS = 4
EXPERTS_PER_GROUP = 8
N_EXPERTS = N_GROUPS * EXPERTS_PER_GROUP
D_EXPERT = 256
N_MOD = 6

LANES = 128
SUBLANES = 8
SLAB = D_MODEL // (2 * 128)
U32 = jnp.uint32
TME = 512
ROW_UNROLL = 8
HALO = 16
VMEM_LIMIT = 48 * 1024 * 1024


class _Geom:
    def __init__(self, bp, sp, bs, ss):
        self.bp, self.sp, self.bs, self.ss = bp, sp, bs, ss
        self.tp = bp * sp
        self.t = bp * sp + bs * ss
        self.nb = bp + bs

    def tile(self, target):
        t = target
        while self.sp % t or self.ss % t:
            t //= 2
        return t

    def batch(self, i, tm):
        npt = self.tp // tm
        return jnp.where(i < npt, i // (self.sp // tm), self.bp + (i - npt) // (self.ss // tm))

    def pos(self, i, tm):
        npt = self.tp // tm
        return jnp.where(i < npt, i % (self.sp // tm), (i - npt) % (self.ss // tm))

    def is_last(self, i, tm):
        npt = self.tp // tm
        return jnp.where(i < npt, i % (self.sp // tm) == self.sp // tm - 1,
                         (i - npt) % (self.ss // tm) == self.ss // tm - 1)


def _cparams(sem):
    return pltpu.CompilerParams(dimension_semantics=sem, vmem_limit_bytes=VMEM_LIMIT)


def _silu(x):
    return x * jax.nn.sigmoid(x)


def _dot(a, b):
    return jnp.dot(a, b, preferred_element_type=F32)


def _dot_nt(a, b):
    return lax.dot_general(a, b, (((1,), (1,)), ((), ())), preferred_element_type=F32)


def _split_dot(x, w):
    hi = x.astype(BF16)
    lo = (x - hi.astype(F32)).astype(BF16)
    return _dot(hi, w) + _dot(lo, w)


def _mod_kernel(c_ref, w_ref, b_ref, o_ref):
    c = c_ref[...]
    o_ref[...] = _dot(_silu(c).astype(BF16), w_ref[...].astype(BF16)) + b_ref[...]


def _modulation(c_pad, w_ada, b_ada):
    L, d, n = w_ada.shape
    tn = 768
    rows = c_pad.shape[0]
    return pl.pallas_call(
        _mod_kernel,
        grid=(L, n // tn),
        in_specs=[pl.BlockSpec((rows, d), lambda l, j: (0, 0)),
                  pl.BlockSpec((None, d, tn), lambda l, j: (l, 0, j)),
                  pl.BlockSpec((None, 1, tn), lambda l, j: (l, 0, j))],
        out_specs=pl.BlockSpec((None, rows, tn), lambda l, j: (l, 0, j)),
        out_shape=jax.ShapeDtypeStruct((L, rows, n), F32),
        compiler_params=_cparams(("arbitrary", "arbitrary")),
        name="adaln_mod",
    )(c_pad, w_ada, b_ada.reshape(L, 1, n))


_IN_SEGS = (("q", A_Q), ("k", A_KV), ("v", A_KV), ("cq", MLA_Q_RANK), ("ckv", MLA_KV_RANK), ("kr", LANES),
            ("uc", 2 * CONV_CH))


def _norm1_project(x, sh_ref, sc_ref, g_ref, w_ref, out_refs, after_weight_load=None):
    ms = jnp.mean(x * x, axis=-1, keepdims=True)
    h = x * lax.rsqrt(ms + EPS) * g_ref[...]
    h = (h * (1.0 + sc_ref[...]) + sh_ref[...]).astype(BF16)
    off = 0
    for k, ((_, width), o_ref) in enumerate(zip(_IN_SEGS, out_refs)):
        w = w_ref[:, off:off + width]
        if after_weight_load is not None:
            after_weight_load(k, len(_IN_SEGS))
        o_ref[...] = _dot(h, w).astype(o_ref.dtype)
        off += width


def _x_specs(geom, tm, x_parts):
    if len(x_parts) == 1:
        return [pl.BlockSpec((tm, D_MODEL), lambda i: (i, 0))]
    n_prompt = geom.tp // tm
    return [pl.BlockSpec((tm, D_MODEL), lambda i: (jnp.minimum(i, n_prompt - 1), 0)),
            pl.BlockSpec((tm, D_MODEL), lambda i: (jnp.maximum(i - n_prompt, 0), 0))]


def _x_tile(geom, x_refs):
    if len(x_refs) == 1:
        return x_refs[0][...]
    tm = x_refs[0].shape[0]
    return jnp.where(pl.program_id(0) < geom.tp // tm, x_refs[0][...], x_refs[1][...])


def _inproj_kernel(geom, n_x, *refs):
    sh_ref, sc_ref, g_ref, w_ref = refs[n_x:n_x + 4]
    _norm1_project(_x_tile(geom, refs[:n_x]), sh_ref, sc_ref, g_ref, w_ref, refs[n_x + 4:])


def _mod_spec(geom, tm, k):
    return pl.BlockSpec((None, 1, D_MODEL), lambda i: (geom.batch(i, tm) * N_MOD + k, 0, 0))


def _inproj(geom, x_parts, mods, g1, w_in_p):
    tm = geom.tile(512)
    nw = w_in_p.shape[1]
    return pl.pallas_call(
        functools.partial(_inproj_kernel, geom, len(x_parts)),
        grid=(geom.t // tm,),
        in_specs=_x_specs(geom, tm, x_parts) + [
                  _mod_spec(geom, tm, 0), _mod_spec(geom, tm, 1),
                  pl.BlockSpec((1, D_MODEL), lambda i: (0, 0)),
                  pl.BlockSpec((D_MODEL, nw), lambda i: (0, 0))],
        out_specs=[pl.BlockSpec((tm, w), lambda i: (i, 0)) for _, w in _IN_SEGS],
        out_shape=[jax.ShapeDtypeStruct((geom.t, w), BF16) for _, w in _IN_SEGS],
        compiler_params=_cparams(("parallel",)),
        name="norm1_inproj",
    )(*x_parts, mods, mods, g1, w_in_p)


def _swa_kernel(geom, nb, q_ref, kp_ref, kc_ref, kn_ref, vp_ref, vc_ref, vn_ref, gq_ref, gk_ref,
                bdq_ref, bdk_ref, sink_ref, bias_ref, o_ref):
    i = pl.program_id(0)
    first = (geom.pos(i, nb * BLOCK) == 0).astype(jnp.int32)
    last = geom.is_last(i, nb * BLOCK).astype(jnp.int32)

    q = q_ref[...].astype(F32)
    msq = _dot((q * q).astype(BF16), bdq_ref[...]) * (1.0 / HEAD_DIM)
    q_t = (q * lax.rsqrt(msq + EPS) * gq_ref[...] * (HEAD_DIM ** -0.5 * LOG2E)).T.astype(BF16)

    k_all = jnp.concatenate([kp_ref[...], kc_ref[...], kn_ref[...]], axis=0).astype(F32)
    msk = _dot((k_all * k_all).astype(BF16), bdk_ref[...]) * (1.0 / HEAD_DIM)
    k_all = (k_all * lax.rsqrt(msk + EPS) * gk_ref[...]).astype(BF16)
    v_all = jnp.concatenate([vp_ref[...], vc_ref[...], vn_ref[...]], axis=0).astype(F32)
    ones_rows = 16
    v_t = jnp.concatenate([v_all.T, jnp.ones((ones_rows, v_all.shape[0]), F32)], axis=0).astype(BF16)
    zeros = jnp.zeros((HEAD_DIM, SWA_GROUP * BLOCK), BF16)

    pairs = [(b, g) for b in range(nb) for g in range(SWA_KV_HEADS)]
    scores, maxes = {}, {}
    for b, g in pairs:
        cls = (first if b == 0 else 0) + (2 * last if b == nb - 1 else 0)
        heads = range(g * SWA_GROUP, (g + 1) * SWA_GROUP)
        qg = jnp.concatenate([q_t[h * HEAD_DIM:(h + 1) * HEAD_DIM, b * BLOCK:(b + 1) * BLOCK] for h in heads],
                             axis=1)
        q_pad = jnp.concatenate([qg if j == g else zeros for j in range(SWA_KV_HEADS)], axis=0)
        s = _dot(k_all[b * BLOCK:(b + 3) * BLOCK, :], q_pad) + bias_ref[cls, g]
        scores[b, g] = s
        maxes[b, g] = jnp.maximum(jnp.max(s, axis=0, keepdims=True), sink_ref[g])
    probs = {bg: jnp.exp2(scores[bg] - maxes[bg]).astype(BF16) for bg in pairs}
    outs = {(b, g): _dot(v_t[:, b * BLOCK:(b + 3) * BLOCK], probs[b, g]) for b, g in pairs}
    for b in range(nb):
        rows = []
        for g in range(SWA_KV_HEADS):
            o = outs[b, g]
            denom = o[A_KV:A_KV + 1, :] + jnp.exp2(sink_ref[g] - maxes[b, g])
            og = o[g * HEAD_DIM:(g + 1) * HEAD_DIM, :] / denom
            rows += [og[:, j * BLOCK:(j + 1) * BLOCK] for j in range(SWA_GROUP)]
        o_ref[b * BLOCK:(b + 1) * BLOCK, :] = jnp.concatenate(rows, axis=0).T.astype(o_ref.dtype)


def _swa_bias():
    k = np.arange(3 * BLOCK)[:, None]
    q = np.arange(BLOCK)[None, :]
    rel = np.abs(k - BLOCK - q)
    out = np.zeros((4, SWA_KV_HEADS, 3 * BLOCK, SWA_GROUP * BLOCK), np.float32)
    for c in range(4):
        k_lo = BLOCK if c & 1 else 0
        k_hi = 2 * BLOCK if c & 2 else 3 * BLOCK
        valid = (rel <= WINDOW) & (k >= k_lo) & (k < k_hi)
        for g in range(SWA_KV_HEADS):
            for j in range(SWA_GROUP):
                slope = 2.0 ** (-8.0 * (g * SWA_GROUP + j + 1) / SWA_HEADS)
                out[c, g, :, j * BLOCK:(j + 1) * BLOCK] = np.where(valid, -slope * LOG2E * rel, NEG_INF)
    return jnp.asarray(out)


def _swap_rope_halves(a):
    lo, hi = MLA_NOPE, MLA_NOPE + MLA_ROPE // 2
    z = jnp.zeros_like(a)
    return jnp.concatenate([z[..., :lo], a[..., hi:MLA_QK], a[..., lo:hi], z[..., MLA_QK:]], axis=-1)


def _mla_prep_kernel(cq_ref, ckv_ref, kr_ref, gqn_ref, gkvn_ref, wq_ref, wkv_ref, wvt_ref, perm_ref, ones_ref,
                     gq_ref, gqs_ref, gk_ref, gks_ref, c_ref, s_ref, q_out, k_out, vt_out):
    cq = cq_ref[...].astype(F32)
    qn = (cq * lax.rsqrt(jnp.mean(cq * cq, axis=-1, keepdims=True) + EPS) * gqn_ref[...]).astype(BF16)
    ckv = ckv_ref[...].astype(F32)
    kvn = (ckv * lax.rsqrt(jnp.mean(ckv * ckv, axis=-1, keepdims=True) + EPS) * gkvn_ref[...]).astype(BF16)
    kr_b = kr_ref[...]
    kr = kr_b.astype(F32)
    lane = lax.broadcasted_iota(jnp.int32, (1, LANES), 1)
    ones = jnp.ones((MLA_VT_ROWS - MLA_V, cq.shape[0]), BF16)
    c = c_ref[...]
    sn = s_ref[...]
    q_c = c * (gq_ref[...] * (MLA_QK ** -0.5 * LOG2E))
    q_s = sn * (gqs_ref[...] * (MLA_QK ** -0.5 * LOG2E))
    k_c = c * gk_ref[...]
    k_rot = _dot(kr_b, perm_ref[...]) * (sn * gks_ref[...])
    inv_d = 1.0 / MLA_QK
    for hp in range(MLA_HEADS // 2):
        kv2 = _dot(kvn, wkv_ref[:, hp * 2 * LANES:(hp + 1) * 2 * LANES])
        for hh in range(2):
            h = 2 * hp + hh
            xq = _dot(qn, wq_ref[:, h * 2 * LANES:(h + 1) * 2 * LANES])
            x, xs = xq[:, :LANES], xq[:, LANES:]
            xk = jnp.where(lane < MLA_NOPE, kv2[:, hh * LANES:(hh + 1) * LANES], 0.0) + kr
            ss = _dot(jnp.concatenate([x * x, xk * xk], axis=1).astype(BF16), ones_ref[...])
            rq = lax.rsqrt(ss[:, :LANES] * inv_d + EPS)
            rk = lax.rsqrt(ss[:, LANES:] * inv_d + EPS)
            q_out[h] = (rq * (x * q_c + xs * q_s)).astype(BF16)
            k_out[h] = (rk * (xk * k_c + k_rot)).astype(BF16)
            vt_out[h, 0] = jnp.concatenate([_dot_nt(wvt_ref[h], kvn).astype(BF16), ones], axis=0)


def _mla_prep(geom, tk, cq, ckv, kr, gqn, gkvn, wq2, wkv, wvt, perm, ones2, gq_p, gq_s, gk_p, gk_s, rope_c, rope_s):
    tm = tk
    tok = lambda w: pl.BlockSpec((tm, w), lambda i: (i, 0))
    const = lambda shape: pl.BlockSpec(shape, lambda i: (0,) * len(shape))
    rope = pl.BlockSpec((tm, LANES), lambda i: (geom.pos(i, tm), 0))
    hm = pl.BlockSpec((MLA_HEADS, tm, LANES), lambda i: (0, i, 0))
    hm_t = pl.BlockSpec((MLA_HEADS, 1, MLA_VT_ROWS, tm), lambda i: (0, i, 0, 0))
    return pl.pallas_call(
        _mla_prep_kernel,
        grid=(geom.t // tm,),
        in_specs=[tok(MLA_Q_RANK), tok(MLA_KV_RANK), tok(LANES),
                  const((1, MLA_Q_RANK)), const((1, MLA_KV_RANK)),
                  const(wq2.shape), const(wkv.shape), const(wvt.shape), const(perm.shape), const(ones2.shape),
                  const((1, LANES)), const((1, LANES)), const((1, LANES)), const((1, LANES)), rope, rope],
        out_specs=[hm, hm, hm_t],
        out_shape=[jax.ShapeDtypeStruct((MLA_HEADS, geom.t, LANES), BF16),
                   jax.ShapeDtypeStruct((MLA_HEADS, geom.t, LANES), BF16),
                   jax.ShapeDtypeStruct((MLA_HEADS, geom.t // tm, MLA_VT_ROWS, tm), BF16)],
        compiler_params=_cparams(("parallel",)),
        name="mla_prep",
    )(cq, ckv, kr, gqn, gkvn, wq2, wkv, wvt, perm, ones2, gq_p, gq_s, gk_p, gk_s, rope_c, rope_s)


def _mla_attn_kernel(nk, q_ref, k_ref, vt_ref, o_ref, s_ref, mc_ref, p_ref, al_ref, acc_ref):
    tk = vt_ref.shape[3]
    tq = q_ref.shape[1]

    def stage_a(chunk, slot):
        off = chunk * tk
        if not isinstance(off, int):
            off = pl.multiple_of(off, tk)
        for hh in range(2):
            s = _dot_nt(k_ref[hh, pl.ds(off, tk), :], q_ref[hh])
            s_ref[slot, hh] = s
            mc_ref[slot, hh] = jnp.max(s, axis=0, keepdims=True)

    def stage_b(slot, m):
        m_out = []
        for hh in range(2):
            m_new = jnp.maximum(m[hh], mc_ref[slot, hh])
            al_ref[slot, hh] = jnp.exp2(m[hh] - m_new)
            p_ref[slot, hh] = jnp.exp2(s_ref[slot, hh] - m_new).astype(BF16)
            m_out.append(m_new)
        return tuple(m_out)

    def stage_c(chunk, slot):
        for hh in range(2):
            acc_ref[hh] = acc_ref[hh] * al_ref[slot, hh] + _dot(vt_ref[hh, chunk], p_ref[slot, hh])

    stage_a(0, 0)
    p_ref[1] = jnp.zeros(p_ref.shape[1:], BF16)
    al_ref[1] = jnp.ones(al_ref.shape[1:], F32)
    acc_ref[...] = jnp.zeros(acc_ref.shape, F32)
    m = (jnp.full((1, tq), NEG_INF, F32),) * 2

    def body(jj, m):
        j = 2 * jj
        stage_a(j + 1, 1)
        m = stage_b(0, m)
        stage_c(jnp.maximum(j - 1, 0), 1)
        stage_a(j + 2, 0)
        m = stage_b(1, m)
        stage_c(j, 0)
        return m

    m = lax.fori_loop(0, nk // 2 - 1, body, m)
    stage_a(nk - 1, 1)
    m = stage_b(0, m)
    stage_c(max(nk - 3, 0), 1)
    m = stage_b(1, m)
    stage_c(nk - 2, 0)
    stage_c(nk - 1, 1)
    o_t = jnp.concatenate([acc_ref[hh][:MLA_V, :] / acc_ref[hh][MLA_V:MLA_V + 1, :] for hh in range(2)], axis=0)
    o_ref[...] = o_t.T.astype(o_ref.dtype)


def _mla_attn_group(geom, tk, q, k, vt, out_prev, nseq, s, tok0):
    tq = min(1024, s)
    nq = s // tq
    nk = s // tk
    assert nk % 2 == 0
    qb0 = tok0 // tq
    sb0 = tok0 // s
    args = [q, k, vt]
    in_specs = [pl.BlockSpec((2, tq, LANES), lambda b, hp, i: (hp, qb0 + b * nq + i, 0)),
                pl.BlockSpec((2, s, LANES), lambda b, hp, i: (hp, sb0 + b, 0)),
                pl.BlockSpec((2, nk, MLA_VT_ROWS, tk), lambda b, hp, i: (hp, sb0 + b, 0, 0))]
    aliases = {}
    kern = functools.partial(_mla_attn_kernel, nk)
    if out_prev is not None:
        args.append(out_prev)
        in_specs.append(pl.BlockSpec(memory_space=pl.ANY))
        aliases = {3: 0}
        kern = lambda q_ref, k_ref, vt_ref, prev_ref, *rest, _k=kern: _k(q_ref, k_ref, vt_ref, *rest)
    return pl.pallas_call(
        kern,
        grid=(nseq, MLA_HEADS // 2, nq),
        in_specs=in_specs,
        out_specs=pl.BlockSpec((tq, LANES), lambda b, hp, i: (qb0 + b * nq + i, hp)),
        out_shape=jax.ShapeDtypeStruct((geom.t, MLA_HEADS * MLA_V), BF16),
        scratch_shapes=[pltpu.VMEM((2, 2, tk, tq), F32), pltpu.VMEM((2, 2, 1, tq), F32),
                        pltpu.VMEM((2, 2, tk, tq), BF16), pltpu.VMEM((2, 2, 1, tq), F32),
                        pltpu.VMEM((2, MLA_VT_ROWS, tq), F32)],
        input_output_aliases=aliases,
        compiler_params=_cparams(("parallel", "parallel", "arbitrary")),
        name="mla_attention",
    )(*args)


def _mla_attn(geom, tk, q, k, vt):
    assert geom.tp % geom.ss == 0
    out = _mla_attn_group(geom, tk, q, k, vt, None, geom.bp, geom.sp, 0)
    return _mla_attn_group(geom, tk, q, k, vt, out, geom.bs, geom.ss, geom.tp)


def _conv_kernel(geom, tc, up_ref, uc_ref, un_ref, w_ref, b_ref, gng_ref, gnb_ref, bd_ref, wpw_ref, bpw_ref,
                 o_ref, hs_ref):
    i = pl.program_id(0)
    first = geom.pos(i, tc) == 0
    last = geom.is_last(i, tc)

    def glu(u):
        u = u.astype(F32)
        return u[:, :CONV_CH] * jax.nn.sigmoid(u[:, CONV_CH:])

    hs_ref[0, 0:HALO, :] = jnp.where(first, 0.0, glu(up_ref[...]))
    hs_ref[0, HALO:HALO + tc, :] = glu(uc_ref[...])
    hs_ref[0, HALO + tc:HALO + tc + HALO, :] = jnp.where(last, 0.0, glu(un_ref[...]))
    span = tc + 2 * HALO - SUBLANES
    for sft in range(1, SUBLANES):
        hs_ref[sft, 0:span, :] = hs_ref[0, sft:sft + span, :]

    rows = 64 if tc % 64 == 0 else tc
    chunks = list(range(0, tc, rows))
    accs = {}
    for g0 in range(0, len(chunks), 2):
        group = chunks[g0:g0 + 2]
        for r0 in group:
            accs[r0] = jnp.zeros((rows, CONV_CH), F32) + b_ref[...]
        for j in range(CONV_WIDTH):
            for r0 in group:
                start = HALO + r0 + j - CONV_PAD
                sft = start % SUBLANES
                accs[r0] = accs[r0] + hs_ref[sft, start - sft:start - sft + rows, :] * w_ref[j:j + 1, :]
    for r0 in chunks:
        acc = accs[r0]
        mu = _split_dot(acc, bd_ref[...]) * (CONV_GROUPS / CONV_CH)
        d = acc - mu
        var = _split_dot(d * d, bd_ref[...]) * (CONV_GROUPS / CONV_CH)
        hn = d * lax.rsqrt(var + EPS) * gng_ref[...] + gnb_ref[...]
        y = _dot(_silu(hn).astype(BF16), wpw_ref[...]) + bpw_ref[...]
        o_ref[r0:r0 + rows, :] = y.astype(o_ref.dtype)


def _swa_conv_kernel(geom, nb, tc, *refs):
    n_swa = 13
    swa_in, conv_in = refs[:n_swa], refs[n_swa:n_swa + 10]
    oa_ref, oc_ref, hs_ref = refs[n_swa + 10:]
    _conv_kernel(geom, tc, *conv_in, oc_ref, hs_ref)
    _swa_kernel(geom, nb, *swa_in, oa_ref)


def _swa_conv(geom, q, k, v, gq, gk, bdq, bdk, sink, bias, uc, dw_w, dw_b, gn_g, gn_b, bd, w_pw2, b_pw2):
    ts = geom.tile(1024)
    nb = ts // BLOCK
    nblk = geom.t // BLOCK
    hb = ts // HALO
    nh = geom.t // HALO
    cur = lambda i: (i, 0)
    halo = lambda f: pl.BlockSpec((BLOCK, A_KV), f)
    prev = lambda i: (jnp.maximum(i * nb - 1, 0), 0)
    nxt = lambda i: (jnp.minimum((i + 1) * nb, nblk - 1), 0)
    const = lambda shape: pl.BlockSpec(shape, lambda i: (0,) * len(shape))
    return pl.pallas_call(
        functools.partial(_swa_conv_kernel, geom, nb, ts),
        grid=(geom.t // ts,),
        in_specs=[pl.BlockSpec((ts, A_Q), cur),
                  halo(prev), pl.BlockSpec((ts, A_KV), cur), halo(nxt),
                  halo(prev), pl.BlockSpec((ts, A_KV), cur), halo(nxt),
                  const((1, A_Q)), const((1, A_KV)), const((A_Q, A_Q)), const((A_KV, A_KV)),
                  const((SWA_KV_HEADS, 1, SWA_GROUP * BLOCK)), const(bias.shape),
                  pl.BlockSpec((HALO, 2 * CONV_CH), lambda i: (jnp.maximum(i * hb - 1, 0), 0)),
                  pl.BlockSpec((ts, 2 * CONV_CH), cur),
                  pl.BlockSpec((HALO, 2 * CONV_CH), lambda i: (jnp.minimum((i + 1) * hb, nh - 1), 0)),
                  const((32, CONV_CH)), const((1, CONV_CH)), const((1, CONV_CH)), const((1, CONV_CH)),
                  const((CONV_CH, CONV_CH)), const((CONV_CH, CONV_CH)), const((1, CONV_CH))],
        out_specs=[pl.BlockSpec((ts, A_Q), cur), pl.BlockSpec((ts, CONV_CH), cur)],
        out_shape=[jax.ShapeDtypeStruct((geom.t, A_Q), BF16), jax.ShapeDtypeStruct((geom.t, CONV_CH), BF16)],
        scratch_shapes=[pltpu.VMEM((SUBLANES, ts + 2 * HALO, CONV_CH), F32)],
        compiler_params=_cparams(("parallel",)),
        name="swa_conv",
    )(q, k, k, k, v, v, v, gq, gk, bdq, bdk, sink, bias, uc, uc, uc, dw_w, dw_b, gn_g, gn_b, bd, w_pw2, b_pw2)


def _rms_rows(x, g):
    return x * lax.rsqrt(jnp.mean(x * x, axis=-1, keepdims=True) + EPS) * g


def _store_slabs(ref, x, base=0):
    rows, half = x.shape[0], x.shape[1] // 2
    hi = pltpu.bitcast(x[:, :half].astype(BF16).astype(F32), U32)
    lo = pltpu.bitcast(x[:, half:].astype(BF16).astype(F32), U32)
    words = jnp.bitwise_or(hi, jnp.right_shift(lo, jnp.uint32(16)))
    for c in range(SLAB):
        ref[pl.ds(base + c, rows, stride=SLAB), :] = words[:, c * LANES:(c + 1) * LANES]


def _load_slabs(ref, rows, base=0):
    words = jnp.concatenate([ref[pl.ds(base + c, rows, stride=SLAB), :] for c in range(SLAB)], axis=1)
    hi = pltpu.bitcast(jnp.bitwise_and(words, jnp.uint32(0xFFFF0000)), F32)
    lo = pltpu.bitcast(jnp.left_shift(words, jnp.uint32(16)), F32)
    return jnp.concatenate([hi, lo], axis=1)


def _slab_row(ref, r):
    return ref.at[pl.ds(pl.multiple_of(r * SLAB, SLAB), SLAB)]


def _outproj_kernel(geom, n_x, *refs):
    x_refs = refs[:n_x]
    (oa_ref, ob_ref, oc_ref, ga_ref, gb_ref, gc_ref, w_ref, gate_ref, sh_ref, sc_ref,
     g2_ref, wr_hi_ref, br_ref, x1_ref, h2_ref, lg_ref) = refs[n_x:]
    na = _rms_rows(oa_ref[...].astype(F32), ga_ref[...]).astype(BF16)
    nb = _rms_rows(ob_ref[...].astype(F32), gb_ref[...]).astype(BF16)
    nc = _rms_rows(oc_ref[...].astype(F32), gc_ref[...]).astype(BF16)
    wa = A_Q
    wb = wa + MLA_HEADS * MLA_V
    y = _dot(na, w_ref[0:wa, :]) + _dot(nb, w_ref[wa:wb, :]) + _dot(nc, w_ref[wb:, :])
    x1 = _x_tile(geom, x_refs) + gate_ref[...] * y
    x1_ref[...] = x1
    h2 = _rms_rows(x1, g2_ref[...]) * (1.0 + sc_ref[...]) + sh_ref[...]
    hi = h2.astype(BF16)
    lo = (h2 - hi.astype(F32)).astype(BF16)
    _store_slabs(h2_ref, h2)
    lg2 = _dot(hi, wr_hi_ref[...])
    lg = lg2[:, :LANES] + lg2[:, LANES:] + _dot(lo, wr_hi_ref[:, :LANES]) + br_ref[...]
    lg_ref[...] = lg.T


def _outproj(geom, x_parts, oa, ob, oc, ga, gb, gc, w_out, mods, g2, wr2, br):
    tm = geom.tile(512)
    tok = lambda w: pl.BlockSpec((tm, w), lambda i: (i, 0))
    const = lambda shape: pl.BlockSpec(shape, lambda i: (0, 0))
    wb = MLA_HEADS * MLA_V
    return pl.pallas_call(
        functools.partial(_outproj_kernel, geom, len(x_parts)),
        grid=(geom.t // tm,),
        in_specs=_x_specs(geom, tm, x_parts) + [tok(A_Q), tok(wb), tok(CONV_CH),
                  const((1, A_Q)), const((1, wb)), const((1, CONV_CH)),
                  const((D_MODEL, D_MODEL)),
                  _mod_spec(geom, tm, 2), _mod_spec(geom, tm, 3), _mod_spec(geom, tm, 4),
                  const((1, D_MODEL)), const((D_MODEL, 2 * LANES)), const((1, LANES))],
        out_specs=[tok(D_MODEL), pl.BlockSpec((tm * SLAB, LANES), lambda i: (i, 0)),
                   pl.BlockSpec((LANES, tm), lambda i: (0, i))],
        out_shape=[jax.ShapeDtypeStruct((geom.t, D_MODEL), F32),
                   jax.ShapeDtypeStruct((geom.t * SLAB, LANES), U32),
                   jax.ShapeDtypeStruct((LANES, geom.t), F32)],
        compiler_params=_cparams(("parallel",)),
        name="merge_outproj_norm2",
    )(*x_parts, oa, ob, oc, ga, gb, gc, w_out, mods, mods, mods, g2, wr2, br)


ROUTE_ROWS = 40


def _route_kernel(lg_ref, sel_ref, rt_ref, cnt_ref):
    lg = lg_ref[...]
    row = lax.broadcasted_iota(jnp.int32, lg.shape, 0)
    rowf = row.astype(F32)
    big = float(2 * LANES)
    is_g = jnp.logical_and(row >= N_EXPERTS, row < N_EXPERTS + N_GROUPS)
    gl = jnp.where(is_g, lg, NEG_INF)
    gmax = jnp.max(gl, axis=0, keepdims=True)
    gsum = jnp.sum(jnp.exp(gl - gmax), axis=0, keepdims=True)
    g_val = 1.0 / gsum
    g_idx = jnp.min(jnp.where(jnp.logical_and(is_g, gl == gmax), rowf, big), axis=0, keepdims=True) - N_EXPERTS
    lo = g_idx * EXPERTS_PER_GROUP
    is_e = jnp.logical_and(rowf >= lo, rowf < lo + EXPERTS_PER_GROUP)
    el = jnp.where(is_e, lg, NEG_INF)
    emax = jnp.max(el, axis=0, keepdims=True)
    ee = jnp.exp(el - emax)
    e_prob = ee / jnp.sum(ee, axis=0, keepdims=True)
    p1 = jnp.where(is_e, e_prob, -1.0)
    v1 = jnp.max(p1, axis=0, keepdims=True)
    i1 = jnp.min(jnp.where(p1 == v1, rowf, big), axis=0, keepdims=True)
    p2 = jnp.where(rowf == i1, -1.0, p1)
    v2 = jnp.max(p2, axis=0, keepdims=True)
    i2 = jnp.min(jnp.where(p2 == v2, rowf, big), axis=0, keepdims=True)
    scale = g_val / (v1 + v2)
    r8 = lax.broadcasted_iota(jnp.int32, rt_ref.shape, 0)
    rt_ref[...] = (jnp.where(r8 == 0, i1, 0.0) + jnp.where(r8 == 1, i2, 0.0) +
                   jnp.where(r8 == 2, v1 * scale, 0.0) + jnp.where(r8 == 3, v2 * scale, 0.0))
    chosen = jnp.where(jnp.logical_or(rowf == i1, rowf == i2), 1.0, 0.0).astype(BF16)
    cnt_ref[...] = _dot(chosen, sel_ref[...])


def _route(geom, tr, logits_t, sel):
    n = geom.t // tr
    return pl.pallas_call(
        _route_kernel,
        grid=(n,),
        in_specs=[pl.BlockSpec((ROUTE_ROWS, tr), lambda i: (0, i)), pl.BlockSpec(sel.shape, lambda i: (0, 0))],
        out_specs=[pl.BlockSpec((SUBLANES, tr), lambda i: (0, i)),
                   pl.BlockSpec((None, ROUTE_ROWS, LANES), lambda i: (i, 0, 0))],
        out_shape=[jax.ShapeDtypeStruct((SUBLANES, geom.t), F32), jax.ShapeDtypeStruct((n, ROUTE_ROWS, LANES), F32)],
        compiler_params=_cparams(("parallel",)),
        name="moe_route",
    )(logits_t, sel)


def _slots_kernel(tm, rt_ref, st_ref, tri_ref, sl_ref):
    rt = rt_ref[...]
    rowf = lax.broadcasted_iota(jnp.int32, (ROUTE_ROWS, rt.shape[1]), 0).astype(F32)
    oh1 = rowf == rt[0:1, :]
    oh2 = rowf == rt[1:2, :]
    chosen = jnp.where(jnp.logical_or(oh1, oh2), 1.0, 0.0).astype(BF16)
    for j in range(rt.shape[1] // tm):
        cols = slice(j * tm, (j + 1) * tm)
        first = _dot(chosen[:, cols], tri_ref[...]) + st_ref[:, j:j + 1]
        s1 = jnp.sum(jnp.where(oh1[:, cols], first, 0.0), axis=0, keepdims=True)
        s2 = jnp.sum(jnp.where(oh2[:, cols], first, 0.0), axis=0, keepdims=True)
        sl_ref[j] = jnp.concatenate([s1, s2], axis=1).astype(jnp.int32)


def _slots(geom, tr, tm, rt_t, start3, tri):
    ns = tr // tm
    return pl.pallas_call(
        functools.partial(_slots_kernel, tm),
        grid=(geom.t // tr,),
        in_specs=[pl.BlockSpec((SUBLANES, tr), lambda i: (0, i)),
                  pl.BlockSpec((None, ROUTE_ROWS, LANES), lambda i: (i, 0, 0)),
                  pl.BlockSpec((tm, tm), lambda i: (0, 0))],
        out_specs=pl.BlockSpec((ns, 1, 2 * tm), lambda i: (i, 0, 0)),
        out_shape=jax.ShapeDtypeStruct((geom.t // tm, 1, 2 * tm), jnp.int32),
        compiler_params=_cparams(("parallel",)),
        name="moe_slots",
    )(rt_t, start3, tri)


def _plan(cnt3, ns, n_tiles_max):
    n_steps = cnt3.shape[0]
    cnt_t = cnt3[:, :, :ns].transpose(0, 2, 1).reshape(n_steps * ns, ROUTE_ROWS)
    incl = jnp.cumsum(cnt_t, axis=0)
    tiles_e = jnp.ceil(incl[-1] / TME)
    end_e = jnp.cumsum(tiles_e)
    start = incl - cnt_t + ((end_e - tiles_e) * TME)[None, :]
    start3 = jnp.pad(start.reshape(n_steps, ns, ROUTE_ROWS).transpose(0, 2, 1), ((0, 0), (0, 0), (0, LANES - ns)))
    n_used = end_e[-1].astype(jnp.int32)
    tile_idx = jnp.minimum(jnp.arange(n_tiles_max, dtype=jnp.int32), n_used - 1)
    tile_exp = jnp.sum(tile_idx[:, None] >= end_e[None, :N_EXPERTS].astype(jnp.int32), axis=1).astype(jnp.int32)
    return start3, tile_exp, tile_idx, n_used.reshape(1)


def _scatter_kernel(sl_ref, h_ref, xs_in, xs_out, sem):
    del xs_in
    tm = h_ref.shape[0] // SLAB

    def body(rr, carry):
        for j in range(ROW_UNROLL):
            r = rr * ROW_UNROLL + j
            pltpu.make_async_copy(_slab_row(h_ref, r), _slab_row(xs_out, sl_ref[0, r]), sem).start(priority=0)
            pltpu.make_async_copy(_slab_row(h_ref, r), _slab_row(xs_out, sl_ref[0, tm + r]), sem).start(priority=1)
        return carry

    lax.fori_loop(0, tm // ROW_UNROLL, body, 0)
    for _ in range(2):
        pltpu.make_async_copy(h_ref, xs_out.at[pl.ds(0, tm * SLAB)], sem).wait()


def _scatter(geom, tm, slots_smem, h2, xs):
    return pl.pallas_call(
        _scatter_kernel,
        grid=(geom.t // tm,),
        in_specs=[pl.BlockSpec((None, 1, 2 * tm), lambda i: (i, 0, 0), memory_space=pltpu.SMEM),
                  pl.BlockSpec((tm * SLAB, LANES), lambda i: (i, 0)),
                  pl.BlockSpec(memory_space=pl.ANY)],
        out_specs=pl.BlockSpec(memory_space=pl.ANY),
        out_shape=jax.ShapeDtypeStruct(xs.shape, xs.dtype),
        scratch_shapes=[pltpu.SemaphoreType.DMA(())],
        input_output_aliases={2: 0},
        compiler_params=_cparams(("arbitrary",)),
        name="moe_scatter",
    )(slots_smem, h2, xs)


def _experts_kernel(te_ref, ti_ref, nu_ref, xs_ref, wg_ref, wu_ref, wd_ref, ys_ref):
    del te_ref, ti_ref

    @pl.when(pl.program_id(0) < nu_ref[0])
    def _():
        x = _load_slabs(xs_ref, TME).astype(BF16)
        act = (_silu(_dot(x, wg_ref[...].astype(BF16))) * _dot(x, wu_ref[...].astype(BF16))).astype(BF16)
        _store_slabs(ys_ref, _dot(act, wd_ref[...].astype(BF16)))


def _experts(layer, xs, tile_exp, tile_idx, n_used, wg, wu, wd):
    n_tiles_max = xs.shape[0] // (TME * SLAB)
    slab = pl.BlockSpec((TME * SLAB, LANES), lambda n, te, ti, nu: (ti[n], 0))
    wspec = lambda a, b: pl.BlockSpec((None, None, a, b), lambda n, te, ti, nu: (layer, te[n], 0, 0))
    return pl.pallas_call(
        _experts_kernel,
        grid_spec=pltpu.PrefetchScalarGridSpec(
            num_scalar_prefetch=3,
            grid=(n_tiles_max,),
            in_specs=[slab, wspec(D_MODEL, D_EXPERT), wspec(D_MODEL, D_EXPERT), wspec(D_EXPERT, D_MODEL)],
            out_specs=slab),
        out_shape=jax.ShapeDtypeStruct(xs.shape, U32),
        compiler_params=_cparams(("arbitrary",)),
        name="moe_experts",
    )(tile_exp, tile_idx, n_used, xs, wg, wu, wd)


def _combine_kernel(geom, final, sl_cur, sl_nxt, rt_ref, x1_ref, gate_ref, ys_hbm, *refs):
    buf1, buf2, sem = refs[-3:]
    i = pl.program_id(0)
    tm = x1_ref.shape[0]

    def issue_row(sl_ref, slot, r):
        dst = slot * tm + r
        pltpu.make_async_copy(_slab_row(ys_hbm, sl_ref[0, r]), _slab_row(buf1, dst), sem.at[slot]).start(priority=0)
        pltpu.make_async_copy(_slab_row(ys_hbm, sl_ref[0, tm + r]), _slab_row(buf2, dst),
                              sem.at[slot]).start(priority=1)

    def wait_slot(slot):
        base = pl.multiple_of(slot * (tm * SLAB), tm * SLAB)
        for buf in (buf1, buf2):
            pltpu.make_async_copy(ys_hbm.at[pl.ds(0, tm * SLAB)], buf.at[pl.ds(base, tm * SLAB)],
                                  sem.at[slot]).wait()
        return base

    @pl.when(i == 0)
    def _():
        def body(rr, carry):
            for j in range(ROW_UNROLL):
                issue_row(sl_cur, 0, rr * ROW_UNROLL + j)
            return carry
        lax.fori_loop(0, tm // ROW_UNROLL, body, 0)

    slot = i % 2
    base = wait_slot(slot)
    def issue_part(k, n):
        for r in range(k * tm // n, (k + 1) * tm // n):
            issue_row(sl_nxt, 1 - slot, r)

    w = jnp.concatenate([rt_ref[...], jnp.zeros((LANES - SUBLANES, tm), F32)], axis=0).T
    y = w[:, 2:3] * _load_slabs(buf1, tm, base) + w[:, 3:4] * _load_slabs(buf2, tm, base)
    x_new = x1_ref[...] + gate_ref[...] * y
    if final:
        issue_part(0, 1)
        out_p, out_s = refs[:2]
        n_prompt = geom.tp // tm

        @pl.when(i < n_prompt)
        def _():
            out_p[...] = x_new

        @pl.when(i >= n_prompt)
        def _():
            out_s[...] = x_new
    else:
        sh_ref, sc_ref, g_ref, w_ref, o_ref = refs[:5]
        o_ref[...] = x_new
        _norm1_project(x_new, sh_ref, sc_ref, g_ref, w_ref, refs[5:-3], after_weight_load=issue_part)

    @pl.when(i == pl.num_programs(0) - 1)
    def _():
        wait_slot(1 - slot)


def _combine(geom, tm, slots_smem, rt, x1, mods, ys, nxt=None):
    n = geom.t // tm
    n_prompt = geom.tp // tm
    tok = lambda w: pl.BlockSpec((tm, w), lambda i: (i, 0))
    smem = lambda f: pl.BlockSpec((None, 1, 2 * tm), f, memory_space=pltpu.SMEM)
    args = [slots_smem, slots_smem, rt, x1, mods, ys]
    in_specs = [smem(lambda i: (i, 0, 0)), smem(lambda i: (jnp.minimum(i + 1, n - 1), 0, 0)),
                pl.BlockSpec((SUBLANES, tm), lambda i: (0, i)), tok(D_MODEL), _mod_spec(geom, tm, 5),
                pl.BlockSpec(memory_space=pl.ANY)]
    if nxt is None:
        out_specs = [pl.BlockSpec((tm, D_MODEL), lambda i: (jnp.minimum(i, n_prompt - 1), 0)),
                     pl.BlockSpec((tm, D_MODEL), lambda i: (jnp.maximum(i - n_prompt, 0), 0))]
        out_shape = [jax.ShapeDtypeStruct((geom.tp, D_MODEL), F32),
                     jax.ShapeDtypeStruct((geom.t - geom.tp, D_MODEL), F32)]
    else:
        mods_n, g1_n, w_in_n = nxt
        args += [mods_n, mods_n, g1_n, w_in_n]
        in_specs += [_mod_spec(geom, tm, 0), _mod_spec(geom, tm, 1), pl.BlockSpec((1, D_MODEL), lambda i: (0, 0)),
                     pl.BlockSpec(w_in_n.shape, lambda i: (0, 0))]
        out_specs = [tok(D_MODEL)] + [tok(w) for _, w in _IN_SEGS]
        out_shape = [jax.ShapeDtypeStruct((geom.t, D_MODEL), F32)] + \
            [jax.ShapeDtypeStruct((geom.t, w), BF16) for _, w in _IN_SEGS]
    return pl.pallas_call(
        functools.partial(_combine_kernel, geom, nxt is None),
        grid=(n,),
        in_specs=in_specs,
        out_specs=out_specs,
        out_shape=out_shape,
        scratch_shapes=[pltpu.VMEM((2 * tm * SLAB, LANES), U32), pltpu.VMEM((2 * tm * SLAB, LANES), U32),
                        pltpu.SemaphoreType.DMA((2,))],
        compiler_params=_cparams(("arbitrary",)),
        name="moe_combine",
    )(*args)


def _block_diag_ones(n, blk):
    idx = np.arange(n) // blk
    return jnp.asarray((idx[:, None] == idx[None, :]).astype(np.float32), dtype=BF16)


def _rope_tables(smax):
    inv_freq = 1.0 / (ROPE_BASE ** (jnp.arange(0, MLA_ROPE, 2, dtype=F32) / MLA_ROPE))
    ang = jnp.arange(smax, dtype=F32)[:, None] * inv_freq[None, :]
    cos, sin = jnp.cos(ang), jnp.sin(ang)
    zeros = lambda w: jnp.zeros((smax, w), F32)
    pad = LANES - MLA_QK
    c = jnp.concatenate([jnp.ones((smax, MLA_NOPE), F32), cos, cos, zeros(pad)], axis=1)
    s = jnp.concatenate([zeros(MLA_NOPE), -sin, sin, zeros(pad)], axis=1)
    return c, s


def _pad_cols(w, n):
    return jnp.pad(w, ((0, 0), (0, n - w.shape[1])))


def _layer_weights(l, w_in, mla_w_q_up, mla_w_kv_up, mla_q_gain, mla_k_gain, swa_q_gain, swa_k_gain, swa_sink,
                   moe_w_group, moe_b_group, moe_w_expert, moe_b_expert):
    wi = w_in[l]
    kr = jnp.pad(wi[:, A_IN + MLA_Q_RANK + MLA_KV_RANK:A_IN + B_IN], ((0, 0), (MLA_NOPE, LANES - MLA_QK)))
    w_in_p = jnp.concatenate([wi[:, :A_IN], wi[:, A_IN:A_IN + MLA_Q_RANK + MLA_KV_RANK], kr, wi[:, A_IN + B_IN:]],
                             axis=1).astype(BF16)
    wq = mla_w_q_up[l].reshape(MLA_Q_RANK, MLA_HEADS, MLA_QK)
    wq_p = jnp.pad(wq, ((0, 0), (0, 0), (0, LANES - MLA_QK)))
    wq2 = jnp.concatenate([wq_p, _swap_rope_halves(wq_p)], axis=2).reshape(MLA_Q_RANK, MLA_HEADS * 2 * LANES).astype(BF16)
    wvt = mla_w_kv_up[l].reshape(MLA_KV_RANK, MLA_HEADS, MLA_NOPE + MLA_V)[:, :, MLA_NOPE:].transpose(1, 2, 0).astype(BF16)
    gq_p = _pad_cols(mla_q_gain[l][None, :], LANES)
    gk_p = _pad_cols(mla_k_gain[l][None, :], LANES)
    gq_s, gk_s = _swap_rope_halves(gq_p), _swap_rope_halves(gk_p)
    gq_a = jnp.tile(swa_q_gain[l], SWA_HEADS)[None, :]
    gk_a = jnp.tile(swa_k_gain[l], SWA_KV_HEADS)[None, :]
    sink = jnp.repeat(swa_sink[l] * LOG2E, BLOCK).reshape(SWA_KV_HEADS, 1, SWA_GROUP * BLOCK)
    wr = _pad_cols(jnp.concatenate([moe_w_expert[l], moe_w_group[l]], axis=1), LANES)
    wr_hi = wr.astype(BF16)
    wr2 = jnp.concatenate([wr_hi, (wr - wr_hi.astype(F32)).astype(BF16)], axis=1)
    br = _pad_cols(jnp.concatenate([moe_b_expert[l], moe_b_group[l]])[None, :], LANES)
    return w_in_p, wq2, wvt, gq_p, gq_s, gk_p, gk_s, gq_a, gk_a, sink, wr2, br


def kernel(x_prompt, x_sample, c_prompt, c_sample, w_ada, b_ada, norm1_g, norm2_g, w_in, swa_q_gain, swa_k_gain, swa_sink, mla_q_norm_g, mla_w_q_up, mla_kv_norm_g, mla_w_kv_up, mla_q_gain, mla_k_gain, conv_dw_w, conv_dw_b, conv_gn_g, conv_gn_b, conv_w_pw2, conv_b_pw2, out_norm_a, out_norm_b, out_norm_c, w_out, moe_w_group, moe_b_group, moe_w_expert, moe_b_expert, moe_w_gate, moe_w_up, moe_w_down):
    bp, sp, d = x_prompt.shape
    bs, ss, _ = x_sample.shape
    assert d == D_MODEL
    geom = _Geom(bp, sp, bs, ss)
    depth = w_ada.shape[0]

    x_parts = (x_prompt.reshape(bp * sp, d), x_sample.reshape(bs * ss, d))
    c = jnp.concatenate([c_prompt, c_sample], axis=0)
    rows = -(-geom.nb // 8) * 8
    c_pad = jnp.pad(c, ((0, rows - geom.nb), (0, 0)))
    mods_all = _modulation(c_pad, w_ada, b_ada)

    rope_c, rope_s = _rope_tables(max(sp, ss))
    rope_perm = _swap_rope_halves(jnp.eye(LANES, dtype=F32)).astype(BF16)
    ones2 = _block_diag_ones(2 * LANES, LANES)
    bdq = _block_diag_ones(A_Q, HEAD_DIM)
    bdk = _block_diag_ones(A_KV, HEAD_DIM)
    bdc = _block_diag_ones(CONV_CH, CONV_CH // CONV_GROUPS)
    swa_bias = _swa_bias()
    row = lambda v: v[None, :]
    tm_moe = geom.tile(256)
    n_tiles_max = 2 * geom.t // TME + N_EXPERTS
    tr_moe = geom.tile(2048)
    tri = jnp.asarray(np.triu(np.ones((tm_moe, tm_moe), np.float32), 1), dtype=BF16)
    sel = jnp.asarray(np.arange(tr_moe)[:, None] // tm_moe == np.arange(LANES)[None, :], dtype=BF16)
    xs = jnp.zeros((n_tiles_max * TME * SLAB, LANES), U32)

    weights = [_layer_weights(l, w_in, mla_w_q_up, mla_w_kv_up, mla_q_gain, mla_k_gain, swa_q_gain, swa_k_gain,
                              swa_sink, moe_w_group, moe_b_group, moe_w_expert, moe_b_expert) for l in range(depth)]
    mods_l = [mods_all[l, :geom.nb].reshape(geom.nb * N_MOD, 1, D_MODEL) for l in range(depth)]
    proj = _inproj(geom, x_parts, mods_l[0], row(norm1_g[0]), weights[0][0])

    for l in range(depth):
        (w_in_p, wq2, wvt, gq_p, gq_s, gk_p, gk_s, gq_a, gk_a, sink, wr2, br) = weights[l]
        mods = mods_l[l]
        q_a, k_a, v_a, cq, ckv, kr, uc = proj
        tk = geom.tile(512)
        q_b, k_b, vt_b = _mla_prep(geom, tk, cq, ckv, kr, row(mla_q_norm_g[l]), row(mla_kv_norm_g[l]), wq2,
                                   mla_w_kv_up[l].astype(BF16), wvt, rope_perm, ones2, gq_p, gq_s, gk_p, gk_s,
                                   rope_c, rope_s)
        out_b = _mla_attn(geom, tk, q_b, k_b, vt_b)
        dw_w = jnp.pad(conv_dw_w[l].reshape(CONV_WIDTH, CONV_CH), ((0, 32 - CONV_WIDTH), (0, 0)))
        out_a, out_c = _swa_conv(geom, q_a, k_a, v_a, gq_a, gk_a, bdq, bdk, sink, swa_bias, uc, dw_w,
                                 row(conv_dw_b[l]), row(conv_gn_g[l]), row(conv_gn_b[l]), bdc,
                                 conv_w_pw2[l].astype(BF16), row(conv_b_pw2[l]))
        x1, h2, logits = _outproj(geom, x_parts, out_a, out_b, out_c, row(out_norm_a[l]), row(out_norm_b[l]),
                                  row(out_norm_c[l]), w_out[l].astype(BF16), mods, row(norm2_g[l]),
                                  wr2, br)
        rt, cnt3 = _route(geom, tr_moe, logits, sel)
        start3, tile_exp, tile_idx, n_used = _plan(cnt3, tr_moe // tm_moe, n_tiles_max)
        slots = _slots(geom, tr_moe, tm_moe, rt, start3, tri)
        xs = _scatter(geom, tm_moe, slots, h2, xs)
        ys = _experts(l, xs, tile_exp, tile_idx, n_used, moe_w_gate, moe_w_up, moe_w_down)
        if l + 1 < depth:
            x_new, *proj = _combine(geom, tm_moe, slots, rt, x1, mods, ys,
                                nxt=(mods_l[l + 1], row(norm1_g[l + 1]), weights[l + 1][0]))
            x_parts = (x_new,)
        else:
            y_prompt, y_sample = _combine(geom, tm_moe, slots, rt, x1, mods, ys)

    return (y_prompt.reshape(bp, sp, d), y_sample.reshape(bs, ss, d))
```
